```python
import jax, jax.numpy as jnp
from jax import lax
import numpy as np

D_MODEL = 1024
BATCH = 4
SEQ = 4096
DEPTH = 1

CTX_LEN = 256
GRID_W = 64
MLA_HEADS = 16
QK_NOPE = 64
QK_ROPE = 32
V_DIM = 64
Q_LORA = 256
KV_LORA = 128
ROPE_PAIRS = QK_ROPE // 4
ROPE_BASE = 10000.0
Q_BLOCK = 128
ATTN_SCALE = (QK_NOPE + QK_ROPE) ** -0.5
D_INNER = 2 * D_MODEL
SSM_HEAD_DIM = 64
SSM_HEADS = D_INNER // SSM_HEAD_DIM
SSM_GROUPS = 8
D_STATE = 128
D_CONV = 5
CHUNK = 128
CONV_DIM = D_INNER + 2 * SSM_GROUPS * D_STATE
D_FF = 4 * D_MODEL
N_MOD = 6
EPS = 1e-6
IN_SPLITS = (Q_LORA, KV_LORA, QK_ROPE, D_INNER, CONV_DIM, 2 * SSM_HEADS, D_MODEL, D_MODEL)
D_IN_PROJ = Q_LORA + KV_LORA + QK_ROPE + D_INNER + CONV_DIM + 2 * SSM_HEADS + 2 * D_MODEL

kernel_name = "hybrid_mla_ssd_dit_block"


def rmsnorm(u, g):
    uf = u.astype(jnp.float32)
    uf = uf * lax.rsqrt(jnp.mean(uf * uf, axis=-1, keepdims=True) + EPS)
    return uf.astype(u.dtype) * g


def modulate(h, shift, scale):
    return h * (1 + scale) + shift


def apply_axial_rope(u, cos, sin):
    us = u.reshape(u.shape[:-1] + (2, 2, ROPE_PAIRS))
    u1, u2 = us[..., 0, :], us[..., 1, :]
    out = jnp.stack([u1 * cos - u2 * sin, u2 * cos + u1 * sin], axis=-2)
    return out.reshape(u.shape)


def depthwise_conv(u, w, b):
    pad = (D_CONV - 1) // 2
    out = lax.conv_general_dilated(u, w[:, None, :], (1,), [(pad, pad)],
                                   dimension_numbers=('NWC', 'WIO', 'NWC'),
                                   feature_group_count=u.shape[-1])
    return out + b


def attend(q_nope, q_rope, k_nope, k_rope, v):
    b, n, h, _ = q_nope.shape
    nb = n // Q_BLOCK

    def to_blocks(t):
        return jnp.moveaxis(t.reshape((b, nb, Q_BLOCK) + t.shape[2:]), 1, 0)

    def one_block(qs):
        qn, qr = qs
        s = jnp.einsum('bqhd,bkhd->bhqk', qn, k_nope) + jnp.einsum('bqhr,bkr->bhqk', qr, k_rope)
        p = jax.nn.softmax(s.astype(jnp.float32) * ATTN_SCALE, axis=-1).astype(v.dtype)
        return jnp.einsum('bhqk,bkhd->bqhd', p, v)

    out = lax.map(one_block, (to_blocks(q_nope), to_blocks(q_rope)))
    return jnp.moveaxis(out, 0, 1).reshape(b, n, h * V_DIM)


def ssd_chunked(xs, dt, a, bm, cm, h0):
    f32 = jnp.float32
    b, n, h, p = xs.shape
    g, d_state = bm.shape[-2:]
    r = h // g
    nc = n // CHUNK
    x = xs.astype(f32).reshape(b, nc, CHUNK, g, r, p)
    dtc = dt.astype(f32).reshape(b, nc, CHUNK, g, r)
    a_cum = jnp.cumsum(dtc * a.astype(f32).reshape(g, r), axis=2)
    dtx = x * dtc[..., None]
    bc = bm.astype(f32).reshape(b, nc, CHUNK, g, d_state)
    cc = cm.astype(f32).reshape(b, nc, CHUNK, g, d_state)
    lower = jnp.tril(jnp.ones((CHUNK, CHUNK), bool))[:, :, None, None]
    seg = a_cum[:, :, :, None] - a_cum[:, :, None, :]
    decay = jnp.exp(jnp.where(lower, seg, -jnp.inf))
    cb = jnp.einsum('bcign,bcjgn->bcijg', cc, bc)
    y_diag = jnp.einsum('bcijgr,bcjgrp->bcigrp', cb[..., None] * decay, dtx)
    to_end = jnp.exp(a_cum[:, :, -1:] - a_cum)
    states = jnp.einsum('bcjgn,bcjgrp->bcgrpn', bc, dtx * to_end[..., None])
    chunk_decay = jnp.exp(a_cum[:, :, -1])

    def step(hs, inp):
        s, d = inp
        return hs * d[..., None, None] + s, hs

    h_last, h_before = lax.scan(step, h0.astype(f32),
                                (jnp.moveaxis(states, 1, 0), jnp.moveaxis(chunk_decay, 1, 0)))
    h_before = jnp.moveaxis(h_before, 0, 1)
    y_off = jnp.einsum('bcign,bcgrpn->bcigrp', cc, h_before) * jnp.exp(a_cum)[..., None]
    y = (y_diag + y_off).reshape(b, n, h, p)
    return y.astype(xs.dtype), h_last


def ssd_bidir(xs, dt, a, bm, cm, d_skip, h0_f, h0_b):
    y_f, h_f = ssd_chunked(xs, dt[:, :, 0], a[0], bm, cm, h0_f)
    flip = lambda t: jnp.flip(t, axis=1)
    y_b, h_b = ssd_chunked(flip(xs), flip(dt[:, :, 1]), a[1], flip(bm), flip(cm), h0_b)
    y = y_f + flip(y_b) + (d_skip[0] + d_skip[1])[:, None] * xs
    return y, h_f, h_b


def front(h, w_in, q_a_g, w_uq, kv_a_g, w_ukv, qk_q_g, qk_k_g, conv_w, conv_b, dt_bias):
    b, n, _ = h.shape
    proj = h @ w_in
    idx = np.cumsum(IN_SPLITS)[:-1].tolist()
    cq, ckv, k_rope, z, xbc, dt_raw, ga, gb = jnp.split(proj, idx, axis=-1)
    q = (rmsnorm(cq, q_a_g) @ w_uq).reshape(b, n, MLA_HEADS, QK_NOPE + QK_ROPE)
    kv = (rmsnorm(ckv, kv_a_g) @ w_ukv).reshape(b, n, MLA_HEADS, QK_NOPE + V_DIM)
    q_nope = rmsnorm(q[..., :QK_NOPE], qk_q_g[:QK_NOPE])
    q_rope = rmsnorm(q[..., QK_NOPE:], qk_q_g[QK_NOPE:])
    k_nope = rmsnorm(kv[..., :QK_NOPE], qk_k_g[:QK_NOPE])
    v = kv[..., QK_NOPE:]
    k_rope = rmsnorm(k_rope, qk_k_g[QK_NOPE:])
    xbc = jax.nn.silu(depthwise_conv(xbc, conv_w, conv_b))
    xs, bm, cm = jnp.split(xbc, [D_INNER, D_INNER + SSM_GROUPS * D_STATE], axis=-1)
    xs = xs.reshape(b, n, SSM_HEADS, SSM_HEAD_DIM)
    bm = bm.reshape(b, n, SSM_GROUPS, D_STATE)
    cm = cm.reshape(b, n, SSM_GROUPS, D_STATE)
    dt = jax.nn.softplus(dt_raw.reshape(b, n, 2, SSM_HEADS).astype(jnp.float32) + dt_bias)
    return q_nope, q_rope, k_nope, k_rope, v, z, xs, bm, cm, dt, ga, gb


def merge_branches(att, y_ssm, z, ga, gb, ssm_norm_g, w_proj_a, w_proj_b, w_out):
    b, n = att.shape[:2]
    u = y_ssm.reshape(b, n, D_INNER) * jax.nn.silu(z)
    u = rmsnorm(u.reshape(b, n, SSM_GROUPS, D_INNER // SSM_GROUPS),
                ssm_norm_g.reshape(SSM_GROUPS, D_INNER // SSM_GROUPS)).reshape(b, n, D_INNER)
    merged = jax.nn.sigmoid(ga) * (att @ w_proj_a) + jax.nn.sigmoid(gb) * (u @ w_proj_b)
    return merged @ w_out


def sq_relu_mlp(h, w1, w2):
    return jnp.square(jax.nn.relu(h @ w1)) @ w2


def setup_inputs(seed: int = 0) -> dict:
    key = jax.random.key(seed)
    ks = jax.random.split(key, 32)
    f32 = jnp.float32

    def dense(k, fan_in, shape, gain=1.0):
        return gain * fan_in ** -0.5 * jax.random.normal(k, shape, f32)

    def gain_init(k, shape):
        return 1.0 + 0.02 * jax.random.normal(k, shape, f32)

    dt0 = jnp.exp(jax.random.uniform(ks[20], (DEPTH, 2, SSM_HEADS), f32, np.log(1e-3), np.log(1e-1)))
    return {
        "x": jax.random.normal(ks[0], (BATCH, SEQ, D_MODEL), f32),
        "c": jax.random.normal(ks[1], (BATCH, D_MODEL), f32),
        "ctx": jax.random.normal(ks[2], (BATCH, CTX_LEN, D_MODEL), f32),
        "c_ctx": jax.random.normal(ks[3], (D_MODEL,), f32),
        "norm1_g": gain_init(ks[4], (DEPTH, D_MODEL)),
        "norm2_g": gain_init(ks[5], (DEPTH, D_MODEL)),
        "w_mod": dense(ks[6], D_MODEL, (DEPTH, D_MODEL, N_MOD * D_MODEL), 0.5),
        "b_mod": 0.01 * jax.random.normal(ks[7], (DEPTH, N_MOD * D_MODEL), f32),
        "w_in": dense(ks[8], D_MODEL, (DEPTH, D_MODEL, D_IN_PROJ)),
        "q_a_g": gain_init(ks[9], (DEPTH, Q_LORA)),
        "w_uq": dense(ks[10], Q_LORA, (DEPTH, Q_LORA, MLA_HEADS * (QK_NOPE + QK_ROPE))),
        "kv_a_g": gain_init(ks[11], (DEPTH, KV_LORA)),
        "w_ukv": dense(ks[12], KV_LORA, (DEPTH, KV_LORA, MLA_HEADS * (QK_NOPE + V_DIM))),
        "qk_q_g": gain_init(ks[13], (DEPTH, QK_NOPE + QK_ROPE)),
        "qk_k_g": gain_init(ks[14], (DEPTH, QK_NOPE + QK_ROPE)),
        "conv_w": dense(ks[15], D_CONV, (DEPTH, D_CONV, CONV_DIM)),
        "conv_b": 0.01 * jax.random.normal(ks[16], (DEPTH, CONV_DIM), f32),
        "dt_bias": dt0 + jnp.log(-jnp.expm1(-dt0)),
        "a_log": jnp.log(jax.random.uniform(ks[17], (DEPTH, 2, SSM_HEADS), f32, 1.0, 16.0)),
        "d_skip": gain_init(ks[18], (DEPTH, 2, SSM_HEADS)),
        "ssm_norm_g": gain_init(ks[19], (DEPTH, D_INNER)),
        "w_proj_a": dense(ks[21], MLA_HEADS * V_DIM, (DEPTH, MLA_HEADS * V_DIM, D_MODEL)),
        "w_proj_b": dense(ks[22], D_INNER, (DEPTH, D_INNER, D_MODEL)),
        "w_out": dense(ks[23], D_MODEL, (DEPTH, D_MODEL, D_MODEL)),
        "w_mlp1": dense(ks[24], D_MODEL, (DEPTH, D_MODEL, D_FF)),
        "w_mlp2": dense(ks[25], D_FF, (DEPTH, D_FF, D_MODEL)),
    }


def reference(x, c, ctx, c_ctx, norm1_g, norm2_g, w_mod, b_mod, w_in, q_a_g, w_uq, kv_a_g, w_ukv,
              qk_q_g, qk_k_g, conv_w, conv_b, dt_bias, a_log, d_skip, ssm_norm_g,
              w_proj_a, w_proj_b, w_out, w_mlp1, w_mlp2):
    f32 = jnp.float32
    b, n_lat, _ = x.shape
    rows = n_lat // GRID_W
    row = jnp.repeat(jnp.arange(rows), GRID_W)
    col = jnp.tile(jnp.arange(GRID_W), rows)
    freqs = ROPE_BASE ** (-jnp.arange(ROPE_PAIRS, dtype=f32) / ROPE_PAIRS)
    ang = jnp.stack([row, col], axis=-1).astype(f32)[..., None] * freqs
    cos, sin = jnp.cos(ang).astype(x.dtype), jnp.sin(ang).astype(x.dtype)

    c_act = jax.nn.silu(c)[:, None, :]
    cctx_act = jax.nn.silu(c_ctx)[None, None, :]
    h0 = jnp.zeros((b, SSM_GROUPS, SSM_HEADS // SSM_GROUPS, SSM_HEAD_DIM, D_STATE), f32)

    for l in range(DEPTH):
        mod_x = jnp.split(c_act @ w_mod[l] + b_mod[l], N_MOD, axis=-1)
        mod_c = jnp.split(cctx_act @ w_mod[l] + b_mod[l], N_MOD, axis=-1)
        layer_w = (w_in[l], q_a_g[l], w_uq[l], kv_a_g[l], w_ukv[l], qk_q_g[l], qk_k_g[l],
                   conv_w[l], conv_b[l], dt_bias[l])
        merge_w = (ssm_norm_g[l], w_proj_a[l], w_proj_b[l], w_out[l])
        a = -jnp.exp(a_log[l].astype(f32))

        hc = modulate(rmsnorm(ctx, norm1_g[l]), mod_c[0], mod_c[1])
        hx = modulate(rmsnorm(x, norm1_g[l]), mod_x[0], mod_x[1])
        qn_c, qr_c, kn_c, kr_c, v_c, z_c, xs_c, bm_c, cm_c, dt_c, ga_c, gb_c = front(hc, *layer_w)
        qn_x, qr_x, kn_x, kr_x, v_x, z_x, xs_x, bm_x, cm_x, dt_x, ga_x, gb_x = front(hx, *layer_w)
        qr_x = apply_axial_rope(qr_x, cos[:, None], sin[:, None])
        kr_x = apply_axial_rope(kr_x, cos, sin)

        y_c, h_f, h_b = ssd_bidir(xs_c, dt_c, a, bm_c, cm_c, d_skip[l], h0, h0)
        y_x, _, _ = ssd_bidir(xs_x, dt_x, a, bm_x, cm_x, d_skip[l], h_f, h_b)

        att_x = attend(qn_x, qr_x,
                       jnp.concatenate([kn_c, kn_x], axis=1),
                       jnp.concatenate([kr_c, kr_x], axis=1),
                       jnp.concatenate([v_c, v_x], axis=1))
        x = x + mod_x[2] * merge_branches(att_x, y_x, z_x, ga_x, gb_x, *merge_w)

        if l < DEPTH - 1:
            att_c = attend(qn_c, qr_c, kn_c, kr_c, v_c)
            ctx = ctx + mod_c[2] * merge_branches(att_c, y_c, z_c, ga_c, gb_c, *merge_w)
            hc2 = modulate(rmsnorm(ctx, norm2_g[l]), mod_c[3], mod_c[4])
            ctx = ctx + mod_c[5] * sq_relu_mlp(hc2, w_mlp1[l], w_mlp2[l])

        hx2 = modulate(rmsnorm(x, norm2_g[l]), mod_x[3], mod_x[4])
        x = x + mod_x[5] * sq_relu_mlp(hx2, w_mlp1[l], w_mlp2[l])
    return x
```

```python
import functools

import numpy as np
import jax
import jax.numpy as jnp
from jax import lax
from jax.experimental import pallas as pl
from jax.experimental.pallas import tpu as pltpu

F32 = jnp.float32
BF16 = jnp.bfloat16

D_MODEL = 1024
GRID_W = 64
MLA_HEADS = 16
QK_NOPE = 64
QK_ROPE = 32
V_DIM = 64
Q_LORA = 256
KV_LORA = 128
ROPE_PAIRS = QK_ROPE // 4
ROPE_BASE = 10000.0
ATTN_SCALE = (QK_NOPE + QK_ROPE) ** -0.5
D_INNER = 2 * D_MODEL
SSM_HEAD_DIM = 64
SSM_HEADS = D_INNER // SSM_HEAD_DIM
SSM_GROUPS = 8
HEADS_PER_GROUP = SSM_HEADS // SSM_GROUPS
D_STATE = 128
D_CONV = 5
CHUNK = 128
CONV_DIM = D_INNER + 2 * SSM_GROUPS * D_STATE
D_FF = 4 * D_MODEL
N_MOD = 6
EPS = 1e-6

LANES = 128
HEAD_PAD = 128
GROUP_W = HEADS_PER_GROUP * SSM_HEAD_DIM
NARROW_W = 768
WIDE_W = D_INNER + CONV_DIM + 2 * D_MODEL
VMEM_LIMIT = 56 * 1024 * 1024


def _sigmoid(x):
    return 1.0 / (1.0 + jnp.exp(-x))


def _split3(x):
    hi = x.astype(BF16)
    r1 = x - hi.astype(F32)
    mid = r1.astype(BF16)
    lo = (r1 - mid.astype(F32)).astype(BF16)
    return hi, mid, lo


def _dot(a, b):
    return jnp.dot(a, b, preferred_element_type=F32)


def _dot_exact_rhs(a, sel):
    hi, mid, lo = _split3(a)
    return _dot(hi, sel) + _dot(mid, sel) + _dot(lo, sel)


def _dot_exact_lhs(sel, b):
    hi, mid, lo = _split3(b)
    return _dot(sel, hi) + _dot(sel, mid) + _dot(sel, lo)


def _params(*sem):
    return pltpu.CompilerParams(dimension_semantics=sem, vmem_limit_bytes=VMEM_LIMIT)


def _mod_kernel(c_ref, w_ref, b_ref, o_ref):
    c = c_ref[...]
    a = c * _sigmoid(c)
    w = w_ref[...]
    a_hi = a.astype(BF16)
    a_lo = (a - a_hi.astype(F32)).astype(BF16)
    w_hi = w.astype(BF16)
    w_lo = (w - w_hi.astype(F32)).astype(BF16)
    o_ref[...] = _dot(a_hi, w_hi) + _dot(a_hi, w_lo) + _dot(a_lo, w_hi) + b_ref[...]


def _mod_call(cvec, w_mod, b_mod):
    rows, d = cvec.shape
    n_out = w_mod.shape[1]
    tn = 1024
    return pl.pallas_call(
        _mod_kernel,
        out_shape=jax.ShapeDtypeStruct((rows, n_out), F32),
        grid=(n_out // tn,),
        in_specs=[pl.BlockSpec((rows, d), lambda j: (0, 0)),
                  pl.BlockSpec((d, tn), lambda j: (0, j)),
                  pl.BlockSpec((1, tn), lambda j: (0, j))],
        out_specs=pl.BlockSpec((rows, tn), lambda j: (0, j)),
        compiler_params=_params("parallel"),
        name="mod",
    )(cvec, w_mod, b_mod)


def _front_kernel(shift_row, x_ref, g_ref, mod_ref, w_ref, o_ref, h_ref):
    @pl.when(pl.program_id(2) == 0)
    def _():
        x = x_ref[0]
        ms = jnp.mean(x * x, axis=-1, keepdims=True)
        xn = x * lax.rsqrt(ms + EPS) * g_ref[...]
        shift = mod_ref[0, shift_row:shift_row + 1, :]
        scale = mod_ref[0, shift_row + 1:shift_row + 2, :]
        h_ref[...] = (xn * (1.0 + scale) + shift).astype(BF16)

    o_ref[0] = _dot(h_ref[...], w_ref[...]).astype(o_ref.dtype)


def _front_call(x, g, mod, w, out_dtype, tm, tn, shift_row, shared_mod, name):
    b, n, d = x.shape
    n_out = w.shape[1]
    mod_map = (lambda bi, i, j: (0, 0, 0)) if shared_mod else (lambda bi, i, j: (bi, 0, 0))
    return pl.pallas_call(
        functools.partial(_front_kernel, shift_row),
        out_shape=jax.ShapeDtypeStruct((b, n, n_out), out_dtype),
        grid=(b, n // tm, n_out // tn),
        in_specs=[pl.BlockSpec((1, tm, d), lambda bi, i, j: (bi, i, 0)),
                  pl.BlockSpec((1, d), lambda bi, i, j: (0, 0)),
                  pl.BlockSpec((1, 8, d), mod_map),
                  pl.BlockSpec((d, tn), lambda bi, i, j: (0, j))],
        out_specs=pl.BlockSpec((1, tm, tn), lambda bi, i, j: (bi, i, j)),
        scratch_shapes=[pltpu.VMEM((tm, d), BF16)],
        compiler_params=_params("parallel", "parallel", "arbitrary"),
        name=name,
    )(x, g, mod, w)


def _segment_rsqrt(t, e_ref, et_ref):
    ms = _dot((t * t).astype(BF16), e_ref[...])
    r = lax.rsqrt(ms + EPS)
    hi = r.astype(BF16)
    lo = (r - hi.astype(F32)).astype(BF16)
    return _dot(hi, et_ref[...]) + _dot(lo, et_ref[...])


def _mla_kernel(with_q, s_ref, t1_ref, t2_ref, qag_ref, kvag_ref, wuq_ref, wukv_ref, e_ref, et_ref,
                gq_ref, gk_ref, ga_ref, gb_ref, dtb_ref, *out_refs):
    if with_q:
        q_ref, k_ref, v_ref, dt_ref = out_refs
    else:
        k_ref, v_ref, dt_ref = out_refs
    s = s_ref[0]
    t1 = t1_ref[...]
    t2 = t2_ref[...]
    lane = lax.broadcasted_iota(jnp.int32, (1, LANES), 1)
    n_heads = MLA_HEADS
    k_w = n_heads * HEAD_PAD

    ckv = s[:, Q_LORA:Q_LORA + KV_LORA]
    ckvn = ckv * lax.rsqrt(jnp.mean(ckv * ckv, axis=-1, keepdims=True) + EPS) * kvag_ref[...]
    kv = _dot(ckvn.astype(BF16), wukv_ref[...])
    kn = kv[:, :k_w]
    v_ref[0] = kv[:, k_w:].astype(BF16)
    kn = kn * _segment_rsqrt(kn, e_ref, et_ref) * gk_ref[...]
    g1 = s[:, 384:512]
    g2 = s[:, 512:640]
    rope_lanes = (lane >= QK_NOPE) & (lane < QK_NOPE + QK_ROPE)
    kr_ms = jnp.sum(jnp.where(rope_lanes, g1 * g1, 0.0), axis=-1, keepdims=True) * (1.0 / QK_ROPE)
    krot = lax.rsqrt(kr_ms + EPS) * (g1 * ga_ref[...] * t1 + g2 * gb_ref[...] * t2)
    k_ref[0] = (kn + jnp.concatenate([krot] * n_heads, axis=1)).astype(BF16)

    if with_q:
        cq = s[:, :Q_LORA]
        cqn = cq * lax.rsqrt(jnp.mean(cq * cq, axis=-1, keepdims=True) + EPS) * qag_ref[...]
        q = _dot(cqn.astype(BF16), wuq_ref[...])
        tq = t1 + jnp.where(lane < QK_NOPE, 1.0, 0.0)
        q = q * _segment_rsqrt(q, e_ref, et_ref) * gq_ref[...] * jnp.concatenate([tq] * n_heads, axis=1)
        q_ref[0] = (q * ATTN_SCALE).astype(BF16)

    xdt = s[:, 640:768] + dtb_ref[...]
    dt_ref[0] = jnp.maximum(xdt, 0.0) + jnp.log1p(jnp.exp(-jnp.abs(xdt)))


def _mla_call(narrow, t1, t2, consts, with_q, tm, name):
    b, n, _ = narrow.shape
    qag, kvag, wuq, wukv, e, et, gq, gk, ga, gb, dtb = consts
    k_w = MLA_HEADS * HEAD_PAD
    v_w = MLA_HEADS * V_DIM
    full = lambda a: pl.BlockSpec(a.shape, lambda bi, i: (0,) * a.ndim)
    out_shape = [jax.ShapeDtypeStruct((b, n, k_w), BF16), jax.ShapeDtypeStruct((b, n, v_w), BF16),
                 jax.ShapeDtypeStruct((b, n, LANES), F32)]
    out_specs = [pl.BlockSpec((1, tm, k_w), lambda bi, i: (bi, i, 0)),
                 pl.BlockSpec((1, tm, v_w), lambda bi, i: (bi, i, 0)),
                 pl.BlockSpec((1, tm, LANES), lambda bi, i: (bi, i, 0))]
    if with_q:
        out_shape = [jax.ShapeDtypeStruct((b, n, k_w), BF16)] + out_shape
        out_specs = [pl.BlockSpec((1, tm, k_w), lambda bi, i: (bi, i, 0))] + out_specs
    return pl.pallas_call(
        functools.partial(_mla_kernel, with_q),
        out_shape=out_shape,
        grid=(b, n // tm),
        in_specs=[pl.BlockSpec((1, tm, NARROW_W), lambda bi, i: (bi, i, 0)),
                  pl.BlockSpec((tm, LANES), lambda bi, i: (i, 0)),
                  pl.BlockSpec((tm, LANES), lambda bi, i: (i, 0)),
                  full(qag), full(kvag), full(wuq), full(wukv), full(e), full(et),
                  full(gq), full(gk), full(ga), full(gb), full(dtb)],
        out_specs=out_specs,
        compiler_params=_params("parallel", "parallel"),
        name=name,
    )(narrow, t1, t2, qag, kvag, wuq, wukv, e, et, gq, gk, ga, gb, dtb)


CONV_HALO = 16


def _conv_kernel(cur_ref, prev_ref, next_ref, w_ref, b_ref, o_ref):
    i = pl.program_id(2)
    last = pl.num_programs(2) - 1
    rows = cur_ref.shape[1]
    cur = cur_ref[0].astype(F32)
    prev = prev_ref[0].astype(F32) * (i > 0).astype(F32)
    nxt = next_ref[0].astype(F32) * (i < last).astype(F32)
    ext = jnp.concatenate([prev, cur, nxt], axis=0)
    w = w_ref[...]
    pad = (D_CONV - 1) // 2
    acc = b_ref[...] + w[0:1, :] * ext[CONV_HALO - pad:CONV_HALO - pad + rows]
    for k in range(1, D_CONV):
        off = CONV_HALO - pad + k
        acc = acc + w[k:k + 1, :] * ext[off:off + rows]
    o_ref[0] = (acc * _sigmoid(acc)).astype(o_ref.dtype)


def _conv_call(wide, conv_w, conv_b, rows, cw, name):
    b, n, _ = wide.shape
    col0 = D_INNER // cw
    per = rows // CONV_HALO
    n_halo = n // CONV_HALO
    return pl.pallas_call(
        _conv_kernel,
        out_shape=jax.ShapeDtypeStruct((b, n, CONV_DIM), BF16),
        grid=(b, CONV_DIM // cw, n // rows),
        in_specs=[pl.BlockSpec((1, rows, cw), lambda bi, j, i: (bi, i, col0 + j)),
                  pl.BlockSpec((1, CONV_HALO, cw), lambda bi, j, i: (bi, jnp.maximum(i * per - 1, 0), col0 + j)),
                  pl.BlockSpec((1, CONV_HALO, cw),
                               lambda bi, j, i: (bi, jnp.minimum((i + 1) * per, n_halo - 1), col0 + j)),
                  pl.BlockSpec((D_CONV, cw), lambda bi, j, i: (0, j)),
                  pl.BlockSpec((1, cw), lambda bi, j, i: (0, j))],
        out_specs=pl.BlockSpec((1, rows, cw), lambda bi, j, i: (bi, i, j)),
        compiler_params=_params("parallel", "parallel", "parallel"),
        name=name,
    )(wide, wide, wide, conv_w, conv_b)


def _ssd_kernel(nc, xs_ref, b_ref, c_ref, dtc_ref, dtr_ref, arow_ref, acol_ref, dsk_ref, h0_ref,
                y_ref, hout_ref, yacc_ref, h_ref):
    q = CHUNK
    r_heads = HEADS_PER_GROUP
    ii = lax.broadcasted_iota(jnp.int32, (q, q), 0)
    jj = lax.broadcasted_iota(jnp.int32, (q, q), 1)
    wide_l = r_heads * q
    ii_w = lax.broadcasted_iota(jnp.int32, (q, wide_l), 0)
    jj_w = lax.broadcasted_iota(jnp.int32, (q, wide_l), 1) % q
    lane_head = lax.broadcasted_iota(jnp.int32, (1, GROUP_W), 1) // SSM_HEAD_DIM
    sel_row = lax.broadcasted_iota(jnp.int32, (2 * r_heads, GROUP_W + wide_l), 0)
    sel_lane = lax.broadcasted_iota(jnp.int32, (2 * r_heads, GROUP_W + wide_l), 1)
    sel_head = jnp.where(sel_lane < GROUP_W, sel_lane // SSM_HEAD_DIM, (sel_lane - GROUP_W) // q)
    dsk = dsk_ref[0:1, :] + dsk_ref[1:2, :]
    arow = arow_ref[0]
    acol = acol_ref[0]

    for d in range(2):
        if d == 0:
            tri = (jj <= ii).astype(BF16)
            tri_t = (ii <= jj).astype(BF16)
            keep = jj_w <= ii_w
            edge = q - 1
        else:
            tri = (jj >= ii).astype(BF16)
            tri_t = (ii >= jj).astype(BF16)
            keep = jj_w >= ii_w
            edge = 0
        sel = (sel_row == d * r_heads + sel_head).astype(BF16)
        h_ref[...] = h0_ref[0, 0, d]

        def body(t, carry, d=d, tri=tri, tri_t=tri_t, keep=keep, edge=edge, sel=sel):
            c = t if d == 0 else nc - 1 - t
            rows = pl.ds(pl.multiple_of(c * q, q), q)
            x = xs_ref[0, rows, :].astype(F32)
            bm = b_ref[0, rows, :]
            cm = c_ref[0, rows, :]
            dtc = dtc_ref[0, 0, c]
            dtr = dtr_ref[0, 0, c]
            dte = _dot_exact_rhs(dtc, sel[:, :GROUP_W])
            dae = _dot_exact_rhs(dtc * arow, sel)
            cum = _dot_exact_lhs(tri, dae)
            cum64 = cum[:, :GROUP_W]
            cumb = cum[:, GROUP_W:]
            crow = _dot_exact_rhs(dtr * acol, tri_t)
            crow_flat = jnp.concatenate(
                [crow[d * r_heads + r:d * r_heads + r + 1, :] for r in range(r_heads)], axis=1)
            cb = lax.dot_general(cm, bm, (((1,), (1,)), ((), ())), preferred_element_type=F32)
            decay = jnp.where(keep, jnp.exp(cumb - crow_flat), 0.0)
            l_all = (decay * jnp.concatenate([cb] * r_heads, axis=1)).astype(BF16)
            xdt = x * dte
            xdt_b = xdt.astype(BF16)
            x_bd = jnp.concatenate(
                [jnp.where(lane_head == r, xdt_b, jnp.zeros_like(xdt_b)) for r in range(r_heads)], axis=0)
            y = _dot(l_all, x_bd)
            h = h_ref[...]
            y = y + _dot(cm, h.astype(BF16)) * jnp.exp(cum64)
            cum_edge = cum64[edge:edge + 1, :]
            xw = (xdt * jnp.exp(cum_edge - cum64)).astype(BF16)
            h_ref[...] = h * jnp.exp(cum_edge) + lax.dot_general(
                bm, xw, (((0,), (0,)), ((), ())), preferred_element_type=F32)
            if d == 0:
                yacc_ref[rows, :] = y + dsk * x
            else:
                y_ref[0, rows, :] = (yacc_ref[rows, :] + y).astype(y_ref.dtype)
            return carry

        lax.fori_loop(0, nc, body, 0)
        hout_ref[0, 0, d] = h_ref[...]


def _ssd_call(xbc, dtc, dtr, arow, acol, dsk, h0, name):
    b, n, _ = xbc.shape
    nc = n // CHUNK
    g = SSM_GROUPS
    b_col0 = D_INNER // D_STATE
    c_col0 = b_col0 + g
    return pl.pallas_call(
        functools.partial(_ssd_kernel, nc),
        out_shape=[jax.ShapeDtypeStruct((b, n, D_INNER), BF16),
                   jax.ShapeDtypeStruct((b, g, 2, D_STATE, GROUP_W), F32)],
        grid=(b, g),
        in_specs=[pl.BlockSpec((1, n, GROUP_W), lambda bi, gi: (bi, 0, gi)),
                  pl.BlockSpec((1, n, D_STATE), lambda bi, gi: (bi, 0, b_col0 + gi)),
                  pl.BlockSpec((1, n, D_STATE), lambda bi, gi: (bi, 0, c_col0 + gi)),
                  pl.BlockSpec((1, 1, nc, CHUNK, 8), lambda bi, gi: (bi, gi, 0, 0, 0)),
                  pl.BlockSpec((1, 1, nc, 8, CHUNK), lambda bi, gi: (bi, gi, 0, 0, 0)),
                  pl.BlockSpec((1, 1, 8), lambda bi, gi: (gi, 0, 0)),
                  pl.BlockSpec((1, 8, 1), lambda bi, gi: (gi, 0, 0)),
                  pl.BlockSpec((2, GROUP_W), lambda bi, gi: (0, gi)),
                  pl.BlockSpec((1, 1, 2, D_STATE, GROUP_W), lambda bi, gi: (bi, gi, 0, 0, 0))],
        out_specs=[pl.BlockSpec((1, n, GROUP_W), lambda bi, gi: (bi, 0, gi)),
                   pl.BlockSpec((1, 1, 2, D_STATE, GROUP_W), lambda bi, gi: (bi, gi, 0, 0, 0))],
        scratch_shapes=[pltpu.VMEM((n, GROUP_W), F32), pltpu.VMEM((D_STATE, GROUP_W), F32)],
        compiler_params=_params("parallel", "parallel"),
        name=name,
    )(xbc, xbc, xbc, dtc, dtr, arow, acol, dsk, h0)


def _attn_kernel(q_ref, k_ref, v_ref, o_ref):
    q = q_ref[0]
    k = k_ref[0]
    v = v_ref[0]
    lane_head = lax.broadcasted_iota(jnp.int32, (1, 2 * V_DIM), 1) // V_DIM
    out = None
    for hh in range(2):
        qh = q[:, hh * HEAD_PAD:(hh + 1) * HEAD_PAD]
        kh = k[:, hh * HEAD_PAD:(hh + 1) * HEAD_PAD]
        s = lax.dot_general(qh, kh, (((1,), (1,)), ((), ())), preferred_element_type=F32)
        p = jnp.exp(s - jnp.max(s, axis=-1, keepdims=True))
        l = jnp.sum(p, axis=-1, keepdims=True)
        vh = jnp.where(lane_head == hh, v, jnp.zeros_like(v))
        o = _dot(p.astype(BF16), vh) / l
        out = o if out is None else out + o
    o_ref[0] = out.astype(o_ref.dtype)


def _attn_call(q, k, v, tq, name):
    b, n, _ = q.shape
    s = k.shape[1]
    pairs = MLA_HEADS // 2
    return pl.pallas_call(
        _attn_kernel,
        out_shape=jax.ShapeDtypeStruct((b, n, MLA_HEADS * V_DIM), BF16),
        grid=(b, pairs, n // tq),
        in_specs=[pl.BlockSpec((1, tq, 2 * HEAD_PAD), lambda bi, hp, i: (bi, i, hp)),
                  pl.BlockSpec((1, s, 2 * HEAD_PAD), lambda bi, hp, i: (bi, 0, hp)),
                  pl.BlockSpec((1, s, 2 * V_DIM), lambda bi, hp, i: (bi, 0, hp))],
        out_specs=pl.BlockSpec((1, tq, 2 * V_DIM), lambda bi, hp, i: (bi, i, hp)),
        compiler_params=_params("parallel", "parallel", "parallel"),
        name=name,
    )(q, k, v)


def _merge_kernel(att_ref, y_ref, z_ref, ga_ref, gb_ref, x_ref, mod_ref, sg_ref, wa_ref, wb_ref, wo_ref, o_ref):
    y = y_ref[0].astype(F32)
    z = z_ref[0].astype(F32)
    u = y * (z * _sigmoid(z))
    gw = D_INNER // SSM_GROUPS
    parts = []
    for g in range(SSM_GROUPS):
        ug = u[:, g * gw:(g + 1) * gw]
        parts.append(ug * lax.rsqrt(jnp.mean(ug * ug, axis=-1, keepdims=True) + EPS))
    un = (jnp.concatenate(parts, axis=1) * sg_ref[...]).astype(BF16)
    a = _dot(att_ref[0], wa_ref[...])
    bb = _dot(un, wb_ref[...])
    merged = _sigmoid(ga_ref[0].astype(F32)) * a + _sigmoid(gb_ref[0].astype(F32)) * bb
    o = _dot(merged.astype(BF16), wo_ref[...])
    o_ref[0] = x_ref[0] + mod_ref[0, 2:3, :] * o


def _merge_call(att, y, wide, x, mod, sg, wa, wb, wo, tm, name):
    b, n, d = x.shape
    ga_blk = (D_INNER + CONV_DIM) // d
    full = lambda a: pl.BlockSpec(a.shape, lambda bi, i: (0,) * a.ndim)
    return pl.pallas_call(
        _merge_kernel,
        out_shape=jax.ShapeDtypeStruct((b, n, d), F32),
        grid=(b, n // tm),
        in_specs=[pl.BlockSpec((1, tm, MLA_HEADS * V_DIM), lambda bi, i: (bi, i, 0)),
                  pl.BlockSpec((1, tm, D_INNER), lambda bi, i: (bi, i, 0)),
                  pl.BlockSpec((1, tm, D_INNER), lambda bi, i: (bi, i, 0)),
                  pl.BlockSpec((1, tm, d), lambda bi, i: (bi, i, ga_blk)),
                  pl.BlockSpec((1, tm, d), lambda bi, i: (bi, i, ga_blk + 1)),
                  pl.BlockSpec((1, tm, d), lambda bi, i: (bi, i, 0)),
                  pl.BlockSpec((1, 8, d), lambda bi, i: (bi, 0, 0)),
                  full(sg), full(wa), full(wb), full(wo)],
        out_specs=pl.BlockSpec((1, tm, d), lambda bi, i: (bi, i, 0)),
        compiler_params=_params("parallel", "parallel"),
        name=name,
    )(att, y, wide, wide, wide, x, mod, sg, wa, wb, wo)


def _mlp_kernel(ff_chunk, x_ref, g_ref, mod_ref, w1_ref, w2_ref, o_ref):
    x = x_ref[0]
    ms = jnp.mean(x * x, axis=-1, keepdims=True)
    xn = x * lax.rsqrt(ms + EPS) * g_ref[...]
    h = (xn * (1.0 + mod_ref[0, 4:5, :]) + mod_ref[0, 3:4, :]).astype(BF16)
    acc = None
    for c in range(D_FF // ff_chunk):
        a = _dot(h, w1_ref[:, c * ff_chunk:(c + 1) * ff_chunk])
        a = jnp.maximum(a, 0.0)
        part = _dot((a * a).astype(BF16), w2_ref[c * ff_chunk:(c + 1) * ff_chunk, :])
        acc = part if acc is None else acc + part
    o_ref[0] = x + mod_ref[0, 5:6, :] * acc


def _mlp_call(x, g, mod, w1, w2, tm, name):
    b, n, d = x.shape
    return pl.pallas_call(
        functools.partial(_mlp_kernel, 1024),
        out_shape=jax.ShapeDtypeStruct((b, n, d), F32),
        grid=(b, n // tm),
        in_specs=[pl.BlockSpec((1, tm, d), lambda bi, i: (bi, i, 0)),
                  pl.BlockSpec((1, d), lambda bi, i: (0, 0)),
                  pl.BlockSpec((1, 8, d), lambda bi, i: (bi, 0, 0)),
                  pl.BlockSpec(w1.shape, lambda bi, i: (0, 0), pipeline_mode=pl.Buffered(1)),
                  pl.BlockSpec(w2.shape, lambda bi, i: (0, 0), pipeline_mode=pl.Buffered(1))],
        out_specs=pl.BlockSpec((1, tm, d), lambda bi, i: (bi, i, 0)),
        compiler_params=_params("parallel", "parallel"),
        name=name,
    )(x, g, mod, w1, w2)


_ROPE_SWAP = np.concatenate([np.arange(8, 16), np.arange(0, 8), np.arange(24, 32), np.arange(16, 24)])


def _rope_tables(n_lat):
    rows = n_lat // GRID_W
    row = jnp.repeat(jnp.arange(rows), GRID_W)
    col = jnp.tile(jnp.arange(GRID_W), rows)
    freqs = ROPE_BASE ** (-jnp.arange(ROPE_PAIRS, dtype=F32) / ROPE_PAIRS)
    ang = jnp.stack([row, col], axis=-1).astype(F32)[..., None] * freqs
    cos, sin = jnp.cos(ang), jnp.sin(ang)
    cos32 = jnp.stack([cos, cos], axis=2).reshape(n_lat, QK_ROPE)
    sin32 = jnp.stack([-sin, sin], axis=2).reshape(n_lat, QK_ROPE)
    zeros = jnp.zeros((n_lat, QK_NOPE), F32)
    return (jnp.concatenate([zeros, cos32, sin32], axis=1),
            jnp.concatenate([zeros, sin32, cos32], axis=1))


def _identity_rope_tables(n):
    zeros = jnp.zeros((n, QK_NOPE), F32)
    one = jnp.ones((n, QK_ROPE), F32)
    zero = jnp.zeros((n, QK_ROPE), F32)
    return jnp.concatenate([zeros, one, zero], axis=1), jnp.concatenate([zeros, zero, one], axis=1)


def _layout_params(w_in, w_uq, w_ukv, q_a_g, kv_a_g, qk_q_g, qk_k_g, dt_bias):
    d = w_in.shape[0]
    o_cq, o_ckv, o_kr = 0, Q_LORA, Q_LORA + KV_LORA
    o_z = o_kr + QK_ROPE
    o_xbc = o_z + D_INNER
    o_dt = o_xbc + CONV_DIM
    o_ga = o_dt + 2 * SSM_HEADS
    kr = w_in[:, o_kr:o_kr + QK_ROPE]
    kr_sw = kr[:, _ROPE_SWAP]
    z64 = jnp.zeros((d, 64), F32)
    w_narrow = jnp.concatenate(
        [w_in[:, o_cq:o_kr], z64, kr, kr_sw, z64, kr_sw, kr, w_in[:, o_dt:o_ga], z64], axis=1).astype(BF16)
    w_wide = jnp.concatenate([w_in[:, o_z:o_dt], w_in[:, o_ga:]], axis=1).astype(BF16)

    hq = w_uq.reshape(Q_LORA, MLA_HEADS, QK_NOPE + QK_ROPE)
    rope_q = hq[:, :, QK_NOPE:]
    wuq = jnp.concatenate([hq[:, :, :QK_NOPE], rope_q, rope_q[:, :, _ROPE_SWAP]], axis=2)
    wuq = wuq.reshape(Q_LORA, MLA_HEADS * HEAD_PAD).astype(BF16)
    hkv = w_ukv.reshape(KV_LORA, MLA_HEADS, QK_NOPE + V_DIM)
    wk = jnp.concatenate([hkv[:, :, :QK_NOPE], jnp.zeros((KV_LORA, MLA_HEADS, HEAD_PAD - QK_NOPE), F32)], axis=2)
    wukv = jnp.concatenate([wk.reshape(KV_LORA, MLA_HEADS * HEAD_PAD),
                            hkv[:, :, QK_NOPE:].reshape(KV_LORA, MLA_HEADS * V_DIM)], axis=1).astype(BF16)

    gq_r = qk_q_g[QK_NOPE:]
    gq = jnp.tile(jnp.concatenate([qk_q_g[:QK_NOPE], gq_r, gq_r[_ROPE_SWAP]]), MLA_HEADS)[None, :]
    gk = jnp.tile(jnp.concatenate([qk_k_g[:QK_NOPE], jnp.zeros((HEAD_PAD - QK_NOPE,), F32)]), MLA_HEADS)[None, :]
    gk_r = qk_k_g[QK_NOPE:]
    z64v = jnp.zeros((QK_NOPE,), F32)
    ga = jnp.concatenate([z64v, gk_r, gk_r[_ROPE_SWAP]])[None, :]
    gb = jnp.concatenate([z64v, gk_r[_ROPE_SWAP], gk_r])[None, :]
    dtb = jnp.concatenate([dt_bias.reshape(-1), jnp.zeros((LANES - 2 * SSM_HEADS,), F32)])[None, :]

    lane = np.arange(MLA_HEADS * HEAD_PAD)
    head, off = lane // HEAD_PAD, lane % HEAD_PAD
    e = np.zeros((MLA_HEADS * HEAD_PAD, LANES), np.float32)
    e[lane[off < QK_NOPE], 2 * head[off < QK_NOPE]] = 1.0 / QK_NOPE
    rope = (off >= QK_NOPE) & (off < QK_NOPE + QK_ROPE)
    e[lane[rope], 2 * head[rope] + 1] = 1.0 / QK_ROPE
    et = np.zeros((LANES, MLA_HEADS * HEAD_PAD), np.float32)
    et[2 * head[off < QK_NOPE], lane[off < QK_NOPE]] = 1.0
    et[2 * head[off >= QK_NOPE] + 1, lane[off >= QK_NOPE]] = 1.0
    consts = (q_a_g[None, :], kv_a_g[None, :], wuq, wukv, jnp.asarray(e, BF16), jnp.asarray(et, BF16),
              gq, gk, ga, gb, dtb)
    return w_narrow, w_wide, consts


def _group_dt(dt):
    b, n, _ = dt.shape
    nc = n // CHUNK
    d5 = dt[:, :, :2 * SSM_HEADS].reshape(b, nc, CHUNK, 2, SSM_GROUPS, HEADS_PER_GROUP)
    dtc = d5.transpose(0, 4, 1, 2, 3, 5).reshape(b, SSM_GROUPS, nc, CHUNK, 2 * HEADS_PER_GROUP)
    dtr = d5.transpose(0, 4, 1, 3, 5, 2).reshape(b, SSM_GROUPS, nc, 2 * HEADS_PER_GROUP, CHUNK)
    return dtc, dtr


def kernel(x, c, ctx, c_ctx, norm1_g, norm2_g, w_mod, b_mod, w_in, q_a_g, w_uq, kv_a_g, w_ukv, qk_q_g, qk_k_g,
           conv_w, conv_b, dt_bias, a_log, d_skip, ssm_norm_g, w_proj_a, w_proj_b, w_out, w_mlp1, w_mlp2):
    assert w_mod.shape[0] == 1, "single-layer block"
    b, n_lat, d = x.shape
    n_ctx = ctx.shape[1]

    cvec = jnp.concatenate([c, c_ctx[None, :], jnp.zeros((8 - b - 1, d), F32)], axis=0)
    mod = _mod_call(cvec, w_mod[0], b_mod[0][None, :]).reshape(8, N_MOD, d)
    mod = jnp.concatenate([mod, jnp.zeros((8, 8 - N_MOD, d), F32)], axis=1)
    mod_x, mod_c = mod[:b], mod[b:b + 1]

    w_narrow, w_wide, consts = _layout_params(w_in[0], w_uq[0], w_ukv[0], q_a_g[0], kv_a_g[0], qk_q_g[0],
                                              qk_k_g[0], dt_bias[0])
    g1 = norm1_g[0][None, :]
    conv_wl, conv_bl = conv_w[0], conv_b[0][None, :]
    a = -jnp.exp(a_log[0].astype(F32))
    a_g = a.reshape(2, SSM_GROUPS, HEADS_PER_GROUP).transpose(1, 0, 2).reshape(SSM_GROUPS, 2 * HEADS_PER_GROUP)
    arow, acol = a_g[:, None, :], a_g[:, :, None]
    dsk = jnp.repeat(d_skip[0], SSM_HEAD_DIM, axis=1)

    narrow_c = _front_call(ctx, g1, mod_c, w_narrow, F32, n_ctx, NARROW_W, 0, True, "front_narrow_ctx")
    wide_c = _front_call(ctx, g1, mod_c, w_wide, BF16, n_ctx, 1024, 0, True, "front_wide_ctx")
    t1c, t2c = _identity_rope_tables(n_ctx)
    k_c, v_c, dt_c = _mla_call(narrow_c, t1c, t2c, consts, False, n_ctx, "mla_ctx")
    xbc_c = _conv_call(wide_c, conv_wl, conv_bl, n_ctx, 512, "conv_ctx")
    dtc_c, dtr_c = _group_dt(dt_c)
    h_zero = jnp.zeros((b, SSM_GROUPS, 2, D_STATE, GROUP_W), F32)
    _, h_ctx = _ssd_call(xbc_c, dtc_c, dtr_c, arow, acol, dsk, h_zero, "ssd_ctx")

    narrow_x = _front_call(x, g1, mod_x, w_narrow, F32, 512, NARROW_W, 0, False, "front_narrow")
    wide_x = _front_call(x, g1, mod_x, w_wide, BF16, 512, 1024, 0, False, "front_wide")
    t1, t2 = _rope_tables(n_lat)
    q_x, k_x, v_x, dt_x = _mla_call(narrow_x, t1, t2, consts, True, 256, "mla")
    xbc_x = _conv_call(wide_x, conv_wl, conv_bl, 512, 512, "conv")
    dtc_x, dtr_x = _group_dt(dt_x)
    y_x, _ = _ssd_call(xbc_x, dtc_x, dtr_x, arow, acol, dsk, h_ctx, "ssd")

    k_all = jnp.concatenate([k_c, k_x], axis=1)
    v_all = jnp.concatenate([v_c, v_x], axis=1)
    att = _attn_call(q_x, k_all, v_all, 256, "attn")

    x1 = _merge_call(att, y_x, wide_x, x, mod_x, ssm_norm_g[0][None, :], w_proj_a[0].astype(BF16),
                     w_proj_b[0].astype(BF16), w_out[0].astype(BF16), 256, "merge")
    return _mlp_call(x1, norm2_g[0][None, :], mod_x, w_mlp1[0].astype(BF16), w_mlp2[0].astype(BF16), 256, "mlp")
```

```python
import functools

import numpy as np
import jax
import jax.numpy as jnp
from jax import lax
from jax.experimental import pallas as pl
from jax.experimental.pallas import tpu as pltpu

F32 = jnp.float32
BF16 = jnp.bfloat16

D_MODEL = 1024
GRID_W = 64
MLA_HEADS = 16
QK_NOPE = 64
QK_ROPE = 32
V_DIM = 64
Q_LORA = 256
KV_LORA = 128
ROPE_PAIRS = QK_ROPE // 4
ROPE_BASE = 10000.0
ATTN_SCALE = (QK_NOPE + QK_ROPE) ** -0.5
LOG2_E = 1.4426950408889634
D_INNER = 2 * D_MODEL
SSM_HEAD_DIM = 64
SSM_HEADS = D_INNER // SSM_HEAD_DIM
SSM_GROUPS = 8
HEADS_PER_GROUP = SSM_HEADS // SSM_GROUPS
D_STATE = 128
D_CONV = 5
CHUNK = 128
CONV_DIM = D_INNER + 2 * SSM_GROUPS * D_STATE
D_FF = 4 * D_MODEL
N_MOD = 6
EPS = 1e-6

LANES = 128
HEAD_PAD = 128
GROUP_W = HEADS_PER_GROUP * SSM_HEAD_DIM
NARROW_W = 768
WIDE_W = D_INNER + CONV_DIM + 2 * D_MODEL
VMEM_LIMIT = 56 * 1024 * 1024


def _sigmoid(x):
    return 1.0 / (1.0 + jnp.exp(-x))


def _split3(x):
    hi = x.astype(BF16)
    r1 = x - hi.astype(F32)
    mid = r1.astype(BF16)
    lo = (r1 - mid.astype(F32)).astype(BF16)
    return hi, mid, lo


def _dot(a, b):
    return jnp.dot(a, b, preferred_element_type=F32)


def _dot_exact_rhs(a, sel):
    hi, mid, lo = _split3(a)
    return _dot(hi, sel) + _dot(mid, sel) + _dot(lo, sel)


def _dot_exact_lhs(sel, b):
    hi, mid, lo = _split3(b)
    return _dot(sel, hi) + _dot(sel, mid) + _dot(sel, lo)


def _params(*sem):
    return pltpu.CompilerParams(dimension_semantics=sem, vmem_limit_bytes=VMEM_LIMIT)


def _mod_kernel(c_ref, w_ref, b_ref, o_ref):
    c = c_ref[...]
    a = c * _sigmoid(c)
    w = w_ref[...]
    a_hi = a.astype(BF16)
    a_lo = (a - a_hi.astype(F32)).astype(BF16)
    w_hi = w.astype(BF16)
    w_lo = (w - w_hi.astype(F32)).astype(BF16)
    o_ref[...] = _dot(a_hi, w_hi) + _dot(a_hi, w_lo) + _dot(a_lo, w_hi) + b_ref[...]


def _mod_call(cvec, w_mod, b_mod):
    rows, d = cvec.shape
    n_out = w_mod.shape[1]
    tn = 1024
    return pl.pallas_call(
        _mod_kernel,
        out_shape=jax.ShapeDtypeStruct((rows, n_out), F32),
        grid=(n_out // tn,),
        in_specs=[pl.BlockSpec((rows, d), lambda j: (0, 0)),
                  pl.BlockSpec((d, tn), lambda j: (0, j)),
                  pl.BlockSpec((1, tn), lambda j: (0, j))],
        out_specs=pl.BlockSpec((rows, tn), lambda j: (0, j)),
        compiler_params=_params("parallel"),
        name="mod",
    )(cvec, w_mod, b_mod)


def _front_kernel(shift_row, x_ref, g_ref, mod_ref, w_ref, o_ref, h_ref):
    @pl.when(pl.program_id(2) == 0)
    def _():
        x = x_ref[0]
        ms = jnp.mean(x * x, axis=-1, keepdims=True)
        xn = x * lax.rsqrt(ms + EPS) * g_ref[...]
        shift = mod_ref[0, shift_row:shift_row + 1, :]
        scale = mod_ref[0, shift_row + 1:shift_row + 2, :]
        h_ref[...] = (xn * (1.0 + scale) + shift).astype(BF16)

    o_ref[0] = _dot(h_ref[...], w_ref[...]).astype(o_ref.dtype)


def _front_call(x, g, mod, w, out_dtype, tm, tn, shift_row, shared_mod, name):
    b, n, d = x.shape
    n_out = w.shape[1]
    mod_map = (lambda bi, i, j: (0, 0, 0)) if shared_mod else (lambda bi, i, j: (bi, 0, 0))
    return pl.pallas_call(
        functools.partial(_front_kernel, shift_row),
        out_shape=jax.ShapeDtypeStruct((b, n, n_out), out_dtype),
        grid=(b, n // tm, n_out // tn),
        in_specs=[pl.BlockSpec((1, tm, d), lambda bi, i, j: (bi, i, 0)),
                  pl.BlockSpec((1, d), lambda bi, i, j: (0, 0)),
                  pl.BlockSpec((1, 8, d), mod_map),
                  pl.BlockSpec((d, tn), lambda bi, i, j: (0, j))],
        out_specs=pl.BlockSpec((1, tm, tn), lambda bi, i, j: (bi, i, j)),
        scratch_shapes=[pltpu.VMEM((tm, d), BF16)],
        compiler_params=_params("parallel", "parallel", "arbitrary"),
        name=name,
    )(x, g, mod, w)


def _segment_rsqrt(t, e_ref, et_ref):
    ms = _dot((t * t).astype(BF16), e_ref[...])
    r = lax.rsqrt(ms + EPS)
    hi = r.astype(BF16)
    lo = (r - hi.astype(F32)).astype(BF16)
    return _dot(hi, et_ref[...]) + _dot(lo, et_ref[...])


def _mla_kernel(with_q, s_ref, t1_ref, t2_ref, qag_ref, kvag_ref, wuq_ref, wukv_ref, e_ref, et_ref,
                gq_ref, gk_ref, ga_ref, gb_ref, dtb_ref, alane_ref, *out_refs):
    if with_q:
        q_ref, k_ref, v_ref, cum_ref, dc3_ref = out_refs
    else:
        k_ref, v_ref, cum_ref, dc3_ref = out_refs
    s = s_ref[0]
    t1 = t1_ref[...]
    t2 = t2_ref[...]
    lane = lax.broadcasted_iota(jnp.int32, (1, LANES), 1)
    n_heads = MLA_HEADS
    k_w = n_heads * HEAD_PAD

    ckv = s[:, Q_LORA:Q_LORA + KV_LORA]
    ckvn = ckv * lax.rsqrt(jnp.mean(ckv * ckv, axis=-1, keepdims=True) + EPS) * kvag_ref[...]
    kv = _dot(ckvn.astype(BF16), wukv_ref[...])
    kn = kv[:, :k_w]
    v_ref[0] = kv[:, k_w:].astype(BF16)
    kn = kn * _segment_rsqrt(kn, e_ref, et_ref) * gk_ref[...]
    g1 = s[:, 384:512]
    g2 = s[:, 512:640]
    rope_lanes = (lane >= QK_NOPE) & (lane < QK_NOPE + QK_ROPE)
    kr_ms = jnp.sum(jnp.where(rope_lanes, g1 * g1, 0.0), axis=-1, keepdims=True) * (1.0 / QK_ROPE)
    krot = lax.rsqrt(kr_ms + EPS) * (g1 * ga_ref[...] * t1 + g2 * gb_ref[...] * t2)
    k_ref[0] = (kn + jnp.concatenate([krot] * n_heads, axis=1)).astype(BF16)

    if with_q:
        cq = s[:, :Q_LORA]
        cqn = cq * lax.rsqrt(jnp.mean(cq * cq, axis=-1, keepdims=True) + EPS) * qag_ref[...]
        q = _dot(cqn.astype(BF16), wuq_ref[...])
        tq = t1 + jnp.where(lane < QK_NOPE, 1.0, 0.0)
        q = q * _segment_rsqrt(q, e_ref, et_ref) * gq_ref[...] * jnp.concatenate([tq] * n_heads, axis=1)
        q_ref[0] = (q * (ATTN_SCALE * LOG2_E)).astype(BF16)

    xdt = s[:, 640:768] + dtb_ref[...]
    dt = jnp.maximum(xdt, 0.0) + jnp.log1p(jnp.exp(-jnp.abs(xdt)))
    dt = jnp.where(lane < 2 * SSM_HEADS, dt, 0.0)
    da = dt * -jnp.exp(alane_ref[...])
    ii = lax.broadcasted_iota(jnp.int32, (CHUNK, CHUNK), 0)
    jj = lax.broadcasted_iota(jnp.int32, (CHUNK, CHUNK), 1)
    tri_f = (jj <= ii).astype(BF16)
    tri_b = (jj >= ii).astype(BF16)
    cums = []
    for ch in range(s.shape[0] // CHUNK):
        pieces = _split3(da[ch * CHUNK:(ch + 1) * CHUNK])
        cf = sum(_dot(tri_f, p) for p in pieces)
        cb = sum(_dot(tri_b, p) for p in pieces)
        cums.append(jnp.where(lane < SSM_HEADS, cf, cb))
    cum = jnp.concatenate(cums, axis=0)
    cum_ref[0] = cum
    d3 = _split3(dt)
    c3 = _split3(cum)
    dc3_ref[0] = jnp.concatenate([d3[0], c3[0], d3[1], c3[1], d3[2], c3[2]], axis=1)


def _mla_call(narrow, t1, t2, consts, with_q, tm, name):
    b, n, _ = narrow.shape
    k_w = MLA_HEADS * HEAD_PAD
    v_w = MLA_HEADS * V_DIM
    full = lambda a: pl.BlockSpec(a.shape, lambda bi, i: (0,) * a.ndim)
    out_shape = [jax.ShapeDtypeStruct((b, n, k_w), BF16), jax.ShapeDtypeStruct((b, n, v_w), BF16),
                 jax.ShapeDtypeStruct((b, n, LANES), F32), jax.ShapeDtypeStruct((b, n, 6 * LANES), BF16)]
    out_specs = [pl.BlockSpec((1, tm, k_w), lambda bi, i: (bi, i, 0)),
                 pl.BlockSpec((1, tm, v_w), lambda bi, i: (bi, i, 0)),
                 pl.BlockSpec((1, tm, LANES), lambda bi, i: (bi, i, 0)),
                 pl.BlockSpec((1, tm, 6 * LANES), lambda bi, i: (bi, i, 0))]
    if with_q:
        out_shape = [jax.ShapeDtypeStruct((b, n, k_w), BF16)] + out_shape
        out_specs = [pl.BlockSpec((1, tm, k_w), lambda bi, i: (bi, i, 0))] + out_specs
    return pl.pallas_call(
        functools.partial(_mla_kernel, with_q),
        out_shape=out_shape,
        grid=(b, n // tm),
        in_specs=[pl.BlockSpec((1, tm, NARROW_W), lambda bi, i: (bi, i, 0)),
                  pl.BlockSpec((tm, LANES), lambda bi, i: (i, 0)),
                  pl.BlockSpec((tm, LANES), lambda bi, i: (i, 0)),
                  *[full(a) for a in consts]],
        out_specs=out_specs,
        compiler_params=_params("parallel", "parallel"),
        name=name,
    )(narrow, t1, t2, *consts)


CONV_HALO = 16


def _conv_kernel(cur_ref, prev_ref, next_ref, w_ref, b_ref, o_ref):
    i = pl.program_id(2)
    last = pl.num_programs(2) - 1
    rows = cur_ref.shape[1]
    cur = cur_ref[0].astype(F32)
    prev = prev_ref[0].astype(F32) * (i > 0).astype(F32)
    nxt = next_ref[0].astype(F32) * (i < last).astype(F32)
    ext = jnp.concatenate([prev, cur, nxt], axis=0)
    w = w_ref[...]
    pad = (D_CONV - 1) // 2
    acc = b_ref[...] + w[0:1, :] * ext[CONV_HALO - pad:CONV_HALO - pad + rows]
    for k in range(1, D_CONV):
        off = CONV_HALO - pad + k
        acc = acc + w[k:k + 1, :] * ext[off:off + rows]
    o_ref[0] = (acc * _sigmoid(acc)).astype(o_ref.dtype)


def _conv_call(wide, conv_w, conv_b, rows, cw, name):
    b, n, _ = wide.shape
    col0 = D_INNER // cw
    per = rows // CONV_HALO
    n_halo = n // CONV_HALO
    return pl.pallas_call(
        _conv_kernel,
        out_shape=jax.ShapeDtypeStruct((b, n, CONV_DIM), BF16),
        grid=(b, CONV_DIM // cw, n // rows),
        in_specs=[pl.BlockSpec((1, rows, cw), lambda bi, j, i: (bi, i, col0 + j)),
                  pl.BlockSpec((1, CONV_HALO, cw), lambda bi, j, i: (bi, jnp.maximum(i * per - 1, 0), col0 + j)),
                  pl.BlockSpec((1, CONV_HALO, cw),
                               lambda bi, j, i: (bi, jnp.minimum((i + 1) * per, n_halo - 1), col0 + j)),
                  pl.BlockSpec((D_CONV, cw), lambda bi, j, i: (0, j)),
                  pl.BlockSpec((1, cw), lambda bi, j, i: (0, j))],
        out_specs=pl.BlockSpec((1, rows, cw), lambda bi, j, i: (bi, i, j)),
        compiler_params=_params("parallel", "parallel", "parallel"),
        name=name,
    )(wide, wide, wide, conv_w, conv_b)


EXP_W = 4 * GROUP_W + 2 * HEADS_PER_GROUP * CHUNK


def _ssd_kernel(nc, xs_ref, b_ref, c_ref, p_ref, crow_ref, sel_ref, dsk_ref, h0_ref,
                y_ref, hout_ref, yacc_ref, s_ref, e_ref, dec_ref, h_ref):
    q = CHUNK
    r_heads = HEADS_PER_GROUP
    wide_l = r_heads * q
    ii_w = lax.broadcasted_iota(jnp.int32, (q, wide_l), 0)
    jj_w = lax.broadcasted_iota(jnp.int32, (q, wide_l), 1) % q
    keep_f = jj_w <= ii_w
    keep_b = jj_w >= ii_w
    lane_head = lax.broadcasted_iota(jnp.int32, (1, GROUP_W), 1) // SSM_HEAD_DIM
    dsk = dsk_ref[0:1, :] + dsk_ref[1:2, :]
    gw = GROUP_W

    def block_diag(xdt):
        xb = xdt.astype(BF16)
        return jnp.concatenate([jnp.where(lane_head == r, xb, jnp.zeros_like(xb)) for r in range(r_heads)], axis=0)

    def local(c, carry):
        rows = pl.ds(pl.multiple_of(c * q, q), q)
        x = xs_ref[0, rows, :].astype(F32)
        bm = b_ref[0, rows, :]
        cm = c_ref[0, rows, :]
        ex = _dot(p_ref[0, 0, rows, :], sel_ref[...])
        dte_f, dte_b = ex[:, 0:gw], ex[:, gw:2 * gw]
        c64_f, c64_b = ex[:, 2 * gw:3 * gw], ex[:, 3 * gw:4 * gw]
        cw_f, cw_b = ex[:, 4 * gw:4 * gw + wide_l], ex[:, 4 * gw + wide_l:]
        crow = crow_ref[0, 0, c]
        flat_f = jnp.concatenate([crow[r:r + 1, :] for r in range(r_heads)], axis=1)
        flat_b = jnp.concatenate([crow[r_heads + r:r_heads + r + 1, :] for r in range(r_heads)], axis=1)
        cb = lax.dot_general(cm, bm, (((1,), (1,)), ((), ())), preferred_element_type=F32)
        cb4 = jnp.concatenate([cb] * r_heads, axis=1)
        l_f = (jnp.where(keep_f, jnp.exp(cw_f - flat_f), 0.0) * cb4).astype(BF16)
        l_b = (jnp.where(keep_b, jnp.exp(cw_b - flat_b), 0.0) * cb4).astype(BF16)
        xdt_f = x * dte_f
        xdt_b = x * dte_b
        y = _dot(jnp.concatenate([l_f, l_b], axis=1),
                 jnp.concatenate([block_diag(xdt_f), block_diag(xdt_b)], axis=0))
        yacc_ref[rows, :] = y + dsk * x
        edge_f = c64_f[q - 1:q, :]
        edge_b = c64_b[0:1, :]
        xw = jnp.concatenate([(xdt_f * jnp.exp(edge_f - c64_f)).astype(BF16),
                              (xdt_b * jnp.exp(edge_b - c64_b)).astype(BF16)], axis=1)
        st = lax.dot_general(bm, xw, (((0,), (0,)), ((), ())), preferred_element_type=F32)
        s_ref[0, c] = st[:, :gw]
        s_ref[1, c] = st[:, gw:]
        e_ref[0, rows, :] = jnp.exp(c64_f)
        e_ref[1, rows, :] = jnp.exp(c64_b)
        dec_ref[0, c] = jnp.broadcast_to(jnp.exp(edge_f), (8, gw))
        dec_ref[1, c] = jnp.broadcast_to(jnp.exp(edge_b), (8, gw))
        return carry

    lax.fori_loop(0, nc, local, 0, unroll=2)

    h_ref[0] = h0_ref[0, 0, 0]
    h_ref[1] = h0_ref[0, 0, 1]

    def carry_states(t, carry):
        for d, c in ((0, t), (1, nc - 1 - t)):
            rows = pl.ds(pl.multiple_of(c * q, q), q)
            h = h_ref[d]
            yo = _dot(c_ref[0, rows, :], h.astype(BF16)) * e_ref[d, rows, :]
            yacc_ref[rows, :] = yacc_ref[rows, :] + yo
            h_ref[d] = h * dec_ref[d, c, 0:1, :] + s_ref[d, c]
        return carry

    lax.fori_loop(0, nc, carry_states, 0)
    hout_ref[0, 0, 0] = h_ref[0]
    hout_ref[0, 0, 1] = h_ref[1]

    def emit(c, carry):
        rows = pl.ds(pl.multiple_of(c * q, q), q)
        y_ref[0, rows, :] = yacc_ref[rows, :].astype(y_ref.dtype)
        return carry

    lax.fori_loop(0, nc, emit, 0)


def _ssd_call(xbc, pieces, crow, sel, dsk, h0, name):
    b, n, _ = xbc.shape
    nc = n // CHUNK
    g = SSM_GROUPS
    b_col0 = D_INNER // D_STATE
    c_col0 = b_col0 + g
    return pl.pallas_call(
        functools.partial(_ssd_kernel, nc),
        out_shape=[jax.ShapeDtypeStruct((b, n, D_INNER), BF16),
                   jax.ShapeDtypeStruct((b, g, 2, D_STATE, GROUP_W), F32)],
        grid=(b, g),
        in_specs=[pl.BlockSpec((1, n, GROUP_W), lambda bi, gi: (bi, 0, gi)),
                  pl.BlockSpec((1, n, D_STATE), lambda bi, gi: (bi, 0, b_col0 + gi)),
                  pl.BlockSpec((1, n, D_STATE), lambda bi, gi: (bi, 0, c_col0 + gi)),
                  pl.BlockSpec((1, 1, n, LANES), lambda bi, gi: (bi, gi, 0, 0)),
                  pl.BlockSpec((1, 1, nc, 8, CHUNK), lambda bi, gi: (bi, gi, 0, 0, 0)),
                  pl.BlockSpec((LANES, EXP_W), lambda bi, gi: (0, 0)),
                  pl.BlockSpec((2, GROUP_W), lambda bi, gi: (0, gi)),
                  pl.BlockSpec((1, 1, 2, D_STATE, GROUP_W), lambda bi, gi: (bi, gi, 0, 0, 0))],
        out_specs=[pl.BlockSpec((1, n, GROUP_W), lambda bi, gi: (bi, 0, gi)),
                   pl.BlockSpec((1, 1, 2, D_STATE, GROUP_W), lambda bi, gi: (bi, gi, 0, 0, 0))],
        scratch_shapes=[pltpu.VMEM((n, GROUP_W), F32),
                        pltpu.VMEM((2, nc, D_STATE, GROUP_W), F32),
                        pltpu.VMEM((2, n, GROUP_W), F32),
                        pltpu.VMEM((2, nc, 8, GROUP_W), F32),
                        pltpu.VMEM((2, D_STATE, GROUP_W), F32)],
        compiler_params=_params("parallel", "parallel"),
        name=name,
    )(xbc, xbc, xbc, pieces, crow, sel, dsk, h0)


VT_ROWS = V_DIM + 16


def _attn_kernel(nh, q_ref, k_ref, vt_ref, o_ref):
    sts = [lax.dot_general(k_ref[0, :, h * HEAD_PAD:(h + 1) * HEAD_PAD], q_ref[0, :, h * HEAD_PAD:(h + 1) * HEAD_PAD],
                           (((1,), (1,)), ((), ())), preferred_element_type=F32) for h in range(nh)]
    outs = []
    for h in range(nh):
        st = sts[h]
        p = jnp.exp2(st - jnp.max(st, axis=0, keepdims=True)).astype(BF16)
        o = _dot(vt_ref[0, h * VT_ROWS:(h + 1) * VT_ROWS, :], p)
        outs.append(o[:V_DIM] / o[V_DIM:V_DIM + 1])
    o_ref[0] = jnp.concatenate(outs, axis=0).T.astype(o_ref.dtype)


def _attn_call(q, k, vt, tq, nh, name):
    b, n, _ = q.shape
    s = k.shape[1]
    return pl.pallas_call(
        functools.partial(_attn_kernel, nh),
        out_shape=jax.ShapeDtypeStruct((b, n, MLA_HEADS * V_DIM), BF16),
        grid=(b, MLA_HEADS // nh, n // tq),
        in_specs=[pl.BlockSpec((1, tq, nh * HEAD_PAD), lambda bi, hp, i: (bi, i, hp)),
                  pl.BlockSpec((1, s, nh * HEAD_PAD), lambda bi, hp, i: (bi, 0, hp)),
                  pl.BlockSpec((1, nh * VT_ROWS, s), lambda bi, hp, i: (bi, hp, 0))],
        out_specs=pl.BlockSpec((1, tq, nh * V_DIM), lambda bi, hp, i: (bi, i, hp)),
        compiler_params=_params("parallel", "parallel", "parallel"),
        name=name,
    )(q, k, vt)


def _merge_kernel(att_ref, y_ref, z_ref, ga_ref, gb_ref, x_ref, mod_ref, sg_ref, wa_ref, wb_ref, wo_ref, o_ref):
    y = y_ref[0].astype(F32)
    z = z_ref[0].astype(F32)
    u = y * (z * _sigmoid(z))
    gw = D_INNER // SSM_GROUPS
    parts = []
    for g in range(SSM_GROUPS):
        ug = u[:, g * gw:(g + 1) * gw]
        parts.append(ug * lax.rsqrt(jnp.mean(ug * ug, axis=-1, keepdims=True) + EPS))
    un = (jnp.concatenate(parts, axis=1) * sg_ref[...]).astype(BF16)
    a = _dot(att_ref[0], wa_ref[...])
    bb = _dot(un, wb_ref[...])
    merged = _sigmoid(ga_ref[0].astype(F32)) * a + _sigmoid(gb_ref[0].astype(F32)) * bb
    o = _dot(merged.astype(BF16), wo_ref[...])
    o_ref[0] = x_ref[0] + mod_ref[0, 2:3, :] * o


def _merge_call(att, y, wide, x, mod, sg, wa, wb, wo, tm, name):
    b, n, d = x.shape
    ga_blk = (D_INNER + CONV_DIM) // d
    full = lambda a: pl.BlockSpec(a.shape, lambda bi, i: (0,) * a.ndim)
    return pl.pallas_call(
        _merge_kernel,
        out_shape=jax.ShapeDtypeStruct((b, n, d), F32),
        grid=(b, n // tm),
        in_specs=[pl.BlockSpec((1, tm, MLA_HEADS * V_DIM), lambda bi, i: (bi, i, 0)),
                  pl.BlockSpec((1, tm, D_INNER), lambda bi, i: (bi, i, 0)),
                  pl.BlockSpec((1, tm, D_INNER), lambda bi, i: (bi, i, 0)),
                  pl.BlockSpec((1, tm, d), lambda bi, i: (bi, i, ga_blk)),
                  pl.BlockSpec((1, tm, d), lambda bi, i: (bi, i, ga_blk + 1)),
                  pl.BlockSpec((1, tm, d), lambda bi, i: (bi, i, 0)),
                  pl.BlockSpec((1, 8, d), lambda bi, i: (bi, 0, 0)),
                  full(sg), full(wa), full(wb), full(wo)],
        out_specs=pl.BlockSpec((1, tm, d), lambda bi, i: (bi, i, 0)),
        compiler_params=_params("parallel", "parallel"),
        name=name,
    )(att, y, wide, wide, wide, x, mod, sg, wa, wb, wo)


def _mlp_kernel(ff_chunk, x_ref, g_ref, mod_ref, w1_ref, w2_ref, o_ref):
    x = x_ref[0]
    ms = jnp.mean(x * x, axis=-1, keepdims=True)
    xn = x * lax.rsqrt(ms + EPS) * g_ref[...]
    h = (xn * (1.0 + mod_ref[0, 4:5, :]) + mod_ref[0, 3:4, :]).astype(BF16)
    acc = None
    for c in range(D_FF // ff_chunk):
        a = _dot(h, w1_ref[:, c * ff_chunk:(c + 1) * ff_chunk])
        a = jnp.maximum(a, 0.0)
        part = _dot((a * a).astype(BF16), w2_ref[c * ff_chunk:(c + 1) * ff_chunk, :])
        acc = part if acc is None else acc + part
    o_ref[0] = x + mod_ref[0, 5:6, :] * acc


def _mlp_call(x, g, mod, w1, w2, tm, name):
    b, n, d = x.shape
    return pl.pallas_call(
        functools.partial(_mlp_kernel, 1024),
        out_shape=jax.ShapeDtypeStruct((b, n, d), F32),
        grid=(b, n // tm),
        in_specs=[pl.BlockSpec((1, tm, d), lambda bi, i: (bi, i, 0)),
                  pl.BlockSpec((1, d), lambda bi, i: (0, 0)),
                  pl.BlockSpec((1, 8, d), lambda bi, i: (bi, 0, 0)),
                  pl.BlockSpec(w1.shape, lambda bi, i: (0, 0), pipeline_mode=pl.Buffered(1)),
                  pl.BlockSpec(w2.shape, lambda bi, i: (0, 0), pipeline_mode=pl.Buffered(1))],
        out_specs=pl.BlockSpec((1, tm, d), lambda bi, i: (bi, i, 0)),
        compiler_params=_params("parallel", "parallel"),
        name=name,
    )(x, g, mod, w1, w2)


_ROPE_SWAP = np.concatenate([np.arange(8, 16), np.arange(0, 8), np.arange(24, 32), np.arange(16, 24)])


def _rope_tables(n_lat):
    rows = n_lat // GRID_W
    row = jnp.repeat(jnp.arange(rows), GRID_W)
    col = jnp.tile(jnp.arange(GRID_W), rows)
    freqs = ROPE_BASE ** (-jnp.arange(ROPE_PAIRS, dtype=F32) / ROPE_PAIRS)
    ang = jnp.stack([row, col], axis=-1).astype(F32)[..., None] * freqs
    cos, sin = jnp.cos(ang), jnp.sin(ang)
    cos32 = jnp.stack([cos, cos], axis=2).reshape(n_lat, QK_ROPE)
    sin32 = jnp.stack([-sin, sin], axis=2).reshape(n_lat, QK_ROPE)
    zeros = jnp.zeros((n_lat, QK_NOPE), F32)
    return (jnp.concatenate([zeros, cos32, sin32], axis=1),
            jnp.concatenate([zeros, sin32, cos32], axis=1))


def _identity_rope_tables(n):
    zeros = jnp.zeros((n, QK_NOPE), F32)
    one = jnp.ones((n, QK_ROPE), F32)
    zero = jnp.zeros((n, QK_ROPE), F32)
    return jnp.concatenate([zeros, one, zero], axis=1), jnp.concatenate([zeros, zero, one], axis=1)


def _layout_params(w_in, w_uq, w_ukv, q_a_g, kv_a_g, qk_q_g, qk_k_g, dt_bias, a_log):
    d = w_in.shape[0]
    o_cq, o_ckv, o_kr = 0, Q_LORA, Q_LORA + KV_LORA
    o_z = o_kr + QK_ROPE
    o_xbc = o_z + D_INNER
    o_dt = o_xbc + CONV_DIM
    o_ga = o_dt + 2 * SSM_HEADS
    kr = w_in[:, o_kr:o_kr + QK_ROPE]
    kr_sw = kr[:, _ROPE_SWAP]
    z64 = jnp.zeros((d, 64), F32)
    w_narrow = jnp.concatenate(
        [w_in[:, o_cq:o_kr], z64, kr, kr_sw, z64, kr_sw, kr, w_in[:, o_dt:o_ga], z64], axis=1).astype(BF16)
    w_wide = jnp.concatenate([w_in[:, o_z:o_dt], w_in[:, o_ga:]], axis=1).astype(BF16)

    hq = w_uq.reshape(Q_LORA, MLA_HEADS, QK_NOPE + QK_ROPE)
    rope_q = hq[:, :, QK_NOPE:]
    wuq = jnp.concatenate([hq[:, :, :QK_NOPE], rope_q, rope_q[:, :, _ROPE_SWAP]], axis=2)
    wuq = wuq.reshape(Q_LORA, MLA_HEADS * HEAD_PAD).astype(BF16)
    hkv = w_ukv.reshape(KV_LORA, MLA_HEADS, QK_NOPE + V_DIM)
    wk = jnp.concatenate([hkv[:, :, :QK_NOPE], jnp.zeros((KV_LORA, MLA_HEADS, HEAD_PAD - QK_NOPE), F32)], axis=2)
    wukv = jnp.concatenate([wk.reshape(KV_LORA, MLA_HEADS * HEAD_PAD),
                            hkv[:, :, QK_NOPE:].reshape(KV_LORA, MLA_HEADS * V_DIM)], axis=1).astype(BF16)

    gq_r = qk_q_g[QK_NOPE:]
    gq = jnp.tile(jnp.concatenate([qk_q_g[:QK_NOPE], gq_r, gq_r[_ROPE_SWAP]]), MLA_HEADS)[None, :]
    gk = jnp.tile(jnp.concatenate([qk_k_g[:QK_NOPE], jnp.zeros((HEAD_PAD - QK_NOPE,), F32)]), MLA_HEADS)[None, :]
    gk_r = qk_k_g[QK_NOPE:]
    z64v = jnp.zeros((QK_NOPE,), F32)
    ga = jnp.concatenate([z64v, gk_r, gk_r[_ROPE_SWAP]])[None, :]
    gb = jnp.concatenate([z64v, gk_r[_ROPE_SWAP], gk_r])[None, :]
    lane_pad = jnp.zeros((LANES - 2 * SSM_HEADS,), F32)
    dtb = jnp.concatenate([dt_bias.reshape(-1), lane_pad])[None, :]
    alane = jnp.concatenate([a_log.astype(F32).reshape(-1), lane_pad])[None, :]

    lane = np.arange(MLA_HEADS * HEAD_PAD)
    head, off = lane // HEAD_PAD, lane % HEAD_PAD
    e = np.zeros((MLA_HEADS * HEAD_PAD, LANES), np.float32)
    e[lane[off < QK_NOPE], 2 * head[off < QK_NOPE]] = 1.0 / QK_NOPE
    rope = (off >= QK_NOPE) & (off < QK_NOPE + QK_ROPE)
    e[lane[rope], 2 * head[rope] + 1] = 1.0 / QK_ROPE
    et = np.zeros((LANES, MLA_HEADS * HEAD_PAD), np.float32)
    et[2 * head[off < QK_NOPE], lane[off < QK_NOPE]] = 1.0
    et[2 * head[off >= QK_NOPE] + 1, lane[off >= QK_NOPE]] = 1.0
    consts = (q_a_g[None, :], kv_a_g[None, :], wuq, wukv, jnp.asarray(e, BF16), jnp.asarray(et, BF16),
              gq, gk, ga, gb, dtb, alane)
    return w_narrow, w_wide, consts


def _group_scan_inputs(cum, dc3):
    b, n, _ = cum.shape
    nc = n // CHUNK
    g, r = SSM_GROUPS, HEADS_PER_GROUP
    c5 = cum[:, :, :2 * SSM_HEADS].reshape(b, nc, CHUNK, 2, g, r)
    crow = c5.transpose(0, 4, 1, 3, 5, 2).reshape(b, g, nc, 2 * r, CHUNK)
    p = dc3.reshape(b, n, 3, 2, LANES)[..., :2 * SSM_HEADS].reshape(b, n, 3, 2, 2, g, r)
    p = p.transpose(0, 5, 1, 2, 3, 4, 6).reshape(b, g, n, 3 * 4 * r)
    p = jnp.concatenate([p, jnp.zeros((b, g, n, LANES - 3 * 4 * r), BF16)], axis=-1)
    return p, crow


def _expansion_selector():
    r = HEADS_PER_GROUP
    sel = np.zeros((LANES, EXP_W), np.float32)
    for piece in range(3):
        for d in range(2):
            for h in range(r):
                row_dt = piece * 4 * r + d * r + h
                row_cum = piece * 4 * r + 2 * r + d * r + h
                sel[row_dt, d * GROUP_W + h * SSM_HEAD_DIM:d * GROUP_W + (h + 1) * SSM_HEAD_DIM] = 1.0
                base = 2 * GROUP_W + d * GROUP_W + h * SSM_HEAD_DIM
                sel[row_cum, base:base + SSM_HEAD_DIM] = 1.0
                base = 4 * GROUP_W + d * r * CHUNK + h * CHUNK
                sel[row_cum, base:base + CHUNK] = 1.0
    return jnp.asarray(sel, BF16)


def kernel(x, c, ctx, c_ctx, norm1_g, norm2_g, w_mod, b_mod, w_in, q_a_g, w_uq, kv_a_g, w_ukv, qk_q_g, qk_k_g,
           conv_w, conv_b, dt_bias, a_log, d_skip, ssm_norm_g, w_proj_a, w_proj_b, w_out, w_mlp1, w_mlp2):
    assert w_mod.shape[0] == 1, "single-layer block"
    b, n_lat, d = x.shape
    n_ctx = ctx.shape[1]

    cvec = jnp.concatenate([c, c_ctx[None, :], jnp.zeros((8 - b - 1, d), F32)], axis=0)
    mod = _mod_call(cvec, w_mod[0], b_mod[0][None, :]).reshape(8, N_MOD, d)
    mod = jnp.concatenate([mod, jnp.zeros((8, 8 - N_MOD, d), F32)], axis=1)
    mod_x, mod_c = mod[:b], mod[b:b + 1]

    w_narrow, w_wide, consts = _layout_params(w_in[0], w_uq[0], w_ukv[0], q_a_g[0], kv_a_g[0], qk_q_g[0],
                                              qk_k_g[0], dt_bias[0], a_log[0])
    g1 = norm1_g[0][None, :]
    conv_wl, conv_bl = conv_w[0], conv_b[0][None, :]
    dsk = jnp.repeat(d_skip[0], SSM_HEAD_DIM, axis=1)
    sel = _expansion_selector()

    narrow_c = _front_call(ctx, g1, mod_c, w_narrow, F32, n_ctx, NARROW_W, 0, True, "front_narrow_ctx")
    wide_c = _front_call(ctx, g1, mod_c, w_wide, BF16, n_ctx, 1024, 0, True, "front_wide_ctx")
    t1c, t2c = _identity_rope_tables(n_ctx)
    k_c, v_c, cum_c, dc3_c = _mla_call(narrow_c, t1c, t2c, consts, False, n_ctx, "mla_ctx")
    xbc_c = _conv_call(wide_c, conv_wl, conv_bl, n_ctx, 512, "conv_ctx")
    p_c, crow_c = _group_scan_inputs(cum_c, dc3_c)
    h_zero = jnp.zeros((b, SSM_GROUPS, 2, D_STATE, GROUP_W), F32)
    _, h_ctx = _ssd_call(xbc_c, p_c, crow_c, sel, dsk, h_zero, "ssd_ctx")

    narrow_x = _front_call(x, g1, mod_x, w_narrow, F32, 512, NARROW_W, 0, False, "front_narrow")
    wide_x = _front_call(x, g1, mod_x, w_wide, BF16, 512, 1024, 0, False, "front_wide")
    t1, t2 = _rope_tables(n_lat)
    q_x, k_x, v_x, cum_x, dc3_x = _mla_call(narrow_x, t1, t2, consts, True, 256, "mla")
    xbc_x = _conv_call(wide_x, conv_wl, conv_bl, 512, 512, "conv")
    p_x, crow_x = _group_scan_inputs(cum_x, dc3_x)
    y_x, _ = _ssd_call(xbc_x, p_x, crow_x, sel, dsk, h_ctx, "ssd")

    k_all = jnp.concatenate([k_c, k_x], axis=1)
    v_all = jnp.concatenate([v_c, v_x], axis=1).reshape(b, -1, MLA_HEADS, V_DIM)
    v_all = jnp.concatenate([v_all, jnp.ones(v_all.shape[:3] + (VT_ROWS - V_DIM,), BF16)], axis=-1)
    vt_all = jnp.swapaxes(v_all.reshape(b, -1, MLA_HEADS * VT_ROWS), 1, 2)
    att = _attn_call(q_x, k_all, vt_all, 512, 2, "attn")

    x1 = _merge_call(att, y_x, wide_x, x, mod_x, ssm_norm_g[0][None, :], w_proj_a[0].astype(BF16),
                     w_proj_b[0].astype(BF16), w_out[0].astype(BF16), 256, "merge")
    return _mlp_call(x1, norm2_g[0][None, :], mod_x, w_mlp1[0].astype(BF16), w_mlp2[0].astype(BF16), 256, "mlp")
```

```python
import functools

import numpy as np
import jax
import jax.numpy as jnp
from jax import lax
from jax.experimental import pallas as pl
from jax.experimental.pallas import tpu as pltpu

F32 = jnp.float32
BF16 = jnp.bfloat16

D_MODEL = 1024
GRID_W = 64
MLA_HEADS = 16
QK_NOPE = 64
QK_ROPE = 32
V_DIM = 64
Q_LORA = 256
KV_LORA = 128
ROPE_PAIRS = QK_ROPE // 4
ROPE_BASE = 10000.0
ATTN_SCALE = (QK_NOPE + QK_ROPE) ** -0.5
LOG2_E = 1.4426950408889634
D_INNER = 2 * D_MODEL
SSM_HEAD_DIM = 64
SSM_HEADS = D_INNER // SSM_HEAD_DIM
SSM_GROUPS = 8
HEADS_PER_GROUP = SSM_HEADS // SSM_GROUPS
D_STATE = 128
D_CONV = 5
CHUNK = 128
CONV_DIM = D_INNER + 2 * SSM_GROUPS * D_STATE
D_FF = 4 * D_MODEL
N_MOD = 6
EPS = 1e-6

LANES = 128
HEAD_PAD = 128
GROUP_W = HEADS_PER_GROUP * SSM_HEAD_DIM
NARROW_W = 768
WIDE_W = D_INNER + CONV_DIM + 2 * D_MODEL
VMEM_LIMIT = 56 * 1024 * 1024


def _sigmoid(x):
    return 1.0 / (1.0 + jnp.exp(-x))


def _split3(x):
    hi = x.astype(BF16)
    r1 = x - hi.astype(F32)
    mid = r1.astype(BF16)
    lo = (r1 - mid.astype(F32)).astype(BF16)
    return hi, mid, lo


def _dot(a, b):
    return jnp.dot(a, b, preferred_element_type=F32)


def _dot_exact_rhs(a, sel):
    hi, mid, lo = _split3(a)
    return _dot(hi, sel) + _dot(mid, sel) + _dot(lo, sel)


def _dot_exact_lhs(sel, b):
    hi, mid, lo = _split3(b)
    return _dot(sel, hi) + _dot(sel, mid) + _dot(sel, lo)


def _params(*sem):
    return pltpu.CompilerParams(dimension_semantics=sem, vmem_limit_bytes=VMEM_LIMIT)


def _mod_kernel(c_ref, w_ref, b_ref, o_ref):
    c = c_ref[...]
    a = c * _sigmoid(c)
    w = w_ref[...]
    a_hi = a.astype(BF16)
    a_lo = (a - a_hi.astype(F32)).astype(BF16)
    w_hi = w.astype(BF16)
    w_lo = (w - w_hi.astype(F32)).astype(BF16)
    o_ref[...] = _dot(a_hi, w_hi) + _dot(a_hi, w_lo) + _dot(a_lo, w_hi) + b_ref[...]


def _mod_call(cvec, w_mod, b_mod):
    rows, d = cvec.shape
    n_out = w_mod.shape[1]
    tn = 1024
    return pl.pallas_call(
        _mod_kernel,
        out_shape=jax.ShapeDtypeStruct((rows, n_out), F32),
        grid=(n_out // tn,),
        in_specs=[pl.BlockSpec((rows, d), lambda j: (0, 0)),
                  pl.BlockSpec((d, tn), lambda j: (0, j)),
                  pl.BlockSpec((1, tn), lambda j: (0, j))],
        out_specs=pl.BlockSpec((rows, tn), lambda j: (0, j)),
        compiler_params=_params("parallel"),
        name="mod",
    )(cvec, w_mod, b_mod)


FRONT_HALO = 16
FRONT_COLS = 1024


def _front_kernel(x_ref, xp_ref, xn_ref, g_ref, mod_ref, wn_ref, ww_ref, cw_ref, cb_ref, narrow_ref, wide_ref, pre_ref):
    i = pl.program_id(1)
    last = pl.num_programs(1) - 1
    tm = x_ref.shape[1]
    g = g_ref[...]
    shift = mod_ref[0, 0:1, :]
    scale = mod_ref[0, 1:2, :]

    def normmod(xv):
        ms = jnp.mean(xv * xv, axis=-1, keepdims=True)
        return xv * lax.rsqrt(ms + EPS) * g * (1.0 + scale) + shift

    hc = normmod(x_ref[0]).astype(BF16)
    hp = (normmod(xp_ref[0]) * (i > 0).astype(F32)).astype(BF16)
    hn = (normmod(xn_ref[0]) * (i < last).astype(F32)).astype(BF16)
    h_ext = jnp.concatenate([hp, hc, hn], axis=0)
    narrow_ref[0] = _dot(hc, wn_ref[...])
    pad = (D_CONV - 1) // 2
    n_chunks = WIDE_W // FRONT_COLS
    is_conv = [D_INNER <= j * FRONT_COLS < D_INNER + CONV_DIM for j in range(n_chunks)]
    conv_js = [j for j in range(n_chunks) if is_conv[j]]
    plain_js = [j for j in range(n_chunks) if not is_conv[j]]
    order = [j for pair in zip(conv_js, plain_js) for j in pair] + conv_js[len(plain_js):] + plain_js[len(conv_js):]
    for j in order:
        lo = j * FRONT_COLS
        cols = slice(lo, lo + FRONT_COLS)
        if D_INNER <= lo < D_INNER + CONV_DIM:
            slot = j % 2
            pre_ref[slot] = _dot(h_ext, ww_ref[:, cols])
            cc = slice(lo - D_INNER, lo - D_INNER + FRONT_COLS)
            acc = cb_ref[:, cc] + cw_ref[0:1, cc] * pre_ref[slot, pl.ds(FRONT_HALO - pad, tm), :]
            for k in range(1, D_CONV):
                acc = acc + cw_ref[k:k + 1, cc] * pre_ref[slot, pl.ds(FRONT_HALO - pad + k, tm), :]
            wide_ref[0, :, cols] = (acc * _sigmoid(acc)).astype(BF16)
        else:
            wide_ref[0, :, cols] = _dot(hc, ww_ref[:, cols]).astype(BF16)


def _front_call(x, g, mod, w_narrow, w_wide, conv_w, conv_b, tm, shared_mod, name):
    b, n, d = x.shape
    per = tm // FRONT_HALO
    n_halo = n // FRONT_HALO
    mod_map = (lambda bi, i: (0, 0, 0)) if shared_mod else (lambda bi, i: (bi, 0, 0))
    resident = lambda a: pl.BlockSpec(a.shape, lambda bi, i: (0,) * a.ndim, pipeline_mode=pl.Buffered(1))
    return pl.pallas_call(
        _front_kernel,
        out_shape=[jax.ShapeDtypeStruct((b, n, NARROW_W), F32), jax.ShapeDtypeStruct((b, n, WIDE_W), BF16)],
        grid=(b, n // tm),
        in_specs=[pl.BlockSpec((1, tm, d), lambda bi, i: (bi, i, 0)),
                  pl.BlockSpec((1, FRONT_HALO, d), lambda bi, i: (bi, jnp.maximum(i * per - 1, 0), 0)),
                  pl.BlockSpec((1, FRONT_HALO, d), lambda bi, i: (bi, jnp.minimum((i + 1) * per, n_halo - 1), 0)),
                  pl.BlockSpec((1, d), lambda bi, i: (0, 0)),
                  pl.BlockSpec((1, 8, d), mod_map),
                  resident(w_narrow), resident(w_wide), resident(conv_w), resident(conv_b)],
        out_specs=[pl.BlockSpec((1, tm, NARROW_W), lambda bi, i: (bi, i, 0)),
                   pl.BlockSpec((1, tm, WIDE_W), lambda bi, i: (bi, i, 0))],
        scratch_shapes=[pltpu.VMEM((2, tm + 2 * FRONT_HALO, FRONT_COLS), F32)],
        compiler_params=_params("parallel", "parallel"),
        name=name,
    )(x, x, x, g, mod, w_narrow, w_wide, conv_w, conv_b)


def _segment_rsqrt(t, e_ref, et_ref):
    ms = _dot((t * t).astype(BF16), e_ref[...])
    r = lax.rsqrt(ms + EPS)
    hi = r.astype(BF16)
    lo = (r - hi.astype(F32)).astype(BF16)
    return _dot(jnp.concatenate([hi, lo], axis=1), et_ref[...])


def _mla_kernel(sc_ref, sx_ref, t1_ref, t2_ref, qag_ref, kvag_ref, wuq_ref, wukv_ref, e_ref, et_ref,
                gq_ref, gk_ref, ga_ref, gb_ref, dtb_ref, alane_ref, q_ref, k_ref, vt_ref, cum_ref, dc3_ref):
    is_ctx = pl.program_id(1) == 0
    s = jnp.where(is_ctx, sc_ref[0], sx_ref[0])
    t1 = t1_ref[...]
    t2 = t2_ref[...]
    lane = lax.broadcasted_iota(jnp.int32, (1, LANES), 1)
    n_heads = MLA_HEADS
    k_w = n_heads * HEAD_PAD

    ckv = s[:, Q_LORA:Q_LORA + KV_LORA]
    ckvn = ckv * lax.rsqrt(jnp.mean(ckv * ckv, axis=-1, keepdims=True) + EPS) * kvag_ref[...]
    kv = _dot(ckvn.astype(BF16), wukv_ref[...])
    kn = kv[:, :k_w]
    vt = kv[:, k_w:].T
    ones = jnp.ones((VT_ROWS - V_DIM, vt.shape[1]), F32)
    vt_ref[0] = jnp.concatenate(
        [blk for h in range(n_heads) for blk in (vt[h * V_DIM:(h + 1) * V_DIM], ones)], axis=0).astype(BF16)
    kn = kn * _segment_rsqrt(kn, e_ref, et_ref) * gk_ref[...]
    g1 = s[:, 384:512]
    g2 = s[:, 512:640]
    rope_lanes = (lane >= QK_NOPE) & (lane < QK_NOPE + QK_ROPE)
    kr_ms = jnp.sum(jnp.where(rope_lanes, g1 * g1, 0.0), axis=-1, keepdims=True) * (1.0 / QK_ROPE)
    krot = lax.rsqrt(kr_ms + EPS) * (g1 * ga_ref[...] * t1 + g2 * gb_ref[...] * t2)
    k_ref[0] = (kn + jnp.concatenate([krot] * n_heads, axis=1)).astype(BF16)

    cq = s[:, :Q_LORA]
    cqn = cq * lax.rsqrt(jnp.mean(cq * cq, axis=-1, keepdims=True) + EPS) * qag_ref[...]
    q = _dot(cqn.astype(BF16), wuq_ref[...])
    tq = t1 + jnp.where(lane < QK_NOPE, 1.0, 0.0)
    q = q * _segment_rsqrt(q, e_ref, et_ref) * gq_ref[...] * jnp.concatenate([tq] * n_heads, axis=1)
    q_ref[0] = (q * (ATTN_SCALE * LOG2_E)).astype(BF16)

    xdt = s[:, 640:768] + dtb_ref[...]
    dt = jnp.maximum(xdt, 0.0) + jnp.log1p(jnp.exp(-jnp.abs(xdt)))
    dt = jnp.where(lane < 2 * SSM_HEADS, dt, 0.0)
    da = dt * (-LOG2_E * jnp.exp(alane_ref[...]))
    ii = lax.broadcasted_iota(jnp.int32, (CHUNK, CHUNK), 0)
    jj = lax.broadcasted_iota(jnp.int32, (CHUNK, CHUNK), 1)
    tri_f = (jj <= ii).astype(BF16)
    tri_b = (jj >= ii).astype(BF16)
    cums = []
    for ch in range(s.shape[0] // CHUNK):
        pieces = _split3(da[ch * CHUNK:(ch + 1) * CHUNK])
        cf = sum(_dot(tri_f, p) for p in pieces)
        cb = sum(_dot(tri_b, p) for p in pieces)
        cums.append(jnp.where(lane < SSM_HEADS, cf, cb))
    cum = jnp.concatenate(cums, axis=0)
    cum_ref[0] = cum
    d3 = _split3(dt)
    c3 = _split3(cum)
    dc3_ref[0] = jnp.concatenate([d3[0], c3[0], d3[1], c3[1], d3[2], c3[2]], axis=1)


def _mla_call(narrow_c, narrow_x, t1, t2, consts, name):
    b, n_ctx, _ = narrow_c.shape
    n_lat = narrow_x.shape[1]
    tm = n_ctx
    s_all = n_ctx + n_lat
    k_w = MLA_HEADS * HEAD_PAD
    full = lambda a: pl.BlockSpec(a.shape, lambda bi, i: (0,) * a.ndim)
    latent_tile = lambda bi, i: (bi, jnp.maximum(i - 1, 0), 0)
    return pl.pallas_call(
        _mla_kernel,
        out_shape=[jax.ShapeDtypeStruct((b, n_lat, k_w), BF16),
                   jax.ShapeDtypeStruct((b, s_all, k_w), BF16),
                   jax.ShapeDtypeStruct((b, MLA_HEADS * VT_ROWS, s_all), BF16),
                   jax.ShapeDtypeStruct((b, s_all, LANES), F32),
                   jax.ShapeDtypeStruct((b, s_all, 6 * LANES), BF16)],
        grid=(b, s_all // tm),
        in_specs=[pl.BlockSpec((1, tm, NARROW_W), lambda bi, i: (bi, 0, 0)),
                  pl.BlockSpec((1, tm, NARROW_W), latent_tile),
                  pl.BlockSpec((tm, LANES), lambda bi, i: (i, 0)),
                  pl.BlockSpec((tm, LANES), lambda bi, i: (i, 0)),
                  *[full(a) for a in consts]],
        out_specs=[pl.BlockSpec((1, tm, k_w), latent_tile),
                   pl.BlockSpec((1, tm, k_w), lambda bi, i: (bi, i, 0)),
                   pl.BlockSpec((1, MLA_HEADS * VT_ROWS, tm), lambda bi, i: (bi, 0, i)),
                   pl.BlockSpec((1, tm, LANES), lambda bi, i: (bi, i, 0)),
                   pl.BlockSpec((1, tm, 6 * LANES), lambda bi, i: (bi, i, 0))],
        compiler_params=_params("parallel", "arbitrary"),
        name=name,
    )(narrow_c, narrow_x, t1, t2, *consts)


EXP_W = 4 * GROUP_W


def _ssd_kernel(nc, xs_ref, b_ref, c_ref, p_ref, crow_ref, sel_ref, dsk_ref, h0_ref,
                y_ref, hout_ref, yacc_ref, s_ref, e_ref, dec_ref, h_ref):
    q = CHUNK
    r_heads = HEADS_PER_GROUP
    wide_l = r_heads * q
    ii_w = lax.broadcasted_iota(jnp.int32, (q, wide_l), 0)
    jj_w = lax.broadcasted_iota(jnp.int32, (q, wide_l), 1) % q
    keep_f = jj_w <= ii_w
    keep_b = jj_w >= ii_w
    lane_head = lax.broadcasted_iota(jnp.int32, (1, GROUP_W), 1) // SSM_HEAD_DIM
    low_half = lax.broadcasted_iota(jnp.int32, (1, LANES), 1) < SSM_HEAD_DIM
    dsk = dsk_ref[0:1, :] + dsk_ref[1:2, :]
    gw = GROUP_W

    def per_head_128(c64):
        out = []
        for pair in range(r_heads // 2):
            v = c64[:, pair * LANES:(pair + 1) * LANES]
            sw = pltpu.roll(v, SSM_HEAD_DIM, axis=1)
            out += [jnp.where(low_half, v, sw), jnp.where(low_half, sw, v)]
        return jnp.concatenate(out, axis=1)

    def block_diag(xdt):
        xb = xdt.astype(BF16)
        return jnp.concatenate([jnp.where(lane_head == r, xb, jnp.zeros_like(xb)) for r in range(r_heads)], axis=0)

    def local(c, carry):
        rows = pl.ds(pl.multiple_of(c * q, q), q)
        x = xs_ref[0, rows, :].astype(F32)
        bm = b_ref[0, rows, :]
        cm = c_ref[0, rows, :]
        ex = _dot(p_ref[0, 0, rows, :], sel_ref[...])
        dte_f, dte_b = ex[:, 0:gw], ex[:, gw:2 * gw]
        c64_f, c64_b = ex[:, 2 * gw:3 * gw], ex[:, 3 * gw:4 * gw]
        cw_f, cw_b = per_head_128(c64_f), per_head_128(c64_b)
        crow = crow_ref[0, 0, c]
        flat_f = jnp.concatenate([crow[r:r + 1, :] for r in range(r_heads)], axis=1)
        flat_b = jnp.concatenate([crow[r_heads + r:r_heads + r + 1, :] for r in range(r_heads)], axis=1)
        cb = lax.dot_general(cm, bm, (((1,), (1,)), ((), ())), preferred_element_type=F32)
        cb4 = jnp.concatenate([cb] * r_heads, axis=1)
        l_f = (jnp.where(keep_f, jnp.exp2(cw_f - flat_f), 0.0) * cb4).astype(BF16)
        l_b = (jnp.where(keep_b, jnp.exp2(cw_b - flat_b), 0.0) * cb4).astype(BF16)
        xdt_f = x * dte_f
        xdt_b = x * dte_b
        y = _dot(jnp.concatenate([l_f, l_b], axis=1),
                 jnp.concatenate([block_diag(xdt_f), block_diag(xdt_b)], axis=0))
        yacc_ref[rows, :] = y + dsk * x
        edge_f = c64_f[q - 1:q, :]
        edge_b = c64_b[0:1, :]
        xw = jnp.concatenate([(xdt_f * jnp.exp2(edge_f - c64_f)).astype(BF16),
                              (xdt_b * jnp.exp2(edge_b - c64_b)).astype(BF16)], axis=1)
        st = lax.dot_general(bm, xw, (((0,), (0,)), ((), ())), preferred_element_type=F32)
        s_ref[0, c] = st[:, :gw]
        s_ref[1, c] = st[:, gw:]
        e_ref[0, rows, :] = jnp.exp2(c64_f)
        e_ref[1, rows, :] = jnp.exp2(c64_b)
        dec_ref[0, c] = jnp.broadcast_to(jnp.exp2(edge_f), (8, gw))
        dec_ref[1, c] = jnp.broadcast_to(jnp.exp2(edge_b), (8, gw))
        return carry

    lax.fori_loop(0, nc, local, 0, unroll=4 if nc % 4 == 0 else 2)

    h_ref[0] = h0_ref[0, 0, 0]
    h_ref[1] = h0_ref[0, 0, 1]

    def carry_states(t, carry):
        for d, c in ((0, t), (1, nc - 1 - t)):
            rows = pl.ds(pl.multiple_of(c * q, q), q)
            h = h_ref[d]
            yo = _dot(c_ref[0, rows, :], h.astype(BF16)) * e_ref[d, rows, :]
            yacc_ref[rows, :] = yacc_ref[rows, :] + yo
            h_ref[d] = h * dec_ref[d, c, 0:1, :] + s_ref[d, c]
        return carry

    lax.fori_loop(0, nc, carry_states, 0, unroll=2)
    hout_ref[0, 0, 0] = h_ref[0]
    hout_ref[0, 0, 1] = h_ref[1]

    def emit(c, carry):
        rows = pl.ds(pl.multiple_of(c * q, q), q)
        y_ref[0, rows, :] = yacc_ref[rows, :].astype(y_ref.dtype)
        return carry

    lax.fori_loop(0, nc, emit, 0)


def _ssd_call(wide, pieces, crow, sel, dsk, h0, name):
    xbc = wide
    b, n, _ = wide.shape
    nc = n // CHUNK
    g = SSM_GROUPS
    x_col0 = D_INNER // GROUP_W
    b_col0 = 2 * D_INNER // D_STATE
    c_col0 = b_col0 + g
    return pl.pallas_call(
        functools.partial(_ssd_kernel, nc),
        out_shape=[jax.ShapeDtypeStruct((b, n, D_INNER), BF16),
                   jax.ShapeDtypeStruct((b, g, 2, D_STATE, GROUP_W), F32)],
        grid=(b, g),
        in_specs=[pl.BlockSpec((1, n, GROUP_W), lambda bi, gi: (bi, 0, x_col0 + gi)),
                  pl.BlockSpec((1, n, D_STATE), lambda bi, gi: (bi, 0, b_col0 + gi)),
                  pl.BlockSpec((1, n, D_STATE), lambda bi, gi: (bi, 0, c_col0 + gi)),
                  pl.BlockSpec((1, 1, n, LANES), lambda bi, gi: (bi, gi, 0, 0)),
                  pl.BlockSpec((1, 1, nc, 8, CHUNK), lambda bi, gi: (bi, gi, 0, 0, 0)),
                  pl.BlockSpec((LANES, EXP_W), lambda bi, gi: (0, 0)),
                  pl.BlockSpec((2, GROUP_W), lambda bi, gi: (0, gi)),
                  pl.BlockSpec((1, 1, 2, D_STATE, GROUP_W), lambda bi, gi: (bi, gi, 0, 0, 0))],
        out_specs=[pl.BlockSpec((1, n, GROUP_W), lambda bi, gi: (bi, 0, gi)),
                   pl.BlockSpec((1, 1, 2, D_STATE, GROUP_W), lambda bi, gi: (bi, gi, 0, 0, 0))],
        scratch_shapes=[pltpu.VMEM((n, GROUP_W), F32),
                        pltpu.VMEM((2, nc, D_STATE, GROUP_W), F32),
                        pltpu.VMEM((2, n, GROUP_W), F32),
                        pltpu.VMEM((2, nc, 8, GROUP_W), F32),
                        pltpu.VMEM((2, D_STATE, GROUP_W), F32)],
        compiler_params=_params("parallel", "parallel"),
        name=name,
    )(xbc, xbc, xbc, pieces, crow, sel, dsk, h0)


VT_ROWS = V_DIM + 16


def _attn_kernel(nh, q_ref, k_ref, vt_ref, o_ref):
    sts = [lax.dot_general(k_ref[0, :, h * HEAD_PAD:(h + 1) * HEAD_PAD], q_ref[0, :, h * HEAD_PAD:(h + 1) * HEAD_PAD],
                           (((1,), (1,)), ((), ())), preferred_element_type=F32) for h in range(nh)]
    outs = []
    for h in range(nh):
        st = sts[h]
        p = jnp.exp2(st - jnp.max(st, axis=0, keepdims=True)).astype(BF16)
        o = _dot(vt_ref[0, h * VT_ROWS:(h + 1) * VT_ROWS, :], p)
        outs.append(o[:V_DIM] / o[V_DIM:V_DIM + 1])
    o_ref[0] = jnp.concatenate(outs, axis=0).T.astype(o_ref.dtype)


def _attn_call(q, k, vt, tq, nh, name):
    b, n, _ = q.shape
    s = k.shape[1]
    return pl.pallas_call(
        functools.partial(_attn_kernel, nh),
        out_shape=jax.ShapeDtypeStruct((b, n, MLA_HEADS * V_DIM), BF16),
        grid=(b, MLA_HEADS // nh, n // tq),
        in_specs=[pl.BlockSpec((1, tq, nh * HEAD_PAD), lambda bi, hp, i: (bi, i, hp)),
                  pl.BlockSpec((1, s, nh * HEAD_PAD), lambda bi, hp, i: (bi, 0, hp)),
                  pl.BlockSpec((1, nh * VT_ROWS, s), lambda bi, hp, i: (bi, hp, 0))],
        out_specs=pl.BlockSpec((1, tq, nh * V_DIM), lambda bi, hp, i: (bi, i, hp)),
        compiler_params=_params("parallel", "parallel", "parallel"),
        name=name,
    )(q, k, vt)


def _merge_kernel(att_ref, y_ref, z_ref, ga_ref, gb_ref, x_ref, mod_ref, sg_ref, wa_ref, wb_ref, wo_ref, o_ref):
    y = y_ref[0].astype(F32)
    z = z_ref[0].astype(F32)
    u = y * (z * _sigmoid(z))
    gw = D_INNER // SSM_GROUPS
    parts = []
    for g in range(SSM_GROUPS):
        ug = u[:, g * gw:(g + 1) * gw]
        parts.append(ug * lax.rsqrt(jnp.mean(ug * ug, axis=-1, keepdims=True) + EPS))
    un = (jnp.concatenate(parts, axis=1) * sg_ref[...]).astype(BF16)
    a = _dot(att_ref[0], wa_ref[...])
    bb = _dot(un, wb_ref[...])
    merged = _sigmoid(ga_ref[0].astype(F32)) * a + _sigmoid(gb_ref[0].astype(F32)) * bb
    o = _dot(merged.astype(BF16), wo_ref[...])
    o_ref[0] = x_ref[0] + mod_ref[0, 2:3, :] * o


def _merge_call(att, y, wide, x, mod, sg, wa, wb, wo, tm, name):
    b, n, d = x.shape
    ga_blk = (D_INNER + CONV_DIM) // d
    full = lambda a: pl.BlockSpec(a.shape, lambda bi, i: (0,) * a.ndim)
    return pl.pallas_call(
        _merge_kernel,
        out_shape=jax.ShapeDtypeStruct((b, n, d), F32),
        grid=(b, n // tm),
        in_specs=[pl.BlockSpec((1, tm, MLA_HEADS * V_DIM), lambda bi, i: (bi, i, 0)),
                  pl.BlockSpec((1, tm, D_INNER), lambda bi, i: (bi, i, 0)),
                  pl.BlockSpec((1, tm, D_INNER), lambda bi, i: (bi, i, 0)),
                  pl.BlockSpec((1, tm, d), lambda bi, i: (bi, i, ga_blk)),
                  pl.BlockSpec((1, tm, d), lambda bi, i: (bi, i, ga_blk + 1)),
                  pl.BlockSpec((1, tm, d), lambda bi, i: (bi, i, 0)),
                  pl.BlockSpec((1, 8, d), lambda bi, i: (bi, 0, 0)),
                  full(sg), full(wa), full(wb), full(wo)],
        out_specs=pl.BlockSpec((1, tm, d), lambda bi, i: (bi, i, 0)),
        compiler_params=_params("parallel", "parallel"),
        name=name,
    )(att, y, wide, wide, wide, x, mod, sg, wa, wb, wo)


def _mlp_kernel(ff_chunk, x_ref, g_ref, mod_ref, w1_ref, w2_ref, o_ref):
    x = x_ref[0]
    ms = jnp.mean(x * x, axis=-1, keepdims=True)
    xn = x * lax.rsqrt(ms + EPS) * g_ref[...]
    h = (xn * (1.0 + mod_ref[0, 4:5, :]) + mod_ref[0, 3:4, :]).astype(BF16)
    acc = None
    for c in range(D_FF // ff_chunk):
        a = _dot(h, w1_ref[:, c * ff_chunk:(c + 1) * ff_chunk])
        a = jnp.maximum(a, 0.0)
        part = _dot((a * a).astype(BF16), w2_ref[c * ff_chunk:(c + 1) * ff_chunk, :])
        acc = part if acc is None else acc + part
    o_ref[0] = x + mod_ref[0, 5:6, :] * acc


def _mlp_call(x, g, mod, w1, w2, tm, name):
    b, n, d = x.shape
    return pl.pallas_call(
        functools.partial(_mlp_kernel, 1024),
        out_shape=jax.ShapeDtypeStruct((b, n, d), F32),
        grid=(b, n // tm),
        in_specs=[pl.BlockSpec((1, tm, d), lambda bi, i: (bi, i, 0)),
                  pl.BlockSpec((1, d), lambda bi, i: (0, 0)),
                  pl.BlockSpec((1, 8, d), lambda bi, i: (bi, 0, 0)),
                  pl.BlockSpec(w1.shape, lambda bi, i: (0, 0), pipeline_mode=pl.Buffered(1)),
                  pl.BlockSpec(w2.shape, lambda bi, i: (0, 0), pipeline_mode=pl.Buffered(1))],
        out_specs=pl.BlockSpec((1, tm, d), lambda bi, i: (bi, i, 0)),
        compiler_params=_params("parallel", "parallel"),
        name=name,
    )(x, g, mod, w1, w2)


_ROPE_SWAP = np.concatenate([np.arange(8, 16), np.arange(0, 8), np.arange(24, 32), np.arange(16, 24)])


def _rope_tables(n_lat):
    rows = n_lat // GRID_W
    row = jnp.repeat(jnp.arange(rows), GRID_W)
    col = jnp.tile(jnp.arange(GRID_W), rows)
    freqs = ROPE_BASE ** (-jnp.arange(ROPE_PAIRS, dtype=F32) / ROPE_PAIRS)
    ang = jnp.stack([row, col], axis=-1).astype(F32)[..., None] * freqs
    cos, sin = jnp.cos(ang), jnp.sin(ang)
    cos32 = jnp.stack([cos, cos], axis=2).reshape(n_lat, QK_ROPE)
    sin32 = jnp.stack([-sin, sin], axis=2).reshape(n_lat, QK_ROPE)
    zeros = jnp.zeros((n_lat, QK_NOPE), F32)
    return (jnp.concatenate([zeros, cos32, sin32], axis=1),
            jnp.concatenate([zeros, sin32, cos32], axis=1))


def _identity_rope_tables(n):
    zeros = jnp.zeros((n, QK_NOPE), F32)
    one = jnp.ones((n, QK_ROPE), F32)
    zero = jnp.zeros((n, QK_ROPE), F32)
    return jnp.concatenate([zeros, one, zero], axis=1), jnp.concatenate([zeros, zero, one], axis=1)


def _layout_params(w_in, w_uq, w_ukv, q_a_g, kv_a_g, qk_q_g, qk_k_g, dt_bias, a_log):
    d = w_in.shape[0]
    o_cq, o_ckv, o_kr = 0, Q_LORA, Q_LORA + KV_LORA
    o_z = o_kr + QK_ROPE
    o_xbc = o_z + D_INNER
    o_dt = o_xbc + CONV_DIM
    o_ga = o_dt + 2 * SSM_HEADS
    kr = w_in[:, o_kr:o_kr + QK_ROPE]
    kr_sw = kr[:, _ROPE_SWAP]
    z64 = jnp.zeros((d, 64), F32)
    w_narrow = jnp.concatenate(
        [w_in[:, o_cq:o_kr], z64, kr, kr_sw, z64, kr_sw, kr, w_in[:, o_dt:o_ga], z64], axis=1).astype(BF16)
    w_wide = jnp.concatenate([w_in[:, o_z:o_dt], w_in[:, o_ga:]], axis=1).astype(BF16)

    hq = w_uq.reshape(Q_LORA, MLA_HEADS, QK_NOPE + QK_ROPE)
    rope_q = hq[:, :, QK_NOPE:]
    wuq = jnp.concatenate([hq[:, :, :QK_NOPE], rope_q, rope_q[:, :, _ROPE_SWAP]], axis=2)
    wuq = wuq.reshape(Q_LORA, MLA_HEADS * HEAD_PAD).astype(BF16)
    hkv = w_ukv.reshape(KV_LORA, MLA_HEADS, QK_NOPE + V_DIM)
    wk = jnp.concatenate([hkv[:, :, :QK_NOPE], jnp.zeros((KV_LORA, MLA_HEADS, HEAD_PAD - QK_NOPE), F32)], axis=2)
    wukv = jnp.concatenate([wk.reshape(KV_LORA, MLA_HEADS * HEAD_PAD),
                            hkv[:, :, QK_NOPE:].reshape(KV_LORA, MLA_HEADS * V_DIM)], axis=1).astype(BF16)

    gq_r = qk_q_g[QK_NOPE:]
    gq = jnp.tile(jnp.concatenate([qk_q_g[:QK_NOPE], gq_r, gq_r[_ROPE_SWAP]]), MLA_HEADS)[None, :]
    gk = jnp.tile(jnp.concatenate([qk_k_g[:QK_NOPE], jnp.zeros((HEAD_PAD - QK_NOPE,), F32)]), MLA_HEADS)[None, :]
    gk_r = qk_k_g[QK_NOPE:]
    z64v = jnp.zeros((QK_NOPE,), F32)
    ga = jnp.concatenate([z64v, gk_r, gk_r[_ROPE_SWAP]])[None, :]
    gb = jnp.concatenate([z64v, gk_r[_ROPE_SWAP], gk_r])[None, :]
    lane_pad = jnp.zeros((LANES - 2 * SSM_HEADS,), F32)
    dtb = jnp.concatenate([dt_bias.reshape(-1), lane_pad])[None, :]
    alane = jnp.concatenate([a_log.astype(F32).reshape(-1), lane_pad])[None, :]

    lane = np.arange(MLA_HEADS * HEAD_PAD)
    head, off = lane // HEAD_PAD, lane % HEAD_PAD
    e = np.zeros((MLA_HEADS * HEAD_PAD, LANES), np.float32)
    e[lane[off < QK_NOPE], 2 * head[off < QK_NOPE]] = 1.0 / QK_NOPE
    rope = (off >= QK_NOPE) & (off < QK_NOPE + QK_ROPE)
    e[lane[rope], 2 * head[rope] + 1] = 1.0 / QK_ROPE
    et = np.zeros((LANES, MLA_HEADS * HEAD_PAD), np.float32)
    et[2 * head[off < QK_NOPE], lane[off < QK_NOPE]] = 1.0
    et[2 * head[off >= QK_NOPE] + 1, lane[off >= QK_NOPE]] = 1.0
    et = np.concatenate([et, et], axis=0)
    consts = (q_a_g[None, :], kv_a_g[None, :], wuq, wukv, jnp.asarray(e, BF16), jnp.asarray(et, BF16),
              gq, gk, ga, gb, dtb, alane)
    return w_narrow, w_wide, consts


def _group_scan_inputs(cum, dc3):
    b, n, _ = cum.shape
    nc = n // CHUNK
    g, r = SSM_GROUPS, HEADS_PER_GROUP
    c5 = cum[:, :, :2 * SSM_HEADS].reshape(b, nc, CHUNK, 2, g, r)
    crow = c5.transpose(0, 4, 1, 3, 5, 2).reshape(b, g, nc, 2 * r, CHUNK)
    p = dc3.reshape(b, n, 3, 2, LANES)[..., :2 * SSM_HEADS].reshape(b, n, 3, 2, 2, g, r)
    p = p.transpose(0, 5, 1, 2, 3, 4, 6).reshape(b, g, n, 3 * 4 * r)
    p = jnp.concatenate([p, jnp.zeros((b, g, n, LANES - 3 * 4 * r), BF16)], axis=-1)
    return p, crow


def _expansion_selector():
    r = HEADS_PER_GROUP
    sel = np.zeros((LANES, EXP_W), np.float32)
    for piece in range(3):
        for d in range(2):
            for h in range(r):
                row_dt = piece * 4 * r + d * r + h
                row_cum = piece * 4 * r + 2 * r + d * r + h
                sel[row_dt, d * GROUP_W + h * SSM_HEAD_DIM:d * GROUP_W + (h + 1) * SSM_HEAD_DIM] = 1.0
                base = 2 * GROUP_W + d * GROUP_W + h * SSM_HEAD_DIM
                sel[row_cum, base:base + SSM_HEAD_DIM] = 1.0
    return jnp.asarray(sel, BF16)


def kernel(x, c, ctx, c_ctx, norm1_g, norm2_g, w_mod, b_mod, w_in, q_a_g, w_uq, kv_a_g, w_ukv, qk_q_g, qk_k_g,
           conv_w, conv_b, dt_bias, a_log, d_skip, ssm_norm_g, w_proj_a, w_proj_b, w_out, w_mlp1, w_mlp2):
    assert w_mod.shape[0] == 1, "single-layer block"
    b, n_lat, d = x.shape
    n_ctx = ctx.shape[1]

    cvec = jnp.concatenate([c, c_ctx[None, :], jnp.zeros((8 - b - 1, d), F32)], axis=0)
    mod = _mod_call(cvec, w_mod[0], b_mod[0][None, :]).reshape(8, N_MOD, d)
    mod = jnp.concatenate([mod, jnp.zeros((8, 8 - N_MOD, d), F32)], axis=1)
    mod_x, mod_c = mod[:b], mod[b:b + 1]

    w_narrow, w_wide, consts = _layout_params(w_in[0], w_uq[0], w_ukv[0], q_a_g[0], kv_a_g[0], qk_q_g[0],
                                              qk_k_g[0], dt_bias[0], a_log[0])
    g1 = norm1_g[0][None, :]
    conv_wl, conv_bl = conv_w[0], conv_b[0][None, :]
    dsk = jnp.repeat(d_skip[0], SSM_HEAD_DIM, axis=1)
    sel = _expansion_selector()

    narrow_c, wide_c = _front_call(ctx, g1, mod_c, w_narrow, w_wide, conv_wl, conv_bl, n_ctx, True, "front_ctx")
    narrow_x, wide_x = _front_call(x, g1, mod_x, w_narrow, w_wide, conv_wl, conv_bl, 256, False, "front")

    t1c, t2c = _identity_rope_tables(n_ctx)
    t1x, t2x = _rope_tables(n_lat)
    q_x, k_all, vt_all, cum, dc3 = _mla_call(narrow_c, narrow_x, jnp.concatenate([t1c, t1x], axis=0),
                                             jnp.concatenate([t2c, t2x], axis=0), consts, "mla")

    p_c, crow_c = _group_scan_inputs(cum[:, :n_ctx], dc3[:, :n_ctx])
    p_x, crow_x = _group_scan_inputs(cum[:, n_ctx:], dc3[:, n_ctx:])
    h_zero = jnp.zeros((b, SSM_GROUPS, 2, D_STATE, GROUP_W), F32)
    _, h_ctx = _ssd_call(wide_c, p_c, crow_c, sel, dsk, h_zero, "ssd_ctx")
    y_x, _ = _ssd_call(wide_x, p_x, crow_x, sel, dsk, h_ctx, "ssd")

    att = _attn_call(q_x, k_all, vt_all, 512, 2, "attn")

    x1 = _merge_call(att, y_x, wide_x, x, mod_x, ssm_norm_g[0][None, :], w_proj_a[0].astype(BF16),
                     w_proj_b[0].astype(BF16), w_out[0].astype(BF16), 256, "merge")
    return _mlp_call(x1, norm2_g[0][None, :], mod_x, w_mlp1[0].astype(BF16), w_mlp2[0].astype(BF16), 256, "mlp")
```

```python
import functools

import numpy as np
import jax
import jax.numpy as jnp
from jax import lax
from jax.experimental import pallas as pl
from jax.experimental.pallas import tpu as pltpu

F32 = jnp.float32
BF16 = jnp.bfloat16

D_MODEL = 1024
GRID_W = 64
MLA_HEADS = 16
QK_NOPE = 64
QK_ROPE = 32
V_DIM = 64
Q_LORA = 256
KV_LORA = 128
ROPE_PAIRS = QK_ROPE // 4
ROPE_BASE = 10000.0
ATTN_SCALE = (QK_NOPE + QK_ROPE) ** -0.5
LOG2_E = 1.4426950408889634
D_INNER = 2 * D_MODEL
SSM_HEAD_DIM = 64
SSM_HEADS = D_INNER // SSM_HEAD_DIM
SSM_GROUPS = 8
HEADS_PER_GROUP = SSM_HEADS // SSM_GROUPS
D_STATE = 128
D_CONV = 5
CHUNK = 128
CONV_DIM = D_INNER + 2 * SSM_GROUPS * D_STATE
D_FF = 4 * D_MODEL
N_MOD = 6
EPS = 1e-6

LANES = 128
HEAD_PAD = 128
GROUP_W = HEADS_PER_GROUP * SSM_HEAD_DIM
NARROW_W = 768
WIDE_W = D_INNER + CONV_DIM + 2 * D_MODEL
VMEM_LIMIT = 56 * 1024 * 1024


def _sigmoid(x):
    return 1.0 / (1.0 + jnp.exp(-x))


def _split3(x):
    hi = x.astype(BF16)
    r1 = x - hi.astype(F32)
    mid = r1.astype(BF16)
    lo = (r1 - mid.astype(F32)).astype(BF16)
    return hi, mid, lo


def _dot(a, b):
    return jnp.dot(a, b, preferred_element_type=F32)


def _dot_exact_rhs(a, sel):
    hi, mid, lo = _split3(a)
    return _dot(hi, sel) + _dot(mid, sel) + _dot(lo, sel)


def _dot_exact_lhs(sel, b):
    hi, mid, lo = _split3(b)
    return _dot(sel, hi) + _dot(sel, mid) + _dot(sel, lo)


def _params(*sem, flags=None):
    return pltpu.CompilerParams(dimension_semantics=sem, vmem_limit_bytes=VMEM_LIMIT, flags=flags)


def _mod_kernel(c_ref, w_ref, b_ref, o_ref):
    c = c_ref[...]
    a = c * _sigmoid(c)
    w = w_ref[...]
    a_hi = a.astype(BF16)
    a_lo = (a - a_hi.astype(F32)).astype(BF16)
    w_hi = w.astype(BF16)
    w_lo = (w - w_hi.astype(F32)).astype(BF16)
    o_ref[...] = _dot(a_hi, w_hi) + _dot(a_hi, w_lo) + _dot(a_lo, w_hi) + b_ref[...]


def _mod_call(cvec, w_mod, b_mod):
    rows, d = cvec.shape
    n_out = w_mod.shape[1]
    tn = 1024
    return pl.pallas_call(
        _mod_kernel,
        out_shape=jax.ShapeDtypeStruct((rows, n_out), F32),
        grid=(n_out // tn,),
        in_specs=[pl.BlockSpec((rows, d), lambda j: (0, 0)),
                  pl.BlockSpec((d, tn), lambda j: (0, j)),
                  pl.BlockSpec((1, tn), lambda j: (0, j))],
        out_specs=pl.BlockSpec((rows, tn), lambda j: (0, j)),
        compiler_params=_params("parallel"),
        name="mod",
    )(cvec, w_mod, b_mod)


FRONT_HALO = 16
FRONT_COLS = 1024


def _front_kernel(x_ref, xp_ref, xn_ref, g_ref, mod_ref, wn_ref, ww_ref, cw_ref, cb_ref, narrow_ref, wide_ref, pre_ref):
    i = pl.program_id(1)
    last = pl.num_programs(1) - 1
    tm = x_ref.shape[1]
    g = g_ref[...]
    shift = mod_ref[0, 0:1, :]
    scale = mod_ref[0, 1:2, :]

    def normmod(xv):
        ms = jnp.mean(xv * xv, axis=-1, keepdims=True)
        return xv * lax.rsqrt(ms + EPS) * g * (1.0 + scale) + shift

    hc = normmod(x_ref[0]).astype(BF16)
    hp = (normmod(xp_ref[0]) * (i > 0).astype(F32)).astype(BF16)
    hn = (normmod(xn_ref[0]) * (i < last).astype(F32)).astype(BF16)
    h_ext = jnp.concatenate([hp, hc, hn], axis=0)
    narrow_ref[0] = _dot(hc, wn_ref[...])
    pad = (D_CONV - 1) // 2
    n_chunks = WIDE_W // FRONT_COLS
    is_conv = [D_INNER <= j * FRONT_COLS < D_INNER + CONV_DIM for j in range(n_chunks)]
    conv_js = [j for j in range(n_chunks) if is_conv[j]]
    plain_js = [j for j in range(n_chunks) if not is_conv[j]]
    order = [j for pair in zip(conv_js, plain_js) for j in pair] + conv_js[len(plain_js):] + plain_js[len(conv_js):]
    for j in order:
        lo = j * FRONT_COLS
        cols = slice(lo, lo + FRONT_COLS)
        if D_INNER <= lo < D_INNER + CONV_DIM:
            slot = j % 2
            pre_ref[slot] = _dot(h_ext, ww_ref[:, cols])
            cc = slice(lo - D_INNER, lo - D_INNER + FRONT_COLS)
            acc = cb_ref[:, cc] + cw_ref[0:1, cc] * pre_ref[slot, pl.ds(FRONT_HALO - pad, tm), :]
            for k in range(1, D_CONV):
                acc = acc + cw_ref[k:k + 1, cc] * pre_ref[slot, pl.ds(FRONT_HALO - pad + k, tm), :]
            wide_ref[0, :, cols] = (acc * _sigmoid(acc)).astype(BF16)
        else:
            wide_ref[0, :, cols] = _dot(hc, ww_ref[:, cols]).astype(BF16)


def _front_call(x, g, mod, w_narrow, w_wide, conv_w, conv_b, tm, shared_mod, name):
    b, n, d = x.shape
    per = tm // FRONT_HALO
    n_halo = n // FRONT_HALO
    mod_map = (lambda bi, i: (0, 0, 0)) if shared_mod else (lambda bi, i: (bi, 0, 0))
    resident = lambda a: pl.BlockSpec(a.shape, lambda bi, i: (0,) * a.ndim, pipeline_mode=pl.Buffered(1))
    return pl.pallas_call(
        _front_kernel,
        out_shape=[jax.ShapeDtypeStruct((b, n, NARROW_W), F32), jax.ShapeDtypeStruct((b, n, WIDE_W), BF16)],
        grid=(b, n // tm),
        in_specs=[pl.BlockSpec((1, tm, d), lambda bi, i: (bi, i, 0)),
                  pl.BlockSpec((1, FRONT_HALO, d), lambda bi, i: (bi, jnp.maximum(i * per - 1, 0), 0)),
                  pl.BlockSpec((1, FRONT_HALO, d), lambda bi, i: (bi, jnp.minimum((i + 1) * per, n_halo - 1), 0)),
                  pl.BlockSpec((1, d), lambda bi, i: (0, 0)),
                  pl.BlockSpec((1, 8, d), mod_map),
                  resident(w_narrow), resident(w_wide), resident(conv_w), resident(conv_b)],
        out_specs=[pl.BlockSpec((1, tm, NARROW_W), lambda bi, i: (bi, i, 0)),
                   pl.BlockSpec((1, tm, WIDE_W), lambda bi, i: (bi, i, 0))],
        scratch_shapes=[pltpu.VMEM((2, tm + 2 * FRONT_HALO, FRONT_COLS), F32)],
        compiler_params=_params("parallel", "parallel"),
        name=name,
    )(x, x, x, g, mod, w_narrow, w_wide, conv_w, conv_b)


def _segment_rsqrt(t, e_ref, et_ref):
    ms = _dot((t * t).astype(BF16), e_ref[...])
    r = lax.rsqrt(ms + EPS)
    hi = r.astype(BF16)
    lo = (r - hi.astype(F32)).astype(BF16)
    return _dot(jnp.concatenate([hi, lo], axis=1), et_ref[...])


def _mla_kernel(sc_ref, sx_ref, t1_ref, t2_ref, qag_ref, kvag_ref, wuq_ref, wukv_ref, e_ref, et_ref,
                gq_ref, gk_ref, ga_ref, gb_ref, dtb_ref, alane_ref, perm_ref, q_ref, k_ref, vt_ref, crow_ref, pg_ref):
    is_ctx = pl.program_id(1) == pl.num_programs(1) - 1
    s = jnp.where(is_ctx, sc_ref[0], sx_ref[0])
    t1 = t1_ref[...]
    t2 = t2_ref[...]
    lane = lax.broadcasted_iota(jnp.int32, (1, LANES), 1)
    n_heads = MLA_HEADS
    k_w = n_heads * HEAD_PAD

    ckv = s[:, Q_LORA:Q_LORA + KV_LORA]
    ckvn = ckv * lax.rsqrt(jnp.mean(ckv * ckv, axis=-1, keepdims=True) + EPS) * kvag_ref[...]
    kv = _dot(ckvn.astype(BF16), wukv_ref[...])
    kn = kv[:, :k_w]
    vt = kv[:, k_w:].T
    ones = jnp.ones((VT_ROWS - V_DIM, vt.shape[1]), F32)
    vt_ref[0] = jnp.concatenate(
        [blk for h in range(n_heads) for blk in (vt[h * V_DIM:(h + 1) * V_DIM], ones)], axis=0).astype(BF16)
    kn = kn * _segment_rsqrt(kn, e_ref, et_ref) * gk_ref[...]
    g1 = s[:, 384:512]
    g2 = s[:, 512:640]
    rope_lanes = (lane >= QK_NOPE) & (lane < QK_NOPE + QK_ROPE)
    kr_ms = jnp.sum(jnp.where(rope_lanes, g1 * g1, 0.0), axis=-1, keepdims=True) * (1.0 / QK_ROPE)
    krot = lax.rsqrt(kr_ms + EPS) * (g1 * ga_ref[...] * t1 + g2 * gb_ref[...] * t2)
    k_ref[0] = (kn + jnp.concatenate([krot] * n_heads, axis=1)).astype(BF16)

    @pl.when(jnp.logical_not(is_ctx))
    def _():
        cq = s[:, :Q_LORA]
        cqn = cq * lax.rsqrt(jnp.mean(cq * cq, axis=-1, keepdims=True) + EPS) * qag_ref[...]
        q = _dot(cqn.astype(BF16), wuq_ref[...])
        tq = t1 + jnp.where(lane < QK_NOPE, 1.0, 0.0)
        q = q * _segment_rsqrt(q, e_ref, et_ref) * gq_ref[...] * jnp.concatenate([tq] * n_heads, axis=1)
        q_ref[0] = (q * (ATTN_SCALE * LOG2_E)).astype(BF16)

    xdt = s[:, 640:768] + dtb_ref[...]
    dt = jnp.maximum(xdt, 0.0) + jnp.log1p(jnp.exp(-jnp.abs(xdt)))
    dt = jnp.where(lane < 2 * SSM_HEADS, dt, 0.0)
    da = dt * (-LOG2_E * jnp.exp(alane_ref[...]))
    ii = lax.broadcasted_iota(jnp.int32, (CHUNK, CHUNK), 0)
    jj = lax.broadcasted_iota(jnp.int32, (CHUNK, CHUNK), 1)
    tri_f = (jj <= ii).astype(BF16)
    tri_b = (jj >= ii).astype(BF16)
    fwd_lane = (lane % (2 * HEADS_PER_GROUP)) < HEADS_PER_GROUP
    cums = []
    for ch in range(s.shape[0] // CHUNK):
        pieces = _split3(da[ch * CHUNK:(ch + 1) * CHUNK])
        cf = sum(_dot(tri_f, p) for p in pieces)
        cb = sum(_dot(tri_b, p) for p in pieces)
        cums.append(jnp.where(fwd_lane, cf, cb))
        crow_ref[0, ch] = cums[-1].T[:2 * SSM_HEADS]
    cum = jnp.concatenate(cums, axis=0)
    d3 = _split3(dt)
    c3 = _split3(cum)
    dc3 = jnp.concatenate([d3[0], c3[0], d3[1], c3[1], d3[2], c3[2]], axis=1)
    pg_ref[0] = _dot(dc3, perm_ref[...]).astype(BF16)


def _piece_permutation():
    per = 2 * HEADS_PER_GROUP
    perm = np.zeros((6 * LANES, SSM_GROUPS * LANES), np.float32)
    for piece in range(3):
        for kind in range(2):
            for g in range(SSM_GROUPS):
                for j in range(per):
                    perm[(piece * 2 + kind) * LANES + g * per + j, g * LANES + piece * 2 * per + kind * per + j] = 1.0
    return jnp.asarray(perm, BF16)


def _mla_call(narrow_c, narrow_x, t1, t2, consts, name):
    b, n_ctx, _ = narrow_c.shape
    n_lat = narrow_x.shape[1]
    tm = n_ctx
    s_all = n_ctx + n_lat
    last_latent = n_lat // tm - 1
    k_w = MLA_HEADS * HEAD_PAD
    full = lambda a: pl.BlockSpec(a.shape, lambda bi, i: (0,) * a.ndim)
    latent_tile = lambda bi, i: (bi, jnp.minimum(i, last_latent), 0)
    return pl.pallas_call(
        _mla_kernel,
        out_shape=[jax.ShapeDtypeStruct((b, n_lat, k_w), BF16),
                   jax.ShapeDtypeStruct((b, s_all, k_w), BF16),
                   jax.ShapeDtypeStruct((b, MLA_HEADS * VT_ROWS, s_all), BF16),
                   jax.ShapeDtypeStruct((b, s_all // CHUNK, 2 * SSM_HEADS, CHUNK), F32),
                   jax.ShapeDtypeStruct((b, s_all, SSM_GROUPS * LANES), BF16)],
        grid=(b, s_all // tm),
        in_specs=[pl.BlockSpec((1, tm, NARROW_W), lambda bi, i: (bi, 0, 0)),
                  pl.BlockSpec((1, tm, NARROW_W), latent_tile),
                  pl.BlockSpec((tm, LANES), lambda bi, i: (i, 0)),
                  pl.BlockSpec((tm, LANES), lambda bi, i: (i, 0)),
                  *[full(a) for a in consts]],
        out_specs=[pl.BlockSpec((1, tm, k_w), latent_tile),
                   pl.BlockSpec((1, tm, k_w), lambda bi, i: (bi, i, 0)),
                   pl.BlockSpec((1, MLA_HEADS * VT_ROWS, tm), lambda bi, i: (bi, 0, i)),
                   pl.BlockSpec((1, tm // CHUNK, 2 * SSM_HEADS, CHUNK), lambda bi, i: (bi, i, 0, 0)),
                   pl.BlockSpec((1, tm, SSM_GROUPS * LANES), lambda bi, i: (bi, i, 0))],
        compiler_params=_params("parallel", "arbitrary"),
        name=name,
    )(narrow_c, narrow_x, t1, t2, *consts)


EXP_W = 4 * GROUP_W


def _ssd_kernel(nc, xs_ref, b_ref, c_ref, p_ref, crow_ref, sel_ref, dsk_ref, h0_ref,
                y_ref, hout_ref, yacc_ref, s_ref, e_ref, dec_ref, h_ref):
    q = CHUNK
    r_heads = HEADS_PER_GROUP
    wide_l = r_heads * q
    ii_w = lax.broadcasted_iota(jnp.int32, (q, wide_l), 0)
    jj_w = lax.broadcasted_iota(jnp.int32, (q, wide_l), 1) % q
    keep_f = jj_w <= ii_w
    keep_b = jj_w >= ii_w
    lane_head = lax.broadcasted_iota(jnp.int32, (1, GROUP_W), 1) // SSM_HEAD_DIM
    low_half = lax.broadcasted_iota(jnp.int32, (1, LANES), 1) < SSM_HEAD_DIM
    dsk = dsk_ref[0:1, :] + dsk_ref[1:2, :]
    gw = GROUP_W

    def per_head_128(c64):
        out = []
        for pair in range(r_heads // 2):
            v = c64[:, pair * LANES:(pair + 1) * LANES]
            sw = pltpu.roll(v, SSM_HEAD_DIM, axis=1)
            out += [jnp.where(low_half, v, sw), jnp.where(low_half, sw, v)]
        return jnp.concatenate(out, axis=1)

    def block_diag(xdt):
        xb = xdt.astype(BF16)
        return jnp.concatenate([jnp.where(lane_head == r, xb, jnp.zeros_like(xb)) for r in range(r_heads)], axis=0)

    def local(c, carry):
        rows = pl.ds(pl.multiple_of(c * q, q), q)
        x = xs_ref[0, rows, :].astype(F32)
        bm = b_ref[0, rows, :]
        cm = c_ref[0, rows, :]
        ex = _dot(p_ref[0, rows, :], sel_ref[...])
        dte_f, dte_b = ex[:, 0:gw], ex[:, gw:2 * gw]
        c64_f, c64_b = ex[:, 2 * gw:3 * gw], ex[:, 3 * gw:4 * gw]
        cw_f, cw_b = per_head_128(c64_f), per_head_128(c64_b)
        crow = crow_ref[0, c]
        flat_f = jnp.concatenate([crow[r:r + 1, :] for r in range(r_heads)], axis=1)
        flat_b = jnp.concatenate([crow[r_heads + r:r_heads + r + 1, :] for r in range(r_heads)], axis=1)
        cb = lax.dot_general(cm, bm, (((1,), (1,)), ((), ())), preferred_element_type=F32)
        cb4 = jnp.concatenate([cb] * r_heads, axis=1)
        l_f = (jnp.where(keep_f, jnp.exp2(cw_f - flat_f), 0.0) * cb4).astype(BF16)
        l_b = (jnp.where(keep_b, jnp.exp2(cw_b - flat_b), 0.0) * cb4).astype(BF16)
        xdt_f = x * dte_f
        xdt_b = x * dte_b
        y = _dot(jnp.concatenate([l_f, l_b], axis=1),
                 jnp.concatenate([block_diag(xdt_f), block_diag(xdt_b)], axis=0))
        yacc_ref[rows, :] = y + dsk * x
        edge_f = c64_f[q - 1:q, :]
        edge_b = c64_b[0:1, :]
        xw = jnp.concatenate([(xdt_f * jnp.exp2(edge_f - c64_f)).astype(BF16),
                              (xdt_b * jnp.exp2(edge_b - c64_b)).astype(BF16)], axis=1)
        st = lax.dot_general(bm, xw, (((0,), (0,)), ((), ())), preferred_element_type=F32)
        s_ref[0, c] = st[:, :gw]
        s_ref[1, c] = st[:, gw:]
        e_ref[0, rows, :] = jnp.exp2(c64_f)
        e_ref[1, rows, :] = jnp.exp2(c64_b)
        dec_ref[0, c] = jnp.broadcast_to(jnp.exp2(edge_f), (8, gw))
        dec_ref[1, c] = jnp.broadcast_to(jnp.exp2(edge_b), (8, gw))
        return carry

    lax.fori_loop(0, nc, local, 0, unroll=4 if nc % 4 == 0 else 2)

    h_ref[0] = h0_ref[0, 0, 0]
    h_ref[1] = h0_ref[0, 0, 1]

    def carry_states(t, carry):
        for d, c in ((0, t), (1, nc - 1 - t)):
            rows = pl.ds(pl.multiple_of(c * q, q), q)
            h = h_ref[d]
            yo = _dot(c_ref[0, rows, :], h.astype(BF16)) * e_ref[d, rows, :]
            yacc_ref[rows, :] = yacc_ref[rows, :] + yo
            h_ref[d] = h * dec_ref[d, c, 0:1, :] + s_ref[d, c]
        return carry

    lax.fori_loop(0, nc, carry_states, 0, unroll=2)
    hout_ref[0, 0, 0] = h_ref[0]
    hout_ref[0, 0, 1] = h_ref[1]

    def emit(c, carry):
        rows = pl.ds(pl.multiple_of(c * q, q), q)
        y_ref[0, rows, :] = yacc_ref[rows, :].astype(y_ref.dtype)
        return carry

    lax.fori_loop(0, nc, emit, 0)


def _ssd_call(wide, pieces, crow, seq_block, sel, dsk, h0, name):
    xbc = wide
    b, n, _ = wide.shape
    nc = n // CHUNK
    g = SSM_GROUPS
    x_col0 = D_INNER // GROUP_W
    b_col0 = 2 * D_INNER // D_STATE
    c_col0 = b_col0 + g
    return pl.pallas_call(
        functools.partial(_ssd_kernel, nc),
        out_shape=[jax.ShapeDtypeStruct((b, n, D_INNER), BF16),
                   jax.ShapeDtypeStruct((b, g, 2, D_STATE, GROUP_W), F32)],
        grid=(b, g),
        in_specs=[pl.BlockSpec((1, n, GROUP_W), lambda bi, gi: (bi, 0, x_col0 + gi)),
                  pl.BlockSpec((1, n, D_STATE), lambda bi, gi: (bi, 0, b_col0 + gi)),
                  pl.BlockSpec((1, n, D_STATE), lambda bi, gi: (bi, 0, c_col0 + gi)),
                  pl.BlockSpec((1, n, LANES), lambda bi, gi: (bi, seq_block, gi)),
                  pl.BlockSpec((1, nc, 2 * HEADS_PER_GROUP, CHUNK), lambda bi, gi: (bi, seq_block, gi, 0)),
                  pl.BlockSpec((LANES, EXP_W), lambda bi, gi: (0, 0)),
                  pl.BlockSpec((2, GROUP_W), lambda bi, gi: (0, gi)),
                  pl.BlockSpec((1, 1, 2, D_STATE, GROUP_W), lambda bi, gi: (bi, gi, 0, 0, 0))],
        out_specs=[pl.BlockSpec((1, n, GROUP_W), lambda bi, gi: (bi, 0, gi)),
                   pl.BlockSpec((1, 1, 2, D_STATE, GROUP_W), lambda bi, gi: (bi, gi, 0, 0, 0))],
        scratch_shapes=[pltpu.VMEM((n, GROUP_W), F32),
                        pltpu.VMEM((2, nc, D_STATE, GROUP_W), F32),
                        pltpu.VMEM((2, n, GROUP_W), F32),
                        pltpu.VMEM((2, nc, 8, GROUP_W), F32),
                        pltpu.VMEM((2, D_STATE, GROUP_W), F32)],
        compiler_params=_params("parallel", "parallel"),
        name=name,
    )(xbc, xbc, xbc, pieces, crow, sel, dsk, h0)


VT_ROWS = V_DIM + 16
KEY_CHUNK = 256


def _attn_kernel(q_ref, k_ref, vt_ref, o_ref, s_a, s_b, m_a, m_b, p_ref):
    t = pl.program_id(0)

    @pl.when(t == 0)
    def _():
        s_b[...] = jnp.zeros_like(s_b)
        m_b[...] = jnp.zeros_like(m_b)

    def step(s_w, m_w, s_r, m_r, half):
        n_keys = k_ref.shape[1]
        m_prev = m_r[...]
        k_rows = []
        for c0 in range(0, n_keys, KEY_CHUNK):
            pc = jnp.exp2(s_r[c0:c0 + KEY_CHUNK, :] - m_prev).astype(BF16)
            p_ref[c0:c0 + KEY_CHUNK, :] = pc
            tile = jnp.max(pc.reshape(KEY_CHUNK // 16, 16, pc.shape[1]), axis=0)
            tile = functools.reduce(jnp.maximum, [tile[:, l:l + LANES] for l in range(0, tile.shape[1], LANES)])
            zero = tile * jnp.zeros((), BF16)
            kc = k_ref[0, c0:c0 + KEY_CHUNK, :]
            k_rows.append(kc + jnp.broadcast_to(zero[None], (KEY_CHUNK // 16, 16, LANES)).reshape(KEY_CHUNK, LANES))
        st = lax.dot_general(jnp.concatenate(k_rows, axis=0), q_ref[0], (((1,), (1,)), ((), ())),
                             preferred_element_type=F32)
        s_w[...] = st
        m_w[...] = jnp.max(st, axis=0, keepdims=True)
        o = _dot(vt_ref[0], p_ref[...])
        out = (o[:V_DIM] / o[V_DIM:V_DIM + 1]).T
        o_ref[0, :, half * V_DIM:(half + 1) * V_DIM] = out.astype(o_ref.dtype)

    @pl.when(t % 2 == 0)
    def _():
        step(s_a, m_a, s_b, m_b, 1)

    @pl.when(t % 2 == 1)
    def _():
        step(s_b, m_b, s_a, m_a, 0)


def _attn_call(q, k, vt, tq, name):
    b, n, _ = q.shape
    s = k.shape[1]
    nq = n // tq
    items = b * nq * MLA_HEADS

    def item(t):
        return t // (nq * MLA_HEADS), (t // MLA_HEADS) % nq, t % MLA_HEADS

    def cur(t):
        return item(jnp.minimum(t, items - 1))

    def prev(t):
        return item(jnp.maximum(t - 1, 0))

    return pl.pallas_call(
        _attn_kernel,
        out_shape=jax.ShapeDtypeStruct((b, n, MLA_HEADS * V_DIM), BF16),
        grid=(items + 1,),
        in_specs=[pl.BlockSpec((1, tq, HEAD_PAD), lambda t: (cur(t)[0], cur(t)[1], cur(t)[2])),
                  pl.BlockSpec((1, s, HEAD_PAD), lambda t: (cur(t)[0], 0, cur(t)[2])),
                  pl.BlockSpec((1, VT_ROWS, s), lambda t: (prev(t)[0], prev(t)[2], 0))],
        out_specs=pl.BlockSpec((1, tq, 2 * V_DIM), lambda t: (prev(t)[0], prev(t)[1], prev(t)[2] // 2)),
        scratch_shapes=[pltpu.VMEM((s, tq), F32), pltpu.VMEM((s, tq), F32),
                        pltpu.VMEM((1, tq), F32), pltpu.VMEM((1, tq), F32), pltpu.VMEM((s, tq), BF16)],
        compiler_params=_params("arbitrary"),
        name=name,
    )(q, k, vt)


def _merge_kernel(att_ref, y_ref, z_ref, ga_ref, gb_ref, x_ref, mod_ref, sg_ref, wa_ref, wb_ref, wo_ref, o_ref):
    y = y_ref[0].astype(F32)
    z = z_ref[0].astype(F32)
    u = y * (z * _sigmoid(z))
    gw = D_INNER // SSM_GROUPS
    parts = []
    for g in range(SSM_GROUPS):
        ug = u[:, g * gw:(g + 1) * gw]
        parts.append(ug * lax.rsqrt(jnp.mean(ug * ug, axis=-1, keepdims=True) + EPS))
    un = (jnp.concatenate(parts, axis=1) * sg_ref[...]).astype(BF16)
    a = _dot(att_ref[0], wa_ref[...])
    bb = _dot(un, wb_ref[...])
    merged = _sigmoid(ga_ref[0].astype(F32)) * a + _sigmoid(gb_ref[0].astype(F32)) * bb
    o = _dot(merged.astype(BF16), wo_ref[...])
    o_ref[0] = x_ref[0] + mod_ref[0, 2:3, :] * o


def _merge_call(att, y, wide, x, mod, sg, wa, wb, wo, tm, name):
    b, n, d = x.shape
    ga_blk = (D_INNER + CONV_DIM) // d
    full = lambda a: pl.BlockSpec(a.shape, lambda bi, i: (0,) * a.ndim)
    return pl.pallas_call(
        _merge_kernel,
        out_shape=jax.ShapeDtypeStruct((b, n, d), F32),
        grid=(b, n // tm),
        in_specs=[pl.BlockSpec((1, tm, MLA_HEADS * V_DIM), lambda bi, i: (bi, i, 0)),
                  pl.BlockSpec((1, tm, D_INNER), lambda bi, i: (bi, i, 0)),
                  pl.BlockSpec((1, tm, D_INNER), lambda bi, i: (bi, i, 0)),
                  pl.BlockSpec((1, tm, d), lambda bi, i: (bi, i, ga_blk)),
                  pl.BlockSpec((1, tm, d), lambda bi, i: (bi, i, ga_blk + 1)),
                  pl.BlockSpec((1, tm, d), lambda bi, i: (bi, i, 0)),
                  pl.BlockSpec((1, 8, d), lambda bi, i: (bi, 0, 0)),
                  full(sg), full(wa), full(wb), full(wo)],
        out_specs=pl.BlockSpec((1, tm, d), lambda bi, i: (bi, i, 0)),
        compiler_params=_params("parallel", "parallel"),
        name=name,
    )(att, y, wide, wide, wide, x, mod, sg, wa, wb, wo)


def _mlp_kernel(ff_chunk, x_ref, g_ref, mod_ref, w1_ref, w2_ref, o_ref):
    x = x_ref[0]
    ms = jnp.mean(x * x, axis=-1, keepdims=True)
    xn = x * lax.rsqrt(ms + EPS) * g_ref[...]
    h = (xn * (1.0 + mod_ref[0, 4:5, :]) + mod_ref[0, 3:4, :]).astype(BF16)
    acc = None
    for c in range(D_FF // ff_chunk):
        a = _dot(h, w1_ref[:, c * ff_chunk:(c + 1) * ff_chunk])
        a = jnp.maximum(a, 0.0)
        part = _dot((a * a).astype(BF16), w2_ref[c * ff_chunk:(c + 1) * ff_chunk, :])
        acc = part if acc is None else acc + part
    o_ref[0] = x + mod_ref[0, 5:6, :] * acc


def _mlp_call(x, g, mod, w1, w2, tm, name):
    b, n, d = x.shape
    return pl.pallas_call(
        functools.partial(_mlp_kernel, 1024),
        out_shape=jax.ShapeDtypeStruct((b, n, d), F32),
        grid=(b, n // tm),
        in_specs=[pl.BlockSpec((1, tm, d), lambda bi, i: (bi, i, 0)),
                  pl.BlockSpec((1, d), lambda bi, i: (0, 0)),
                  pl.BlockSpec((1, 8, d), lambda bi, i: (bi, 0, 0)),
                  pl.BlockSpec(w1.shape, lambda bi, i: (0, 0), pipeline_mode=pl.Buffered(1)),
                  pl.BlockSpec(w2.shape, lambda bi, i: (0, 0), pipeline_mode=pl.Buffered(1))],
        out_specs=pl.BlockSpec((1, tm, d), lambda bi, i: (bi, i, 0)),
        compiler_params=_params("parallel", "parallel"),
        name=name,
    )(x, g, mod, w1, w2)


_ROPE_SWAP = np.concatenate([np.arange(8, 16), np.arange(0, 8), np.arange(24, 32), np.arange(16, 24)])
_DT_LANE_ORDER = np.array([d * SSM_HEADS + g * HEADS_PER_GROUP + r for g in range(SSM_GROUPS) for d in range(2)
                           for r in range(HEADS_PER_GROUP)])


def _rope_tables(n_lat):
    rows = n_lat // GRID_W
    row = jnp.repeat(jnp.arange(rows), GRID_W)
    col = jnp.tile(jnp.arange(GRID_W), rows)
    freqs = ROPE_BASE ** (-jnp.arange(ROPE_PAIRS, dtype=F32) / ROPE_PAIRS)
    ang = jnp.stack([row, col], axis=-1).astype(F32)[..., None] * freqs
    cos, sin = jnp.cos(ang), jnp.sin(ang)
    cos32 = jnp.stack([cos, cos], axis=2).reshape(n_lat, QK_ROPE)
    sin32 = jnp.stack([-sin, sin], axis=2).reshape(n_lat, QK_ROPE)
    zeros = jnp.zeros((n_lat, QK_NOPE), F32)
    return (jnp.concatenate([zeros, cos32, sin32], axis=1),
            jnp.concatenate([zeros, sin32, cos32], axis=1))


def _identity_rope_tables(n):
    zeros = jnp.zeros((n, QK_NOPE), F32)
    one = jnp.ones((n, QK_ROPE), F32)
    zero = jnp.zeros((n, QK_ROPE), F32)
    return jnp.concatenate([zeros, one, zero], axis=1), jnp.concatenate([zeros, zero, one], axis=1)


def _layout_params(w_in, w_uq, w_ukv, q_a_g, kv_a_g, qk_q_g, qk_k_g, dt_bias, a_log):
    d = w_in.shape[0]
    o_cq, o_ckv, o_kr = 0, Q_LORA, Q_LORA + KV_LORA
    o_z = o_kr + QK_ROPE
    o_xbc = o_z + D_INNER
    o_dt = o_xbc + CONV_DIM
    o_ga = o_dt + 2 * SSM_HEADS
    kr = w_in[:, o_kr:o_kr + QK_ROPE]
    kr_sw = kr[:, _ROPE_SWAP]
    z64 = jnp.zeros((d, 64), F32)
    w_narrow = jnp.concatenate(
        [w_in[:, o_cq:o_kr], z64, kr, kr_sw, z64, kr_sw, kr, w_in[:, o_dt:o_ga][:, _DT_LANE_ORDER], z64],
        axis=1).astype(BF16)
    w_wide = jnp.concatenate([w_in[:, o_z:o_dt], w_in[:, o_ga:]], axis=1).astype(BF16)

    hq = w_uq.reshape(Q_LORA, MLA_HEADS, QK_NOPE + QK_ROPE)
    rope_q = hq[:, :, QK_NOPE:]
    wuq = jnp.concatenate([hq[:, :, :QK_NOPE], rope_q, rope_q[:, :, _ROPE_SWAP]], axis=2)
    wuq = wuq.reshape(Q_LORA, MLA_HEADS * HEAD_PAD).astype(BF16)
    hkv = w_ukv.reshape(KV_LORA, MLA_HEADS, QK_NOPE + V_DIM)
    wk = jnp.concatenate([hkv[:, :, :QK_NOPE], jnp.zeros((KV_LORA, MLA_HEADS, HEAD_PAD - QK_NOPE), F32)], axis=2)
    wukv = jnp.concatenate([wk.reshape(KV_LORA, MLA_HEADS * HEAD_PAD),
                            hkv[:, :, QK_NOPE:].reshape(KV_LORA, MLA_HEADS * V_DIM)], axis=1).astype(BF16)

    gq_r = qk_q_g[QK_NOPE:]
    gq = jnp.tile(jnp.concatenate([qk_q_g[:QK_NOPE], gq_r, gq_r[_ROPE_SWAP]]), MLA_HEADS)[None, :]
    gk = jnp.tile(jnp.concatenate([qk_k_g[:QK_NOPE], jnp.zeros((HEAD_PAD - QK_NOPE,), F32)]), MLA_HEADS)[None, :]
    gk_r = qk_k_g[QK_NOPE:]
    z64v = jnp.zeros((QK_NOPE,), F32)
    ga = jnp.concatenate([z64v, gk_r, gk_r[_ROPE_SWAP]])[None, :]
    gb = jnp.concatenate([z64v, gk_r[_ROPE_SWAP], gk_r])[None, :]
    lane_pad = jnp.zeros((LANES - 2 * SSM_HEADS,), F32)
    dtb = jnp.concatenate([dt_bias.reshape(-1)[_DT_LANE_ORDER], lane_pad])[None, :]
    alane = jnp.concatenate([a_log.astype(F32).reshape(-1)[_DT_LANE_ORDER], lane_pad])[None, :]

    lane = np.arange(MLA_HEADS * HEAD_PAD)
    head, off = lane // HEAD_PAD, lane % HEAD_PAD
    e = np.zeros((MLA_HEADS * HEAD_PAD, LANES), np.float32)
    e[lane[off < QK_NOPE], 2 * head[off < QK_NOPE]] = 1.0 / QK_NOPE
    rope = (off >= QK_NOPE) & (off < QK_NOPE + QK_ROPE)
    e[lane[rope], 2 * head[rope] + 1] = 1.0 / QK_ROPE
    et = np.zeros((LANES, MLA_HEADS * HEAD_PAD), np.float32)
    et[2 * head[off < QK_NOPE], lane[off < QK_NOPE]] = 1.0
    et[2 * head[off >= QK_NOPE] + 1, lane[off >= QK_NOPE]] = 1.0
    et = np.concatenate([et, et], axis=0)
    consts = (q_a_g[None, :], kv_a_g[None, :], wuq, wukv, jnp.asarray(e, BF16), jnp.asarray(et, BF16),
              gq, gk, ga, gb, dtb, alane, _piece_permutation())
    return w_narrow, w_wide, consts


def _expansion_selector():
    r = HEADS_PER_GROUP
    sel = np.zeros((LANES, EXP_W), np.float32)
    for piece in range(3):
        for d in range(2):
            for h in range(r):
                row_dt = piece * 4 * r + d * r + h
                row_cum = piece * 4 * r + 2 * r + d * r + h
                sel[row_dt, d * GROUP_W + h * SSM_HEAD_DIM:d * GROUP_W + (h + 1) * SSM_HEAD_DIM] = 1.0
                base = 2 * GROUP_W + d * GROUP_W + h * SSM_HEAD_DIM
                sel[row_cum, base:base + SSM_HEAD_DIM] = 1.0
    return jnp.asarray(sel, BF16)


def kernel(x, c, ctx, c_ctx, norm1_g, norm2_g, w_mod, b_mod, w_in, q_a_g, w_uq, kv_a_g, w_ukv, qk_q_g, qk_k_g,
           conv_w, conv_b, dt_bias, a_log, d_skip, ssm_norm_g, w_proj_a, w_proj_b, w_out, w_mlp1, w_mlp2):
    assert w_mod.shape[0] == 1, "single-layer block"
    b, n_lat, d = x.shape
    n_ctx = ctx.shape[1]

    cvec = jnp.concatenate([c, c_ctx[None, :], jnp.zeros((8 - b - 1, d), F32)], axis=0)
    mod = _mod_call(cvec, w_mod[0], b_mod[0][None, :]).reshape(8, N_MOD, d)
    mod = jnp.concatenate([mod, jnp.zeros((8, 8 - N_MOD, d), F32)], axis=1)
    mod_x, mod_c = mod[:b], mod[b:b + 1]

    w_narrow, w_wide, consts = _layout_params(w_in[0], w_uq[0], w_ukv[0], q_a_g[0], kv_a_g[0], qk_q_g[0],
                                              qk_k_g[0], dt_bias[0], a_log[0])
    g1 = norm1_g[0][None, :]
    conv_wl, conv_bl = conv_w[0], conv_b[0][None, :]
    dsk = jnp.repeat(d_skip[0], SSM_HEAD_DIM, axis=1)
    sel = _expansion_selector()

    narrow_c, wide_c = _front_call(ctx, g1, mod_c, w_narrow, w_wide, conv_wl, conv_bl, n_ctx, True, "front_ctx")
    narrow_x, wide_x = _front_call(x, g1, mod_x, w_narrow, w_wide, conv_wl, conv_bl, 256, False, "front")

    t1c, t2c = _identity_rope_tables(n_ctx)
    t1x, t2x = _rope_tables(n_lat)
    q_x, k_all, vt_all, crow, pieces = _mla_call(narrow_c, narrow_x, jnp.concatenate([t1x, t1c], axis=0),
                                                 jnp.concatenate([t2x, t2c], axis=0), consts, "mla")

    h_zero = jnp.zeros((b, SSM_GROUPS, 2, D_STATE, GROUP_W), F32)
    _, h_ctx = _ssd_call(wide_c, pieces, crow, n_lat // n_ctx, sel, dsk, h_zero, "ssd_ctx")
    y_x, _ = _ssd_call(wide_x, pieces, crow, 0, sel, dsk, h_ctx, "ssd")

    att = _attn_call(q_x, k_all, vt_all, 512, "attn")

    x1 = _merge_call(att, y_x, wide_x, x, mod_x, ssm_norm_g[0][None, :], w_proj_a[0].astype(BF16),
                     w_proj_b[0].astype(BF16), w_out[0].astype(BF16), 256, "merge")
    return _mlp_call(x1, norm2_g[0][None, :], mod_x, w_mlp1[0].astype(BF16), w_mlp2[0].astype(BF16), 256, "mlp")
```

```python
import functools

import numpy as np
import jax
import jax.numpy as jnp
from jax import lax
from jax.experimental import pallas as pl
from jax.experimental.pallas import tpu as pltpu

F32 = jnp.float32
BF16 = jnp.bfloat16

D_MODEL = 1024
GRID_W = 64
MLA_HEADS = 16
QK_NOPE = 64
QK_ROPE = 32
V_DIM = 64
Q_LORA = 256
KV_LORA = 128
ROPE_PAIRS = QK_ROPE // 4
ROPE_BASE = 10000.0
ATTN_SCALE = (QK_NOPE + QK_ROPE) ** -0.5
LOG2_E = 1.4426950408889634
D_INNER = 2 * D_MODEL
SSM_HEAD_DIM = 64
SSM_HEADS = D_INNER // SSM_HEAD_DIM
SSM_GROUPS = 8
HEADS_PER_GROUP = SSM_HEADS // SSM_GROUPS
D_STATE = 128
D_CONV = 5
CHUNK = 128
CONV_DIM = D_INNER + 2 * SSM_GROUPS * D_STATE
D_FF = 4 * D_MODEL
N_MOD = 6
EPS = 1e-6

LANES = 128
HEAD_PAD = 128
GROUP_W = HEADS_PER_GROUP * SSM_HEAD_DIM
NARROW_W = 768
WIDE_W = D_INNER + CONV_DIM + 2 * D_MODEL
VMEM_LIMIT = 56 * 1024 * 1024


def _sigmoid(x):
    return 1.0 / (1.0 + jnp.exp(-x))


def _split3(x):
    hi = x.astype(BF16)
    r1 = x - hi.astype(F32)
    mid = r1.astype(BF16)
    lo = (r1 - mid.astype(F32)).astype(BF16)
    return hi, mid, lo


def _dot(a, b):
    return jnp.dot(a, b, preferred_element_type=F32)


def _dot_exact_rhs(a, sel):
    hi, mid, lo = _split3(a)
    return _dot(hi, sel) + _dot(mid, sel) + _dot(lo, sel)


def _dot_exact_lhs(sel, b):
    hi, mid, lo = _split3(b)
    return _dot(sel, hi) + _dot(sel, mid) + _dot(sel, lo)


def _params(*sem, flags=None):
    return pltpu.CompilerParams(dimension_semantics=sem, vmem_limit_bytes=VMEM_LIMIT, flags=flags)


def _mod_kernel(c_ref, w_ref, b_ref, o_ref):
    c = c_ref[...]
    a = c * _sigmoid(c)
    w = w_ref[...]
    a_hi = a.astype(BF16)
    a_lo = (a - a_hi.astype(F32)).astype(BF16)
    w_hi = w.astype(BF16)
    w_lo = (w - w_hi.astype(F32)).astype(BF16)
    o_ref[...] = _dot(a_hi, w_hi) + _dot(a_hi, w_lo) + _dot(a_lo, w_hi) + b_ref[...]


def _mod_call(cvec, w_mod, b_mod):
    rows, d = cvec.shape
    n_out = w_mod.shape[1]
    tn = 1024
    return pl.pallas_call(
        _mod_kernel,
        out_shape=jax.ShapeDtypeStruct((rows, n_out), F32),
        grid=(n_out // tn,),
        in_specs=[pl.BlockSpec((rows, d), lambda j: (0, 0)),
                  pl.BlockSpec((d, tn), lambda j: (0, j)),
                  pl.BlockSpec((1, tn), lambda j: (0, j))],
        out_specs=pl.BlockSpec((rows, tn), lambda j: (0, j)),
        compiler_params=_params("parallel"),
        name="mod",
    )(cvec, w_mod, b_mod)


FRONT_HALO = 16
FRONT_COLS = 1024


def _front_kernel(x_ref, xp_ref, xn_ref, g_ref, mod_ref, wn_ref, ww_ref, cw_ref, cb_ref, narrow_ref, wide_ref, pre_ref):
    i = pl.program_id(1)
    last = pl.num_programs(1) - 1
    tm = x_ref.shape[1]
    g = g_ref[...]
    shift = mod_ref[0, 0:1, :]
    scale = mod_ref[0, 1:2, :]

    def normmod(xv):
        ms = jnp.mean(xv * xv, axis=-1, keepdims=True)
        return xv * lax.rsqrt(ms + EPS) * g * (1.0 + scale) + shift

    hc = normmod(x_ref[0]).astype(BF16)
    hp = (normmod(xp_ref[0]) * (i > 0).astype(F32)).astype(BF16)
    hn = (normmod(xn_ref[0]) * (i < last).astype(F32)).astype(BF16)
    h_ext = jnp.concatenate([hp, hc, hn], axis=0)
    narrow_ref[0] = _dot(hc, wn_ref[...])
    pad = (D_CONV - 1) // 2
    n_chunks = WIDE_W // FRONT_COLS
    is_conv = [D_INNER <= j * FRONT_COLS < D_INNER + CONV_DIM for j in range(n_chunks)]
    conv_js = [j for j in range(n_chunks) if is_conv[j]]
    plain_js = [j for j in range(n_chunks) if not is_conv[j]]
    order = [j for pair in zip(conv_js, plain_js) for j in pair] + conv_js[len(plain_js):] + plain_js[len(conv_js):]
    for j in order:
        lo = j * FRONT_COLS
        cols = slice(lo, lo + FRONT_COLS)
        if D_INNER <= lo < D_INNER + CONV_DIM:
            slot = j % 2
            pre_ref[slot] = _dot(h_ext, ww_ref[:, cols])
            cc = slice(lo - D_INNER, lo - D_INNER + FRONT_COLS)
            acc = cb_ref[:, cc] + cw_ref[0:1, cc] * pre_ref[slot, pl.ds(FRONT_HALO - pad, tm), :]
            for k in range(1, D_CONV):
                acc = acc + cw_ref[k:k + 1, cc] * pre_ref[slot, pl.ds(FRONT_HALO - pad + k, tm), :]
            wide_ref[0, :, cols] = (acc * _sigmoid(acc)).astype(BF16)
        else:
            wide_ref[0, :, cols] = _dot(hc, ww_ref[:, cols]).astype(BF16)


def _front_call(x, g, mod, w_narrow, w_wide, conv_w, conv_b, tm, shared_mod, name):
    b, n, d = x.shape
    per = tm // FRONT_HALO
    n_halo = n // FRONT_HALO
    mod_map = (lambda bi, i: (0, 0, 0)) if shared_mod else (lambda bi, i: (bi, 0, 0))
    resident = lambda a: pl.BlockSpec(a.shape, lambda bi, i: (0,) * a.ndim, pipeline_mode=pl.Buffered(1))
    return pl.pallas_call(
        _front_kernel,
        out_shape=[jax.ShapeDtypeStruct((b, n, NARROW_W), F32), jax.ShapeDtypeStruct((b, n, WIDE_W), BF16)],
        grid=(b, n // tm),
        in_specs=[pl.BlockSpec((1, tm, d), lambda bi, i: (bi, i, 0)),
                  pl.BlockSpec((1, FRONT_HALO, d), lambda bi, i: (bi, jnp.maximum(i * per - 1, 0), 0)),
                  pl.BlockSpec((1, FRONT_HALO, d), lambda bi, i: (bi, jnp.minimum((i + 1) * per, n_halo - 1), 0)),
                  pl.BlockSpec((1, d), lambda bi, i: (0, 0)),
                  pl.BlockSpec((1, 8, d), mod_map),
                  resident(w_narrow), resident(w_wide), resident(conv_w), resident(conv_b)],
        out_specs=[pl.BlockSpec((1, tm, NARROW_W), lambda bi, i: (bi, i, 0)),
                   pl.BlockSpec((1, tm, WIDE_W), lambda bi, i: (bi, i, 0))],
        scratch_shapes=[pltpu.VMEM((2, tm + 2 * FRONT_HALO, FRONT_COLS), F32)],
        compiler_params=_params("parallel", "parallel"),
        name=name,
    )(x, x, x, g, mod, w_narrow, w_wide, conv_w, conv_b)


def _segment_rsqrt(t, e_ref, et_ref):
    ms = _dot((t * t).astype(BF16), e_ref[...])
    r = lax.rsqrt(ms + EPS)
    hi = r.astype(BF16)
    lo = (r - hi.astype(F32)).astype(BF16)
    return _dot(jnp.concatenate([hi, lo], axis=1), et_ref[...])


def _mla_kernel(sc_ref, sx_ref, t1_ref, t2_ref, qag_ref, kvag_ref, wuq_ref, wukv_ref, e_ref, et_ref,
                gq_ref, gk_ref, ga_ref, gb_ref, dtb_ref, alane_ref, perm_ref, q_ref, k_ref, vt_ref, crow_ref, pg_ref):
    is_ctx = pl.program_id(1) == pl.num_programs(1) - 1
    s = jnp.where(is_ctx, sc_ref[0], sx_ref[0])
    t1 = t1_ref[...]
    t2 = t2_ref[...]
    lane = lax.broadcasted_iota(jnp.int32, (1, LANES), 1)
    n_heads = MLA_HEADS
    k_w = n_heads * HEAD_PAD

    ckv = s[:, Q_LORA:Q_LORA + KV_LORA]
    ckvn = ckv * lax.rsqrt(jnp.mean(ckv * ckv, axis=-1, keepdims=True) + EPS) * kvag_ref[...]
    kv = _dot(ckvn.astype(BF16), wukv_ref[...])
    kn = kv[:, :k_w]
    vt = kv[:, k_w:].T
    ones = jnp.ones((VT_ROWS - V_DIM, vt.shape[1]), F32)
    vt_ref[0, 0] = jnp.concatenate(
        [blk for h in range(n_heads) for blk in (vt[h * V_DIM:(h + 1) * V_DIM], ones)], axis=0).astype(BF16)
    kn = kn * _segment_rsqrt(kn, e_ref, et_ref) * gk_ref[...]
    g1 = s[:, 384:512]
    g2 = s[:, 512:640]
    rope_lanes = (lane >= QK_NOPE) & (lane < QK_NOPE + QK_ROPE)
    kr_ms = jnp.sum(jnp.where(rope_lanes, g1 * g1, 0.0), axis=-1, keepdims=True) * (1.0 / QK_ROPE)
    krot = lax.rsqrt(kr_ms + EPS) * (g1 * ga_ref[...] * t1 + g2 * gb_ref[...] * t2)
    k_ref[0] = (kn + jnp.concatenate([krot] * n_heads, axis=1)).astype(BF16)

    cq = s[:, :Q_LORA]
    cqn = cq * lax.rsqrt(jnp.mean(cq * cq, axis=-1, keepdims=True) + EPS) * qag_ref[...]
    q = _dot(cqn.astype(BF16), wuq_ref[...])
    tq = t1 + jnp.where(lane < QK_NOPE, 1.0, 0.0)
    q = q * _segment_rsqrt(q, e_ref, et_ref) * gq_ref[...] * jnp.concatenate([tq] * n_heads, axis=1)
    q_ref[0] = (q * (ATTN_SCALE * LOG2_E)).astype(BF16)

    xdt = s[:, 640:768] + dtb_ref[...]
    dt = jnp.maximum(xdt, 0.0) + jnp.log1p(jnp.exp(-jnp.abs(xdt)))
    dt = jnp.where(lane < 2 * SSM_HEADS, dt, 0.0)
    da = dt * (-LOG2_E * jnp.exp(alane_ref[...]))
    ii = lax.broadcasted_iota(jnp.int32, (CHUNK, CHUNK), 0)
    jj = lax.broadcasted_iota(jnp.int32, (CHUNK, CHUNK), 1)
    tri_f = (jj <= ii).astype(BF16)
    tri_b = (jj >= ii).astype(BF16)
    fwd_lane = (lane % (2 * HEADS_PER_GROUP)) < HEADS_PER_GROUP
    cums = []
    for ch in range(s.shape[0] // CHUNK):
        pieces = _split3(da[ch * CHUNK:(ch + 1) * CHUNK])
        cf = sum(_dot(tri_f, p) for p in pieces)
        cb = sum(_dot(tri_b, p) for p in pieces)
        cums.append(jnp.where(fwd_lane, cf, cb))
        crow_ref[0, ch] = cums[-1].T[:2 * SSM_HEADS]
    cum = jnp.concatenate(cums, axis=0)
    d3 = _split3(dt)
    c3 = _split3(cum)
    dc3 = jnp.concatenate([d3[0], c3[0], d3[1], c3[1], d3[2], c3[2]], axis=1)
    pg_ref[0] = _dot(dc3, perm_ref[...]).astype(BF16)


def _piece_permutation():
    per = 2 * HEADS_PER_GROUP
    perm = np.zeros((6 * LANES, SSM_GROUPS * LANES), np.float32)
    for piece in range(3):
        for kind in range(2):
            for g in range(SSM_GROUPS):
                for j in range(per):
                    perm[(piece * 2 + kind) * LANES + g * per + j, g * LANES + piece * 2 * per + kind * per + j] = 1.0
    return jnp.asarray(perm, BF16)


def _mla_call(narrow_c, narrow_x, t1, t2, consts, name):
    b, n_ctx, _ = narrow_c.shape
    n_lat = narrow_x.shape[1]
    tm = n_ctx
    s_all = n_ctx + n_lat
    last_latent = n_lat // tm - 1
    k_w = MLA_HEADS * HEAD_PAD
    full = lambda a: pl.BlockSpec(a.shape, lambda bi, i: (0,) * a.ndim)
    latent_tile = lambda bi, i: (bi, jnp.minimum(i, last_latent), 0)
    return pl.pallas_call(
        _mla_kernel,
        out_shape=[jax.ShapeDtypeStruct((b, s_all, k_w), BF16),
                   jax.ShapeDtypeStruct((b, s_all, k_w), BF16),
                   jax.ShapeDtypeStruct((b, s_all // tm, MLA_HEADS * VT_ROWS, tm), BF16),
                   jax.ShapeDtypeStruct((b, s_all // CHUNK, 2 * SSM_HEADS, CHUNK), F32),
                   jax.ShapeDtypeStruct((b, s_all, SSM_GROUPS * LANES), BF16)],
        grid=(b, s_all // tm),
        in_specs=[pl.BlockSpec((1, tm, NARROW_W), lambda bi, i: (bi, 0, 0)),
                  pl.BlockSpec((1, tm, NARROW_W), latent_tile),
                  pl.BlockSpec((tm, LANES), lambda bi, i: (i, 0)),
                  pl.BlockSpec((tm, LANES), lambda bi, i: (i, 0)),
                  *[full(a) for a in consts]],
        out_specs=[pl.BlockSpec((1, tm, k_w), lambda bi, i: (bi, i, 0)),
                   pl.BlockSpec((1, tm, k_w), lambda bi, i: (bi, i, 0)),
                   pl.BlockSpec((1, 1, MLA_HEADS * VT_ROWS, tm), lambda bi, i: (bi, i, 0, 0)),
                   pl.BlockSpec((1, tm // CHUNK, 2 * SSM_HEADS, CHUNK), lambda bi, i: (bi, i, 0, 0)),
                   pl.BlockSpec((1, tm, SSM_GROUPS * LANES), lambda bi, i: (bi, i, 0))],
        compiler_params=_params("parallel", "arbitrary"),
        name=name,
    )(narrow_c, narrow_x, t1, t2, *consts)


EXP_W = 4 * GROUP_W


def _ssd_kernel(nc, xs_ref, b_ref, c_ref, p_ref, crow_ref, sel_ref, dsk_ref, h0_ref,
                y_ref, hout_ref, yacc_ref, s_ref, ex_ref, cb_ref, dec_ref, h_ref):
    q = CHUNK
    r_heads = HEADS_PER_GROUP
    wide_l = r_heads * q
    ii_w = lax.broadcasted_iota(jnp.int32, (q, wide_l), 0)
    jj_w = lax.broadcasted_iota(jnp.int32, (q, wide_l), 1) % q
    keep_f = jj_w <= ii_w
    keep_b = jj_w >= ii_w
    lane_head = lax.broadcasted_iota(jnp.int32, (1, GROUP_W), 1) // SSM_HEAD_DIM
    low_half = lax.broadcasted_iota(jnp.int32, (1, LANES), 1) < SSM_HEAD_DIM
    dsk = dsk_ref[0:1, :] + dsk_ref[1:2, :]
    gw = GROUP_W

    def per_head_128(c64):
        out = []
        for pair in range(r_heads // 2):
            v = c64[:, pair * LANES:(pair + 1) * LANES]
            sw = pltpu.roll(v, SSM_HEAD_DIM, axis=1)
            out += [jnp.where(low_half, v, sw), jnp.where(low_half, sw, v)]
        return jnp.concatenate(out, axis=1)

    def block_diag(xdt):
        xb = xdt.astype(BF16)
        return jnp.concatenate([jnp.where(lane_head == r, xb, jnp.zeros_like(xb)) for r in range(r_heads)], axis=0)

    def expand(c, carry):
        rows = pl.ds(pl.multiple_of(c * q, q), q)
        ex_ref[rows, :] = _dot(p_ref[0, rows, :], sel_ref[...])
        cb_ref[c] = lax.dot_general(c_ref[0, rows, :], b_ref[0, rows, :], (((1,), (1,)), ((), ())),
                                    preferred_element_type=F32)
        return carry

    lax.fori_loop(0, nc, expand, 0, unroll=4 if nc % 4 == 0 else 2)

    def local(c, carry):
        rows = pl.ds(pl.multiple_of(c * q, q), q)
        x = xs_ref[0, rows, :].astype(F32)
        bm = b_ref[0, rows, :]
        dte_f, dte_b = ex_ref[rows, 0:gw], ex_ref[rows, gw:2 * gw]
        c64_f, c64_b = ex_ref[rows, 2 * gw:3 * gw], ex_ref[rows, 3 * gw:4 * gw]
        cw_f, cw_b = per_head_128(c64_f), per_head_128(c64_b)
        crow = crow_ref[0, c]
        flat_f = jnp.concatenate([crow[r:r + 1, :] for r in range(r_heads)], axis=1)
        flat_b = jnp.concatenate([crow[r_heads + r:r_heads + r + 1, :] for r in range(r_heads)], axis=1)
        cb4 = jnp.concatenate([cb_ref[c]] * r_heads, axis=1)
        l_f = (jnp.where(keep_f, jnp.exp2(cw_f - flat_f), 0.0) * cb4).astype(BF16)
        l_b = (jnp.where(keep_b, jnp.exp2(cw_b - flat_b), 0.0) * cb4).astype(BF16)
        xdt_f = x * dte_f
        xdt_b = x * dte_b
        y = _dot(jnp.concatenate([l_f, l_b], axis=1),
                 jnp.concatenate([block_diag(xdt_f), block_diag(xdt_b)], axis=0))
        yacc_ref[rows, :] = y + dsk * x
        edge_f = c64_f[q - 1:q, :]
        edge_b = c64_b[0:1, :]
        xw = jnp.concatenate([(xdt_f * jnp.exp2(edge_f - c64_f)).astype(BF16),
                              (xdt_b * jnp.exp2(edge_b - c64_b)).astype(BF16)], axis=1)
        st = lax.dot_general(bm, xw, (((0,), (0,)), ((), ())), preferred_element_type=F32)
        s_ref[0, c] = st[:, :gw]
        s_ref[1, c] = st[:, gw:]
        dec_ref[0, c] = jnp.broadcast_to(jnp.exp2(edge_f), (8, gw))
        dec_ref[1, c] = jnp.broadcast_to(jnp.exp2(edge_b), (8, gw))
        return carry

    lax.fori_loop(0, nc, local, 0, unroll=4 if nc % 4 == 0 else 2)

    h_ref[0] = h0_ref[0, 0, 0]
    h_ref[1] = h0_ref[0, 0, 1]

    def carry_states(t, carry):
        for d, c in ((0, t), (1, nc - 1 - t)):
            rows = pl.ds(pl.multiple_of(c * q, q), q)
            h = h_ref[d]
            yo = _dot(c_ref[0, rows, :], h.astype(BF16)) * jnp.exp2(ex_ref[rows, (2 + d) * gw:(3 + d) * gw])
            yacc_ref[rows, :] = yacc_ref[rows, :] + yo
            h_ref[d] = h * dec_ref[d, c, 0:1, :] + s_ref[d, c]
        return carry

    lax.fori_loop(0, nc, carry_states, 0, unroll=2)
    hout_ref[0, 0, 0] = h_ref[0]
    hout_ref[0, 0, 1] = h_ref[1]

    def emit(c, carry):
        rows = pl.ds(pl.multiple_of(c * q, q), q)
        y_ref[0, rows, :] = yacc_ref[rows, :].astype(y_ref.dtype)
        return carry

    lax.fori_loop(0, nc, emit, 0)


def _ssd_call(wide, pieces, crow, seq_block, sel, dsk, h0, name):
    xbc = wide
    b, n, _ = wide.shape
    nc = n // CHUNK
    g = SSM_GROUPS
    x_col0 = D_INNER // GROUP_W
    b_col0 = 2 * D_INNER // D_STATE
    c_col0 = b_col0 + g
    return pl.pallas_call(
        functools.partial(_ssd_kernel, nc),
        out_shape=[jax.ShapeDtypeStruct((b, n, D_INNER), BF16),
                   jax.ShapeDtypeStruct((b, g, 2, D_STATE, GROUP_W), F32)],
        grid=(b, g),
        in_specs=[pl.BlockSpec((1, n, GROUP_W), lambda bi, gi: (bi, 0, x_col0 + gi)),
                  pl.BlockSpec((1, n, D_STATE), lambda bi, gi: (bi, 0, b_col0 + gi)),
                  pl.BlockSpec((1, n, D_STATE), lambda bi, gi: (bi, 0, c_col0 + gi)),
                  pl.BlockSpec((1, n, LANES), lambda bi, gi: (bi, seq_block, gi)),
                  pl.BlockSpec((1, nc, 2 * HEADS_PER_GROUP, CHUNK), lambda bi, gi: (bi, seq_block, gi, 0)),
                  pl.BlockSpec((LANES, EXP_W), lambda bi, gi: (0, 0)),
                  pl.BlockSpec((2, GROUP_W), lambda bi, gi: (0, gi)),
                  pl.BlockSpec((1, 1, 2, D_STATE, GROUP_W), lambda bi, gi: (bi, gi, 0, 0, 0))],
        out_specs=[pl.BlockSpec((1, n, GROUP_W), lambda bi, gi: (bi, 0, gi)),
                   pl.BlockSpec((1, 1, 2, D_STATE, GROUP_W), lambda bi, gi: (bi, gi, 0, 0, 0))],
        scratch_shapes=[pltpu.VMEM((n, GROUP_W), F32),
                        pltpu.VMEM((2, nc, D_STATE, GROUP_W), F32),
                        pltpu.VMEM((n, EXP_W), F32),
                        pltpu.VMEM((nc, CHUNK, CHUNK), F32),
                        pltpu.VMEM((2, nc, 8, GROUP_W), F32),
                        pltpu.VMEM((2, D_STATE, GROUP_W), F32)],
        compiler_params=_params("parallel", "parallel"),
        name=name,
    )(xbc, xbc, xbc, pieces, crow, sel, dsk, h0)


VT_ROWS = V_DIM + 16

KEY_CHUNK = 512
ATTN_HEADS = 2


def _attn_kernel(q_ref, k_ref, vt_ref, o_ref, s_bufs, cmaxs, acc, m_run):
    nt = (((1,), (1,)), ((), ()))
    n_keys = k_ref.shape[1]
    blk = vt_ref.shape[3]
    per = KEY_CHUNK // blk
    n_full = n_keys // KEY_CHUNK
    tail = n_keys - n_full * KEY_CHUNK

    def produce(slot, c, n_rows=KEY_CHUNK):
        start = c * KEY_CHUNK
        rows = pl.ds(start if isinstance(c, int) else pl.multiple_of(start, KEY_CHUNK), n_rows)
        for h in range(ATTN_HEADS):
            st = lax.dot_general(k_ref[0, rows, h * HEAD_PAD:(h + 1) * HEAD_PAD],
                                 q_ref[0, :, h * HEAD_PAD:(h + 1) * HEAD_PAD], nt, preferred_element_type=F32)
            s_bufs[slot][h, 0:n_rows, :] = st
            cmaxs[slot][h, 0:1, :] = jnp.max(st, axis=0, keepdims=True)

    def consume(slot, c, n_rows=KEY_CHUNK):
        for h in range(ATTN_HEADS):
            m_old = m_run[h, 0:1, :]
            m_new = jnp.maximum(m_old, cmaxs[slot][h, 0:1, :])
            p = jnp.exp2(s_bufs[slot][h, 0:n_rows, :] - m_new).astype(BF16)
            vt = jnp.concatenate([vt_ref[0, per * c + j, h * VT_ROWS:(h + 1) * VT_ROWS, :]
                                  for j in range(n_rows // blk)], axis=1)
            acc[h] = acc[h] * jnp.exp2(m_old - m_new) + _dot(vt, p)
            m_run[h, 0:1, :] = m_new

    m_run[...] = jnp.full(m_run.shape, -jnp.inf, F32)
    acc[...] = jnp.zeros(acc.shape, F32)
    assert n_full % 2 == 0 and 0 < tail < KEY_CHUNK
    produce(0, 0)

    def body(j, carry):
        produce(1, 2 * j + 1)
        consume(0, 2 * j)
        produce(0, 2 * j + 2)
        consume(1, 2 * j + 1)
        return carry

    lax.fori_loop(0, n_full // 2 - 1, body, 0)
    produce(1, n_full - 1)
    consume(0, n_full - 2)
    produce(0, n_full, tail)
    consume(1, n_full - 1)
    consume(0, n_full, tail)
    outs = [acc[h, 0:V_DIM, :] / acc[h, V_DIM:V_DIM + 1, :] for h in range(ATTN_HEADS)]
    o_ref[0] = jnp.concatenate(outs, axis=0).T.astype(o_ref.dtype)


def _attn_call(q, k, vt, n, tq, name):
    b = q.shape[0]
    s = k.shape[1]
    n_blk, _, blk = vt.shape[1:]
    nh = ATTN_HEADS
    return pl.pallas_call(
        _attn_kernel,
        out_shape=jax.ShapeDtypeStruct((b, n, MLA_HEADS * V_DIM), BF16),
        grid=(b, MLA_HEADS // nh, n // tq),
        in_specs=[pl.BlockSpec((1, tq, nh * HEAD_PAD), lambda bi, hp, i: (bi, i, hp)),
                  pl.BlockSpec((1, s, nh * HEAD_PAD), lambda bi, hp, i: (bi, 0, hp)),
                  pl.BlockSpec((1, n_blk, nh * VT_ROWS, blk), lambda bi, hp, i: (bi, 0, hp, 0))],
        out_specs=pl.BlockSpec((1, tq, nh * V_DIM), lambda bi, hp, i: (bi, i, hp)),
        scratch_shapes=[[pltpu.VMEM((nh, KEY_CHUNK, tq), F32)] * 2,
                        [pltpu.VMEM((nh, 8, tq), F32)] * 2,
                        pltpu.VMEM((nh, VT_ROWS, tq), F32),
                        pltpu.VMEM((nh, 8, tq), F32)],
        compiler_params=_params("parallel", "parallel", "parallel"),
        name=name,
    )(q, k, vt)


def _tail_kernel(ff_chunk, att_ref, y_ref, z_ref, ga_ref, gb_ref, x_ref, mod_ref, sg_ref, g2_ref,
                 wa_ref, wb_ref, wo_ref, w1_ref, w2_ref, o_ref):
    y = y_ref[0].astype(F32)
    z = z_ref[0].astype(F32)
    u = y * (z * _sigmoid(z))
    gw = D_INNER // SSM_GROUPS
    parts = []
    for g in range(SSM_GROUPS):
        ug = u[:, g * gw:(g + 1) * gw]
        parts.append(ug * lax.rsqrt(jnp.mean(ug * ug, axis=-1, keepdims=True) + EPS))
    un = (jnp.concatenate(parts, axis=1) * sg_ref[...]).astype(BF16)
    a = _dot(att_ref[0], wa_ref[...])
    bb = _dot(un, wb_ref[...])
    merged = _sigmoid(ga_ref[0].astype(F32)) * a + _sigmoid(gb_ref[0].astype(F32)) * bb
    o = _dot(merged.astype(BF16), wo_ref[...])
    x = x_ref[0] + mod_ref[0, 2:3, :] * o

    ms = jnp.mean(x * x, axis=-1, keepdims=True)
    xn = x * lax.rsqrt(ms + EPS) * g2_ref[...]
    h = (xn * (1.0 + mod_ref[0, 4:5, :]) + mod_ref[0, 3:4, :]).astype(BF16)
    acc = None
    for c in range(D_FF // ff_chunk):
        a = _dot(h, w1_ref[:, c * ff_chunk:(c + 1) * ff_chunk])
        a = jnp.maximum(a, 0.0)
        part = _dot((a * a).astype(BF16), w2_ref[c * ff_chunk:(c + 1) * ff_chunk, :])
        acc = part if acc is None else acc + part
    o_ref[0] = x + mod_ref[0, 5:6, :] * acc


def _tail_call(att, y, wide, x, mod, sg, g2, wa, wb, wo, w1, w2, tm, name):
    b, n, d = x.shape
    ga_blk = (D_INNER + CONV_DIM) // d
    full = lambda a: pl.BlockSpec(a.shape, lambda bi, i: (0,) * a.ndim)
    resident = lambda a: pl.BlockSpec(a.shape, lambda bi, i: (0,) * a.ndim, pipeline_mode=pl.Buffered(1))
    return pl.pallas_call(
        functools.partial(_tail_kernel, 1024),
        out_shape=jax.ShapeDtypeStruct((b, n, d), F32),
        grid=(b, n // tm),
        in_specs=[pl.BlockSpec((1, tm, MLA_HEADS * V_DIM), lambda bi, i: (bi, i, 0)),
                  pl.BlockSpec((1, tm, D_INNER), lambda bi, i: (bi, i, 0)),
                  pl.BlockSpec((1, tm, D_INNER), lambda bi, i: (bi, i, 0)),
                  pl.BlockSpec((1, tm, d), lambda bi, i: (bi, i, ga_blk)),
                  pl.BlockSpec((1, tm, d), lambda bi, i: (bi, i, ga_blk + 1)),
                  pl.BlockSpec((1, tm, d), lambda bi, i: (bi, i, 0)),
                  pl.BlockSpec((1, 8, d), lambda bi, i: (bi, 0, 0)),
                  full(sg), full(g2), resident(wa), resident(wb), resident(wo), resident(w1), resident(w2)],
        out_specs=pl.BlockSpec((1, tm, d), lambda bi, i: (bi, i, 0)),
        compiler_params=_params("parallel", "parallel"),
        name=name,
    )(att, y, wide, wide, wide, x, mod, sg, g2, wa, wb, wo, w1, w2)


_ROPE_SWAP = np.concatenate([np.arange(8, 16), np.arange(0, 8), np.arange(24, 32), np.arange(16, 24)])
_DT_LANE_ORDER = np.array([d * SSM_HEADS + g * HEADS_PER_GROUP + r for g in range(SSM_GROUPS) for d in range(2)
                           for r in range(HEADS_PER_GROUP)])


def _rope_tables(n_lat):
    rows = n_lat // GRID_W
    row = jnp.repeat(jnp.arange(rows), GRID_W)
    col = jnp.tile(jnp.arange(GRID_W), rows)
    freqs = ROPE_BASE ** (-jnp.arange(ROPE_PAIRS, dtype=F32) / ROPE_PAIRS)
    ang = jnp.stack([row, col], axis=-1).astype(F32)[..., None] * freqs
    cos, sin = jnp.cos(ang), jnp.sin(ang)
    cos32 = jnp.stack([cos, cos], axis=2).reshape(n_lat, QK_ROPE)
    sin32 = jnp.stack([-sin, sin], axis=2).reshape(n_lat, QK_ROPE)
    zeros = jnp.zeros((n_lat, QK_NOPE), F32)
    return (jnp.concatenate([zeros, cos32, sin32], axis=1),
            jnp.concatenate([zeros, sin32, cos32], axis=1))


def _identity_rope_tables(n):
    zeros = jnp.zeros((n, QK_NOPE), F32)
    one = jnp.ones((n, QK_ROPE), F32)
    zero = jnp.zeros((n, QK_ROPE), F32)
    return jnp.concatenate([zeros, one, zero], axis=1), jnp.concatenate([zeros, zero, one], axis=1)


def _layout_params(w_in, w_uq, w_ukv, q_a_g, kv_a_g, qk_q_g, qk_k_g, dt_bias, a_log):
    d = w_in.shape[0]
    o_cq, o_ckv, o_kr = 0, Q_LORA, Q_LORA + KV_LORA
    o_z = o_kr + QK_ROPE
    o_xbc = o_z + D_INNER
    o_dt = o_xbc + CONV_DIM
    o_ga = o_dt + 2 * SSM_HEADS
    kr = w_in[:, o_kr:o_kr + QK_ROPE]
    kr_sw = kr[:, _ROPE_SWAP]
    z64 = jnp.zeros((d, 64), F32)
    w_narrow = jnp.concatenate(
        [w_in[:, o_cq:o_kr], z64, kr, kr_sw, z64, kr_sw, kr, w_in[:, o_dt:o_ga][:, _DT_LANE_ORDER], z64],
        axis=1).astype(BF16)
    w_wide = jnp.concatenate([w_in[:, o_z:o_dt], w_in[:, o_ga:]], axis=1).astype(BF16)

    hq = w_uq.reshape(Q_LORA, MLA_HEADS, QK_NOPE + QK_ROPE)
    rope_q = hq[:, :, QK_NOPE:]
    wuq = jnp.concatenate([hq[:, :, :QK_NOPE], rope_q, rope_q[:, :, _ROPE_SWAP]], axis=2)
    wuq = wuq.reshape(Q_LORA, MLA_HEADS * HEAD_PAD).astype(BF16)
    hkv = w_ukv.reshape(KV_LORA, MLA_HEADS, QK_NOPE + V_DIM)
    wk = jnp.concatenate([hkv[:, :, :QK_NOPE], jnp.zeros((KV_LORA, MLA_HEADS, HEAD_PAD - QK_NOPE), F32)], axis=2)
    wukv = jnp.concatenate([wk.reshape(KV_LORA, MLA_HEADS * HEAD_PAD),
                            hkv[:, :, QK_NOPE:].reshape(KV_LORA, MLA_HEADS * V_DIM)], axis=1).astype(BF16)

    gq_r = qk_q_g[QK_NOPE:]
    gq = jnp.tile(jnp.concatenate([qk_q_g[:QK_NOPE], gq_r, gq_r[_ROPE_SWAP]]), MLA_HEADS)[None, :]
    gk = jnp.tile(jnp.concatenate([qk_k_g[:QK_NOPE], jnp.zeros((HEAD_PAD - QK_NOPE,), F32)]), MLA_HEADS)[None, :]
    gk_r = qk_k_g[QK_NOPE:]
    z64v = jnp.zeros((QK_NOPE,), F32)
    ga = jnp.concatenate([z64v, gk_r, gk_r[_ROPE_SWAP]])[None, :]
    gb = jnp.concatenate([z64v, gk_r[_ROPE_SWAP], gk_r])[None, :]
    lane_pad = jnp.zeros((LANES - 2 * SSM_HEADS,), F32)
    dtb = jnp.concatenate([dt_bias.reshape(-1)[_DT_LANE_ORDER], lane_pad])[None, :]
    alane = jnp.concatenate([a_log.astype(F32).reshape(-1)[_DT_LANE_ORDER], lane_pad])[None, :]

    lane = np.arange(MLA_HEADS * HEAD_PAD)
    head, off = lane // HEAD_PAD, lane % HEAD_PAD
    e = np.zeros((MLA_HEADS * HEAD_PAD, LANES), np.float32)
    e[lane[off < QK_NOPE], 2 * head[off < QK_NOPE]] = 1.0 / QK_NOPE
    rope = (off >= QK_NOPE) & (off < QK_NOPE + QK_ROPE)
    e[lane[rope], 2 * head[rope] + 1] = 1.0 / QK_ROPE
    et = np.zeros((LANES, MLA_HEADS * HEAD_PAD), np.float32)
    et[2 * head[off < QK_NOPE], lane[off < QK_NOPE]] = 1.0
    et[2 * head[off >= QK_NOPE] + 1, lane[off >= QK_NOPE]] = 1.0
    et = np.concatenate([et, et], axis=0)
    consts = (q_a_g[None, :], kv_a_g[None, :], wuq, wukv, jnp.asarray(e, BF16), jnp.asarray(et, BF16),
              gq, gk, ga, gb, dtb, alane, _piece_permutation())
    return w_narrow, w_wide, consts


def _expansion_selector():
    r = HEADS_PER_GROUP
    sel = np.zeros((LANES, EXP_W), np.float32)
    for piece in range(3):
        for d in range(2):
            for h in range(r):
                row_dt = piece * 4 * r + d * r + h
                row_cum = piece * 4 * r + 2 * r + d * r + h
                sel[row_dt, d * GROUP_W + h * SSM_HEAD_DIM:d * GROUP_W + (h + 1) * SSM_HEAD_DIM] = 1.0
                base = 2 * GROUP_W + d * GROUP_W + h * SSM_HEAD_DIM
                sel[row_cum, base:base + SSM_HEAD_DIM] = 1.0
    return jnp.asarray(sel, BF16)


def kernel(x, c, ctx, c_ctx, norm1_g, norm2_g, w_mod, b_mod, w_in, q_a_g, w_uq, kv_a_g, w_ukv, qk_q_g, qk_k_g,
           conv_w, conv_b, dt_bias, a_log, d_skip, ssm_norm_g, w_proj_a, w_proj_b, w_out, w_mlp1, w_mlp2):
    assert w_mod.shape[0] == 1, "single-layer block"
    b, n_lat, d = x.shape
    n_ctx = ctx.shape[1]

    cvec = jnp.concatenate([c, c_ctx[None, :], jnp.zeros((8 - b - 1, d), F32)], axis=0)
    mod = _mod_call(cvec, w_mod[0], b_mod[0][None, :]).reshape(8, N_MOD, d)
    mod = jnp.concatenate([mod, jnp.zeros((8, 8 - N_MOD, d), F32)], axis=1)
    mod_x, mod_c = mod[:b], mod[b:b + 1]

    w_narrow, w_wide, consts = _layout_params(w_in[0], w_uq[0], w_ukv[0], q_a_g[0], kv_a_g[0], qk_q_g[0],
                                              qk_k_g[0], dt_bias[0], a_log[0])
    g1 = norm1_g[0][None, :]
    conv_wl, conv_bl = conv_w[0], conv_b[0][None, :]
    dsk = jnp.repeat(d_skip[0], SSM_HEAD_DIM, axis=1)
    sel = _expansion_selector()

    narrow_c, wide_c = _front_call(ctx, g1, mod_c, w_narrow, w_wide, conv_wl, conv_bl, n_ctx, True, "front_ctx")
    narrow_x, wide_x = _front_call(x, g1, mod_x, w_narrow, w_wide, conv_wl, conv_bl, 256, False, "front")

    t1c, t2c = _identity_rope_tables(n_ctx)
    t1x, t2x = _rope_tables(n_lat)
    q_all, k_all, vt_all, crow, pieces = _mla_call(narrow_c, narrow_x, jnp.concatenate([t1x, t1c], axis=0),
                                                   jnp.concatenate([t2x, t2c], axis=0), consts, "mla")

    h_zero = jnp.zeros((b, SSM_GROUPS, 2, D_STATE, GROUP_W), F32)
    _, h_ctx = _ssd_call(wide_c, pieces, crow, n_lat // n_ctx, sel, dsk, h_zero, "ssd_ctx")
    y_x, _ = _ssd_call(wide_x, pieces, crow, 0, sel, dsk, h_ctx, "ssd")

    att = _attn_call(q_all, k_all, vt_all, n_lat, 1024, "attn")

    return _tail_call(att, y_x, wide_x, x, mod_x, ssm_norm_g[0][None, :], norm2_g[0][None, :],
                      w_proj_a[0].astype(BF16), w_proj_b[0].astype(BF16), w_out[0].astype(BF16),
                      w_mlp1[0].astype(BF16), w_mlp2[0].astype(BF16), 256, "tail")
```

```python
import functools

import numpy as np
import jax
import jax.numpy as jnp
from jax import lax
from jax.experimental import pallas as pl
from jax.experimental.pallas import tpu as pltpu

F32 = jnp.float32
BF16 = jnp.bfloat16

D_MODEL = 1024
GRID_W = 64
MLA_HEADS = 16
QK_NOPE = 64
QK_ROPE = 32
V_DIM = 64
Q_LORA = 256
KV_LORA = 128
ROPE_PAIRS = QK_ROPE // 4
ROPE_BASE = 10000.0
ATTN_SCALE = (QK_NOPE + QK_ROPE) ** -0.5
LOG2_E = 1.4426950408889634
D_INNER = 2 * D_MODEL
SSM_HEAD_DIM = 64
SSM_HEADS = D_INNER // SSM_HEAD_DIM
SSM_GROUPS = 8
HEADS_PER_GROUP = SSM_HEADS // SSM_GROUPS
D_STATE = 128
D_CONV = 5
CHUNK = 128
CONV_DIM = D_INNER + 2 * SSM_GROUPS * D_STATE
D_FF = 4 * D_MODEL
N_MOD = 6
EPS = 1e-6

LANES = 128
HEAD_PAD = 128
GROUP_W = HEADS_PER_GROUP * SSM_HEAD_DIM
NARROW_ROPE_A = Q_LORA + KV_LORA
NARROW_ROPE_B = NARROW_ROPE_A + LANES
NARROW_DT = NARROW_ROPE_B + LANES
NARROW_W = NARROW_DT + LANES
WIDE_W = D_INNER + CONV_DIM + 2 * D_MODEL
VMEM_LIMIT = 56 * 1024 * 1024
FRONT_TM = 256
TAIL_TM = 512
ATTN_TQ = 1024
MOD_TN = 1024
FF_CHUNK = 1024


def _sigmoid(x):
    return 1.0 / (1.0 + jnp.exp(-x))


def _split3(x):
    hi = x.astype(BF16)
    r1 = x - hi.astype(F32)
    mid = r1.astype(BF16)
    lo = (r1 - mid.astype(F32)).astype(BF16)
    return hi, mid, lo


def _dot(a, b):
    return jnp.dot(a, b, preferred_element_type=F32)


def _params(*sem):
    return pltpu.CompilerParams(dimension_semantics=sem, vmem_limit_bytes=VMEM_LIMIT)


def _mod_kernel(c_ref, w_ref, b_ref, o_ref):
    c = c_ref[...]
    a = c * _sigmoid(c)
    w = w_ref[...]
    a_hi = a.astype(BF16)
    a_lo = (a - a_hi.astype(F32)).astype(BF16)
    w_hi = w.astype(BF16)
    w_lo = (w - w_hi.astype(F32)).astype(BF16)
    o_ref[...] = _dot(a_hi, w_hi) + _dot(a_hi, w_lo) + _dot(a_lo, w_hi) + b_ref[...]


def _mod_call(cvec, w_mod, b_mod):
    rows, d = cvec.shape
    n_out = w_mod.shape[1]
    tn = MOD_TN
    return pl.pallas_call(
        _mod_kernel,
        out_shape=jax.ShapeDtypeStruct((rows, n_out), F32),
        grid=(n_out // tn,),
        in_specs=[pl.BlockSpec((rows, d), lambda j: (0, 0)),
                  pl.BlockSpec((d, tn), lambda j: (0, j)),
                  pl.BlockSpec((1, tn), lambda j: (0, j))],
        out_specs=pl.BlockSpec((rows, tn), lambda j: (0, j)),
        compiler_params=_params("parallel"),
        name="mod",
    )(cvec, w_mod, b_mod)


FRONT_HALO = 16
FRONT_COLS = 1024


def _front_kernel(x_ref, xp_ref, xn_ref, g_ref, mod_ref, wn_ref, ww_ref, cw_ref, cb_ref, narrow_ref, wide_ref, pre_ref):
    i = pl.program_id(1)
    last = pl.num_programs(1) - 1
    tm = x_ref.shape[1]
    g = g_ref[...]
    shift = mod_ref[0, 0:1, :]
    scale = mod_ref[0, 1:2, :]

    def normmod(xv):
        ms = jnp.mean(xv * xv, axis=-1, keepdims=True)
        return xv * lax.rsqrt(ms + EPS) * g * (1.0 + scale) + shift

    hc = normmod(x_ref[0]).astype(BF16)
    hp = (normmod(xp_ref[0]) * (i > 0).astype(F32)).astype(BF16)
    hn = (normmod(xn_ref[0]) * (i < last).astype(F32)).astype(BF16)
    h_ext = jnp.concatenate([hp, hc, hn], axis=0)
    narrow_ref[0] = _dot(hc, wn_ref[...])
    pad = (D_CONV - 1) // 2
    n_chunks = WIDE_W // FRONT_COLS
    is_conv = [D_INNER <= j * FRONT_COLS < D_INNER + CONV_DIM for j in range(n_chunks)]
    conv_js = [j for j in range(n_chunks) if is_conv[j]]
    plain_js = [j for j in range(n_chunks) if not is_conv[j]]
    order = [j for pair in zip(conv_js, plain_js) for j in pair] + conv_js[len(plain_js):] + plain_js[len(conv_js):]
    for j in order:
        lo = j * FRONT_COLS
        cols = slice(lo, lo + FRONT_COLS)
        if D_INNER <= lo < D_INNER + CONV_DIM:
            slot = j % 2
            pre_ref[slot] = _dot(h_ext, ww_ref[:, cols])
            cc = slice(lo - D_INNER, lo - D_INNER + FRONT_COLS)
            acc = cb_ref[:, cc] + cw_ref[0:1, cc] * pre_ref[slot, pl.ds(FRONT_HALO - pad, tm), :]
            for k in range(1, D_CONV):
                acc = acc + cw_ref[k:k + 1, cc] * pre_ref[slot, pl.ds(FRONT_HALO - pad + k, tm), :]
            wide_ref[0, :, cols] = (acc * _sigmoid(acc)).astype(BF16)
        else:
            wide_ref[0, :, cols] = _dot(hc, ww_ref[:, cols]).astype(BF16)


def _front_call(x, g, mod, w_narrow, w_wide, conv_w, conv_b, tm, shared_mod, name):
    b, n, d = x.shape
    per = tm // FRONT_HALO
    n_halo = n // FRONT_HALO
    mod_map = (lambda bi, i: (0, 0, 0)) if shared_mod else (lambda bi, i: (bi, 0, 0))
    resident = lambda a: pl.BlockSpec(a.shape, lambda bi, i: (0,) * a.ndim, pipeline_mode=pl.Buffered(1))
    return pl.pallas_call(
        _front_kernel,
        out_shape=[jax.ShapeDtypeStruct((b, n, NARROW_W), F32), jax.ShapeDtypeStruct((b, n, WIDE_W), BF16)],
        grid=(b, n // tm),
        in_specs=[pl.BlockSpec((1, tm, d), lambda bi, i: (bi, i, 0)),
                  pl.BlockSpec((1, FRONT_HALO, d), lambda bi, i: (bi, jnp.maximum(i * per - 1, 0), 0)),
                  pl.BlockSpec((1, FRONT_HALO, d), lambda bi, i: (bi, jnp.minimum((i + 1) * per, n_halo - 1), 0)),
                  pl.BlockSpec((1, d), lambda bi, i: (0, 0)),
                  pl.BlockSpec((1, 8, d), mod_map),
                  resident(w_narrow), resident(w_wide), resident(conv_w), resident(conv_b)],
        out_specs=[pl.BlockSpec((1, tm, NARROW_W), lambda bi, i: (bi, i, 0)),
                   pl.BlockSpec((1, tm, WIDE_W), lambda bi, i: (bi, i, 0))],
        scratch_shapes=[pltpu.VMEM((2, tm + 2 * FRONT_HALO, FRONT_COLS), F32)],
        compiler_params=_params("parallel", "parallel"),
        name=name,
    )(x, x, x, g, mod, w_narrow, w_wide, conv_w, conv_b)


def _segment_rsqrt(t, e_ref, et_ref):
    ms = _dot((t * t).astype(BF16), e_ref[...])
    r = lax.rsqrt(ms + EPS)
    hi = r.astype(BF16)
    lo = (r - hi.astype(F32)).astype(BF16)
    return _dot(jnp.concatenate([hi, lo], axis=1), et_ref[...])


def _mla_kernel(sc_ref, sx_ref, t1_ref, t2_ref, qag_ref, kvag_ref, wuq_ref, wukv_ref, e_ref, et_ref,
                gq_ref, gk_ref, ga_ref, gb_ref, dtb_ref, alane_ref, perm_ref, q_ref, k_ref, vt_ref, crow_ref, pg_ref):
    is_ctx = pl.program_id(1) == pl.num_programs(1) - 1
    s = jnp.where(is_ctx, sc_ref[0], sx_ref[0])
    t1 = t1_ref[...]
    t2 = t2_ref[...]
    lane = lax.broadcasted_iota(jnp.int32, (1, LANES), 1)
    n_heads = MLA_HEADS
    k_w = n_heads * HEAD_PAD

    ckv = s[:, Q_LORA:Q_LORA + KV_LORA]
    ckvn = ckv * lax.rsqrt(jnp.mean(ckv * ckv, axis=-1, keepdims=True) + EPS) * kvag_ref[...]
    kv = _dot(ckvn.astype(BF16), wukv_ref[...])
    kn = kv[:, :k_w]
    vt = kv[:, k_w:].T
    ones = jnp.ones((VT_ROWS - V_DIM, vt.shape[1]), F32)
    vt_ref[0, 0] = jnp.concatenate(
        [blk for h in range(n_heads) for blk in (vt[h * V_DIM:(h + 1) * V_DIM], ones)], axis=0).astype(BF16)
    kn = kn * _segment_rsqrt(kn, e_ref, et_ref) * gk_ref[...]
    g1 = s[:, NARROW_ROPE_A:NARROW_ROPE_B]
    g2 = s[:, NARROW_ROPE_B:NARROW_DT]
    rope_lanes = (lane >= QK_NOPE) & (lane < QK_NOPE + QK_ROPE)
    kr_ms = jnp.sum(jnp.where(rope_lanes, g1 * g1, 0.0), axis=-1, keepdims=True) * (1.0 / QK_ROPE)
    krot = lax.rsqrt(kr_ms + EPS) * (g1 * ga_ref[...] * t1 + g2 * gb_ref[...] * t2)
    k_ref[0] = (kn + jnp.concatenate([krot] * n_heads, axis=1)).astype(BF16)

    cq = s[:, :Q_LORA]
    cqn = cq * lax.rsqrt(jnp.mean(cq * cq, axis=-1, keepdims=True) + EPS) * qag_ref[...]
    q = _dot(cqn.astype(BF16), wuq_ref[...])
    tq = t1 + jnp.where(lane < QK_NOPE, 1.0, 0.0)
    q = q * _segment_rsqrt(q, e_ref, et_ref) * gq_ref[...] * jnp.concatenate([tq] * n_heads, axis=1)
    q_ref[0] = (q * (ATTN_SCALE * LOG2_E)).astype(BF16)

    xdt = s[:, NARROW_DT:NARROW_W] + dtb_ref[...]
    dt = jnp.maximum(xdt, 0.0) + jnp.log1p(jnp.exp(-jnp.abs(xdt)))
    dt = jnp.where(lane < 2 * SSM_HEADS, dt, 0.0)
    da = dt * (-LOG2_E * jnp.exp(alane_ref[...]))
    ii = lax.broadcasted_iota(jnp.int32, (CHUNK, CHUNK), 0)
    jj = lax.broadcasted_iota(jnp.int32, (CHUNK, CHUNK), 1)
    tri_f = (jj <= ii).astype(BF16)
    tri_b = (jj >= ii).astype(BF16)
    fwd_lane = (lane % (2 * HEADS_PER_GROUP)) < HEADS_PER_GROUP
    cums = []
    for ch in range(s.shape[0] // CHUNK):
        pieces = _split3(da[ch * CHUNK:(ch + 1) * CHUNK])
        cf = sum(_dot(tri_f, p) for p in pieces)
        cb = sum(_dot(tri_b, p) for p in pieces)
        cums.append(jnp.where(fwd_lane, cf, cb))
        crow_ref[0, ch] = cums[-1].T[:2 * SSM_HEADS]
    cum = jnp.concatenate(cums, axis=0)
    d3 = _split3(dt)
    c3 = _split3(cum)
    dc3 = jnp.concatenate([d3[0], c3[0], d3[1], c3[1], d3[2], c3[2]], axis=1)
    pg_ref[0] = _dot(dc3, perm_ref[...]).astype(BF16)


def _piece_permutation():
    per = 2 * HEADS_PER_GROUP
    perm = np.zeros((6 * LANES, SSM_GROUPS * LANES), np.float32)
    for piece in range(3):
        for kind in range(2):
            for g in range(SSM_GROUPS):
                for j in range(per):
                    perm[(piece * 2 + kind) * LANES + g * per + j, g * LANES + piece * 2 * per + kind * per + j] = 1.0
    return jnp.asarray(perm, BF16)


def _mla_call(narrow_c, narrow_x, t1, t2, consts, name):
    b, n_ctx, _ = narrow_c.shape
    n_lat = narrow_x.shape[1]
    tm = n_ctx
    s_all = n_ctx + n_lat
    last_latent = n_lat // tm - 1
    k_w = MLA_HEADS * HEAD_PAD
    full = lambda a: pl.BlockSpec(a.shape, lambda bi, i: (0,) * a.ndim)
    latent_tile = lambda bi, i: (bi, jnp.minimum(i, last_latent), 0)
    return pl.pallas_call(
        _mla_kernel,
        out_shape=[jax.ShapeDtypeStruct((b, s_all, k_w), BF16),
                   jax.ShapeDtypeStruct((b, s_all, k_w), BF16),
                   jax.ShapeDtypeStruct((b, s_all // tm, MLA_HEADS * VT_ROWS, tm), BF16),
                   jax.ShapeDtypeStruct((b, s_all // CHUNK, 2 * SSM_HEADS, CHUNK), F32),
                   jax.ShapeDtypeStruct((b, s_all, SSM_GROUPS * LANES), BF16)],
        grid=(b, s_all // tm),
        in_specs=[pl.BlockSpec((1, tm, NARROW_W), lambda bi, i: (bi, 0, 0)),
                  pl.BlockSpec((1, tm, NARROW_W), latent_tile),
                  pl.BlockSpec((tm, LANES), lambda bi, i: (i, 0)),
                  pl.BlockSpec((tm, LANES), lambda bi, i: (i, 0)),
                  *[full(a) for a in consts]],
        out_specs=[pl.BlockSpec((1, tm, k_w), lambda bi, i: (bi, i, 0)),
                   pl.BlockSpec((1, tm, k_w), lambda bi, i: (bi, i, 0)),
                   pl.BlockSpec((1, 1, MLA_HEADS * VT_ROWS, tm), lambda bi, i: (bi, i, 0, 0)),
                   pl.BlockSpec((1, tm // CHUNK, 2 * SSM_HEADS, CHUNK), lambda bi, i: (bi, i, 0, 0)),
                   pl.BlockSpec((1, tm, SSM_GROUPS * LANES), lambda bi, i: (bi, i, 0))],
        compiler_params=_params("parallel", "arbitrary"),
        name=name,
    )(narrow_c, narrow_x, t1, t2, *consts)


EXP_W = 4 * GROUP_W


def _ssd_kernel(nc, xs_ref, b_ref, c_ref, p_ref, crow_ref, sel_ref, dsk_ref, h0_ref,
                y_ref, hout_ref, yacc_ref, s_ref, ex_ref, cb_ref, dec_ref, h_ref):
    q = CHUNK
    r_heads = HEADS_PER_GROUP
    wide_l = r_heads * q
    ii_w = lax.broadcasted_iota(jnp.int32, (q, wide_l), 0)
    jj_w = lax.broadcasted_iota(jnp.int32, (q, wide_l), 1) % q
    keep_f = jj_w <= ii_w
    keep_b = jj_w >= ii_w
    lane_head = lax.broadcasted_iota(jnp.int32, (1, GROUP_W), 1) // SSM_HEAD_DIM
    low_half = lax.broadcasted_iota(jnp.int32, (1, LANES), 1) < SSM_HEAD_DIM
    dsk = dsk_ref[0:1, :] + dsk_ref[1:2, :]
    gw = GROUP_W

    def per_head_128(c64):
        out = []
        for pair in range(r_heads // 2):
            v = c64[:, pair * LANES:(pair + 1) * LANES]
            sw = pltpu.roll(v, SSM_HEAD_DIM, axis=1)
            out += [jnp.where(low_half, v, sw), jnp.where(low_half, sw, v)]
        return jnp.concatenate(out, axis=1)

    def block_diag(xdt):
        xb = xdt.astype(BF16)
        return jnp.concatenate([jnp.where(lane_head == r, xb, jnp.zeros_like(xb)) for r in range(r_heads)], axis=0)

    def expand(c, carry):
        rows = pl.ds(pl.multiple_of(c * q, q), q)
        ex_ref[rows, :] = _dot(p_ref[0, rows, :], sel_ref[...])
        cb_ref[c] = lax.dot_general(c_ref[0, rows, :], b_ref[0, rows, :], (((1,), (1,)), ((), ())),
                                    preferred_element_type=F32)
        return carry

    lax.fori_loop(0, nc, expand, 0, unroll=4 if nc % 4 == 0 else 2)

    def local(c, carry):
        rows = pl.ds(pl.multiple_of(c * q, q), q)
        x = xs_ref[0, rows, :].astype(F32)
        bm = b_ref[0, rows, :]
        dte_f, dte_b = ex_ref[rows, 0:gw], ex_ref[rows, gw:2 * gw]
        c64_f, c64_b = ex_ref[rows, 2 * gw:3 * gw], ex_ref[rows, 3 * gw:4 * gw]
        cw_f, cw_b = per_head_128(c64_f), per_head_128(c64_b)
        crow = crow_ref[0, c]
        flat_f = jnp.concatenate([crow[r:r + 1, :] for r in range(r_heads)], axis=1)
        flat_b = jnp.concatenate([crow[r_heads + r:r_heads + r + 1, :] for r in range(r_heads)], axis=1)
        cb4 = jnp.concatenate([cb_ref[c]] * r_heads, axis=1)
        l_f = (jnp.where(keep_f, jnp.exp2(cw_f - flat_f), 0.0) * cb4).astype(BF16)
        l_b = (jnp.where(keep_b, jnp.exp2(cw_b - flat_b), 0.0) * cb4).astype(BF16)
        xdt_f = x * dte_f
        xdt_b = x * dte_b
        y = _dot(jnp.concatenate([l_f, l_b], axis=1),
                 jnp.concatenate([block_diag(xdt_f), block_diag(xdt_b)], axis=0))
        yacc_ref[rows, :] = y + dsk * x
        edge_f = c64_f[q - 1:q, :]
        edge_b = c64_b[0:1, :]
        xw = jnp.concatenate([(xdt_f * jnp.exp2(edge_f - c64_f)).astype(BF16),
                              (xdt_b * jnp.exp2(edge_b - c64_b)).astype(BF16)], axis=1)
        st = lax.dot_general(bm, xw, (((0,), (0,)), ((), ())), preferred_element_type=F32)
        s_ref[0, c] = st[:, :gw]
        s_ref[1, c] = st[:, gw:]
        dec_ref[0, c] = jnp.broadcast_to(jnp.exp2(edge_f), (8, gw))
        dec_ref[1, c] = jnp.broadcast_to(jnp.exp2(edge_b), (8, gw))
        return carry

    lax.fori_loop(0, nc, local, 0, unroll=4 if nc % 4 == 0 else 2)

    h_ref[0] = h0_ref[0, 0, 0]
    h_ref[1] = h0_ref[0, 0, 1]

    def carry_states(t, carry):
        for d, c in ((0, t), (1, nc - 1 - t)):
            rows = pl.ds(pl.multiple_of(c * q, q), q)
            h = h_ref[d]
            yo = _dot(c_ref[0, rows, :], h.astype(BF16)) * jnp.exp2(ex_ref[rows, (2 + d) * gw:(3 + d) * gw])
            yacc_ref[rows, :] = yacc_ref[rows, :] + yo
            h_ref[d] = h * dec_ref[d, c, 0:1, :] + s_ref[d, c]
        return carry

    lax.fori_loop(0, nc, carry_states, 0, unroll=2)
    hout_ref[0, 0, 0] = h_ref[0]
    hout_ref[0, 0, 1] = h_ref[1]

    def emit(c, carry):
        rows = pl.ds(pl.multiple_of(c * q, q), q)
        y_ref[0, rows, :] = yacc_ref[rows, :].astype(y_ref.dtype)
        return carry

    lax.fori_loop(0, nc, emit, 0)


def _ssd_call(wide, pieces, crow, seq_block, sel, dsk, h0, name):
    xbc = wide
    b, n, _ = wide.shape
    nc = n // CHUNK
    g = SSM_GROUPS
    x_col0 = D_INNER // GROUP_W
    b_col0 = 2 * D_INNER // D_STATE
    c_col0 = b_col0 + g
    return pl.pallas_call(
        functools.partial(_ssd_kernel, nc),
        out_shape=[jax.ShapeDtypeStruct((b, n, D_INNER), BF16),
                   jax.ShapeDtypeStruct((b, g, 2, D_STATE, GROUP_W), F32)],
        grid=(b, g),
        in_specs=[pl.BlockSpec((1, n, GROUP_W), lambda bi, gi: (bi, 0, x_col0 + gi)),
                  pl.BlockSpec((1, n, D_STATE), lambda bi, gi: (bi, 0, b_col0 + gi)),
                  pl.BlockSpec((1, n, D_STATE), lambda bi, gi: (bi, 0, c_col0 + gi)),
                  pl.BlockSpec((1, n, LANES), lambda bi, gi: (bi, seq_block, gi)),
                  pl.BlockSpec((1, nc, 2 * HEADS_PER_GROUP, CHUNK), lambda bi, gi: (bi, seq_block, gi, 0)),
                  pl.BlockSpec((LANES, EXP_W), lambda bi, gi: (0, 0)),
                  pl.BlockSpec((2, GROUP_W), lambda bi, gi: (0, gi)),
                  pl.BlockSpec((1, 1, 2, D_STATE, GROUP_W), lambda bi, gi: (bi, gi, 0, 0, 0))],
        out_specs=[pl.BlockSpec((1, n, GROUP_W), lambda bi, gi: (bi, 0, gi)),
                   pl.BlockSpec((1, 1, 2, D_STATE, GROUP_W), lambda bi, gi: (bi, gi, 0, 0, 0))],
        scratch_shapes=[pltpu.VMEM((n, GROUP_W), F32),
                        pltpu.VMEM((2, nc, D_STATE, GROUP_W), F32),
                        pltpu.VMEM((n, EXP_W), F32),
                        pltpu.VMEM((nc, CHUNK, CHUNK), F32),
                        pltpu.VMEM((2, nc, 8, GROUP_W), F32),
                        pltpu.VMEM((2, D_STATE, GROUP_W), F32)],
        compiler_params=_params("parallel", "parallel"),
        name=name,
    )(xbc, xbc, xbc, pieces, crow, sel, dsk, h0)


VT_ROWS = V_DIM + 16

KEY_CHUNK = 512
ATTN_HEADS = 2


def _attn_kernel(q_ref, k_ref, vt_ref, o_ref, s_bufs, cmaxs, acc, m_run):
    nt = (((1,), (1,)), ((), ()))
    n_keys = k_ref.shape[1]
    blk = vt_ref.shape[3]
    per = KEY_CHUNK // blk
    n_full = n_keys // KEY_CHUNK
    tail = n_keys - n_full * KEY_CHUNK

    def produce(slot, c, n_rows=KEY_CHUNK):
        start = c * KEY_CHUNK
        rows = pl.ds(start if isinstance(c, int) else pl.multiple_of(start, KEY_CHUNK), n_rows)
        for h in range(ATTN_HEADS):
            st = lax.dot_general(k_ref[0, rows, h * HEAD_PAD:(h + 1) * HEAD_PAD],
                                 q_ref[0, :, h * HEAD_PAD:(h + 1) * HEAD_PAD], nt, preferred_element_type=F32)
            s_bufs[slot][h, 0:n_rows, :] = st
            cmaxs[slot][h, 0:1, :] = jnp.max(st, axis=0, keepdims=True)

    def consume(slot, c, n_rows=KEY_CHUNK):
        for h in range(ATTN_HEADS):
            m_old = m_run[h, 0:1, :]
            m_new = jnp.maximum(m_old, cmaxs[slot][h, 0:1, :])
            p = jnp.exp2(s_bufs[slot][h, 0:n_rows, :] - m_new).astype(BF16)
            vt = jnp.concatenate([vt_ref[0, per * c + j, h * VT_ROWS:(h + 1) * VT_ROWS, :]
                                  for j in range(n_rows // blk)], axis=1)
            acc[h] = acc[h] * jnp.exp2(m_old - m_new) + _dot(vt, p)
            m_run[h, 0:1, :] = m_new

    m_run[...] = jnp.full(m_run.shape, -jnp.inf, F32)
    acc[...] = jnp.zeros(acc.shape, F32)
    n_chunks = n_full + (1 if tail else 0)
    size = lambda c: tail if c == n_full else KEY_CHUNK
    produce(0, 0)

    def body(j, carry):
        produce(1, 2 * j + 1)
        consume(0, 2 * j)
        produce(0, 2 * j + 2)
        consume(1, 2 * j + 1)
        return carry

    trips = (n_full - 1) // 2
    lax.fori_loop(0, trips, body, 0)
    for c in range(2 * trips, n_chunks):
        slot = c % 2
        if c + 1 < n_chunks:
            produce(1 - slot, c + 1, size(c + 1))
        consume(slot, c, size(c))
    outs = [acc[h, 0:V_DIM, :] / acc[h, V_DIM:V_DIM + 1, :] for h in range(ATTN_HEADS)]
    o_ref[0] = jnp.concatenate(outs, axis=0).T.astype(o_ref.dtype)


def _attn_call(q, k, vt, n, tq, name):
    b = q.shape[0]
    s = k.shape[1]
    n_blk, _, blk = vt.shape[1:]
    nh = ATTN_HEADS
    return pl.pallas_call(
        _attn_kernel,
        out_shape=jax.ShapeDtypeStruct((b, n, MLA_HEADS * V_DIM), BF16),
        grid=(b, MLA_HEADS // nh, n // tq),
        in_specs=[pl.BlockSpec((1, tq, nh * HEAD_PAD), lambda bi, hp, i: (bi, i, hp)),
                  pl.BlockSpec((1, s, nh * HEAD_PAD), lambda bi, hp, i: (bi, 0, hp)),
                  pl.BlockSpec((1, n_blk, nh * VT_ROWS, blk), lambda bi, hp, i: (bi, 0, hp, 0))],
        out_specs=pl.BlockSpec((1, tq, nh * V_DIM), lambda bi, hp, i: (bi, i, hp)),
        scratch_shapes=[[pltpu.VMEM((nh, KEY_CHUNK, tq), F32)] * 2,
                        [pltpu.VMEM((nh, 8, tq), F32)] * 2,
                        pltpu.VMEM((nh, VT_ROWS, tq), F32),
                        pltpu.VMEM((nh, 8, tq), F32)],
        compiler_params=_params("parallel", "parallel", "parallel"),
        name=name,
    )(q, k, vt)


def _tail_kernel(ff_chunk, att_ref, y_ref, z_ref, ga_ref, gb_ref, x_ref, mod_ref, sg_ref, g2_ref,
                 wa_ref, wb_ref, wo_ref, w1_ref, w2_ref, o_ref):
    y = y_ref[0].astype(F32)
    z = z_ref[0].astype(F32)
    u = y * (z * _sigmoid(z))
    gw = D_INNER // SSM_GROUPS
    parts = []
    for g in range(SSM_GROUPS):
        ug = u[:, g * gw:(g + 1) * gw]
        parts.append(ug * lax.rsqrt(jnp.mean(ug * ug, axis=-1, keepdims=True) + EPS))
    un = (jnp.concatenate(parts, axis=1) * sg_ref[...]).astype(BF16)
    a = _dot(att_ref[0], wa_ref[...])
    bb = _dot(un, wb_ref[...])
    merged = _sigmoid(ga_ref[0].astype(F32)) * a + _sigmoid(gb_ref[0].astype(F32)) * bb
    o = _dot(merged.astype(BF16), wo_ref[...])
    x = x_ref[0] + mod_ref[0, 2:3, :] * o

    ms = jnp.mean(x * x, axis=-1, keepdims=True)
    xn = x * lax.rsqrt(ms + EPS) * g2_ref[...]
    h = (xn * (1.0 + mod_ref[0, 4:5, :]) + mod_ref[0, 3:4, :]).astype(BF16)
    acc = None
    for c in range(D_FF // ff_chunk):
        a = _dot(h, w1_ref[:, c * ff_chunk:(c + 1) * ff_chunk])
        a = jnp.maximum(a, 0.0)
        part = _dot((a * a).astype(BF16), w2_ref[c * ff_chunk:(c + 1) * ff_chunk, :])
        acc = part if acc is None else acc + part
    o_ref[0] = x + mod_ref[0, 5:6, :] * acc


def _tail_call(att, y, wide, x, mod, sg, g2, wa, wb, wo, w1, w2, tm, name):
    b, n, d = x.shape
    ga_blk = (D_INNER + CONV_DIM) // d
    full = lambda a: pl.BlockSpec(a.shape, lambda bi, i: (0,) * a.ndim)
    resident = lambda a: pl.BlockSpec(a.shape, lambda bi, i: (0,) * a.ndim, pipeline_mode=pl.Buffered(1))
    return pl.pallas_call(
        functools.partial(_tail_kernel, FF_CHUNK),
        out_shape=jax.ShapeDtypeStruct((b, n, d), F32),
        grid=(b, n // tm),
        in_specs=[pl.BlockSpec((1, tm, MLA_HEADS * V_DIM), lambda bi, i: (bi, i, 0)),
                  pl.BlockSpec((1, tm, D_INNER), lambda bi, i: (bi, i, 0)),
                  pl.BlockSpec((1, tm, D_INNER), lambda bi, i: (bi, i, 0)),
                  pl.BlockSpec((1, tm, d), lambda bi, i: (bi, i, ga_blk)),
                  pl.BlockSpec((1, tm, d), lambda bi, i: (bi, i, ga_blk + 1)),
                  pl.BlockSpec((1, tm, d), lambda bi, i: (bi, i, 0)),
                  pl.BlockSpec((1, 8, d), lambda bi, i: (bi, 0, 0)),
                  full(sg), full(g2), resident(wa), resident(wb), resident(wo), resident(w1), resident(w2)],
        out_specs=pl.BlockSpec((1, tm, d), lambda bi, i: (bi, i, 0)),
        compiler_params=_params("parallel", "parallel"),
        name=name,
    )(att, y, wide, wide, wide, x, mod, sg, g2, wa, wb, wo, w1, w2)


_ROPE_SWAP = np.concatenate([np.arange(8, 16), np.arange(0, 8), np.arange(24, 32), np.arange(16, 24)])
_DT_LANE_ORDER = np.array([d * SSM_HEADS + g * HEADS_PER_GROUP + r for g in range(SSM_GROUPS) for d in range(2)
                           for r in range(HEADS_PER_GROUP)])


def _rope_tables(n_lat):
    rows = n_lat // GRID_W
    row = jnp.repeat(jnp.arange(rows), GRID_W)
    col = jnp.tile(jnp.arange(GRID_W), rows)
    freqs = ROPE_BASE ** (-jnp.arange(ROPE_PAIRS, dtype=F32) / ROPE_PAIRS)
    ang = jnp.stack([row, col], axis=-1).astype(F32)[..., None] * freqs
    cos, sin = jnp.cos(ang), jnp.sin(ang)
    cos32 = jnp.stack([cos, cos], axis=2).reshape(n_lat, QK_ROPE)
    sin32 = jnp.stack([-sin, sin], axis=2).reshape(n_lat, QK_ROPE)
    zeros = jnp.zeros((n_lat, QK_NOPE), F32)
    return (jnp.concatenate([zeros, cos32, sin32], axis=1),
            jnp.concatenate([zeros, sin32, cos32], axis=1))


def _identity_rope_tables(n):
    zeros = jnp.zeros((n, QK_NOPE), F32)
    one = jnp.ones((n, QK_ROPE), F32)
    zero = jnp.zeros((n, QK_ROPE), F32)
    return jnp.concatenate([zeros, one, zero], axis=1), jnp.concatenate([zeros, zero, one], axis=1)


def _layout_params(w_in, w_uq, w_ukv, q_a_g, kv_a_g, qk_q_g, qk_k_g, dt_bias, a_log):
    d = w_in.shape[0]
    o_cq, o_ckv, o_kr = 0, Q_LORA, Q_LORA + KV_LORA
    o_z = o_kr + QK_ROPE
    o_xbc = o_z + D_INNER
    o_dt = o_xbc + CONV_DIM
    o_ga = o_dt + 2 * SSM_HEADS
    kr = w_in[:, o_kr:o_kr + QK_ROPE]
    kr_sw = kr[:, _ROPE_SWAP]
    z64 = jnp.zeros((d, LANES - 2 * QK_ROPE), F32)
    w_narrow = jnp.concatenate(
        [w_in[:, o_cq:o_kr], z64, kr, kr_sw, z64, kr_sw, kr, w_in[:, o_dt:o_ga][:, _DT_LANE_ORDER], z64],
        axis=1).astype(BF16)
    w_wide = jnp.concatenate([w_in[:, o_z:o_dt], w_in[:, o_ga:]], axis=1).astype(BF16)

    hq = w_uq.reshape(Q_LORA, MLA_HEADS, QK_NOPE + QK_ROPE)
    rope_q = hq[:, :, QK_NOPE:]
    wuq = jnp.concatenate([hq[:, :, :QK_NOPE], rope_q, rope_q[:, :, _ROPE_SWAP]], axis=2)
    wuq = wuq.reshape(Q_LORA, MLA_HEADS * HEAD_PAD).astype(BF16)
    hkv = w_ukv.reshape(KV_LORA, MLA_HEADS, QK_NOPE + V_DIM)
    wk = jnp.concatenate([hkv[:, :, :QK_NOPE], jnp.zeros((KV_LORA, MLA_HEADS, HEAD_PAD - QK_NOPE), F32)], axis=2)
    wukv = jnp.concatenate([wk.reshape(KV_LORA, MLA_HEADS * HEAD_PAD),
                            hkv[:, :, QK_NOPE:].reshape(KV_LORA, MLA_HEADS * V_DIM)], axis=1).astype(BF16)

    gq_r = qk_q_g[QK_NOPE:]
    gq = jnp.tile(jnp.concatenate([qk_q_g[:QK_NOPE], gq_r, gq_r[_ROPE_SWAP]]), MLA_HEADS)[None, :]
    gk = jnp.tile(jnp.concatenate([qk_k_g[:QK_NOPE], jnp.zeros((HEAD_PAD - QK_NOPE,), F32)]), MLA_HEADS)[None, :]
    gk_r = qk_k_g[QK_NOPE:]
    z64v = jnp.zeros((QK_NOPE,), F32)
    ga = jnp.concatenate([z64v, gk_r, gk_r[_ROPE_SWAP]])[None, :]
    gb = jnp.concatenate([z64v, gk_r[_ROPE_SWAP], gk_r])[None, :]
    lane_pad = jnp.zeros((LANES - 2 * SSM_HEADS,), F32)
    dtb = jnp.concatenate([dt_bias.reshape(-1)[_DT_LANE_ORDER], lane_pad])[None, :]
    alane = jnp.concatenate([a_log.astype(F32).reshape(-1)[_DT_LANE_ORDER], lane_pad])[None, :]

    lane = np.arange(MLA_HEADS * HEAD_PAD)
    head, off = lane // HEAD_PAD, lane % HEAD_PAD
    e = np.zeros((MLA_HEADS * HEAD_PAD, LANES), np.float32)
    e[lane[off < QK_NOPE], 2 * head[off < QK_NOPE]] = 1.0 / QK_NOPE
    rope = (off >= QK_NOPE) & (off < QK_NOPE + QK_ROPE)
    e[lane[rope], 2 * head[rope] + 1] = 1.0 / QK_ROPE
    et = np.zeros((LANES, MLA_HEADS * HEAD_PAD), np.float32)
    et[2 * head[off < QK_NOPE], lane[off < QK_NOPE]] = 1.0
    et[2 * head[off >= QK_NOPE] + 1, lane[off >= QK_NOPE]] = 1.0
    et = np.concatenate([et, et], axis=0)
    consts = (q_a_g[None, :], kv_a_g[None, :], wuq, wukv, jnp.asarray(e, BF16), jnp.asarray(et, BF16),
              gq, gk, ga, gb, dtb, alane, _piece_permutation())
    return w_narrow, w_wide, consts


def _expansion_selector():
    r = HEADS_PER_GROUP
    sel = np.zeros((LANES, EXP_W), np.float32)
    for piece in range(3):
        for d in range(2):
            for h in range(r):
                row_dt = piece * 4 * r + d * r + h
                row_cum = piece * 4 * r + 2 * r + d * r + h
                sel[row_dt, d * GROUP_W + h * SSM_HEAD_DIM:d * GROUP_W + (h + 1) * SSM_HEAD_DIM] = 1.0
                base = 2 * GROUP_W + d * GROUP_W + h * SSM_HEAD_DIM
                sel[row_cum, base:base + SSM_HEAD_DIM] = 1.0
    return jnp.asarray(sel, BF16)


def kernel(x, c, ctx, c_ctx, norm1_g, norm2_g, w_mod, b_mod, w_in, q_a_g, w_uq, kv_a_g, w_ukv, qk_q_g, qk_k_g,
           conv_w, conv_b, dt_bias, a_log, d_skip, ssm_norm_g, w_proj_a, w_proj_b, w_out, w_mlp1, w_mlp2):
    assert w_mod.shape[0] == 1, "single-layer block"
    b, n_lat, d = x.shape
    n_ctx = ctx.shape[1]

    cvec = jnp.concatenate([c, c_ctx[None, :], jnp.zeros((8 - b - 1, d), F32)], axis=0)
    mod = _mod_call(cvec, w_mod[0], b_mod[0][None, :]).reshape(8, N_MOD, d)
    mod = jnp.concatenate([mod, jnp.zeros((8, 8 - N_MOD, d), F32)], axis=1)
    mod_x, mod_c = mod[:b], mod[b:b + 1]

    w_narrow, w_wide, consts = _layout_params(w_in[0], w_uq[0], w_ukv[0], q_a_g[0], kv_a_g[0], qk_q_g[0],
                                              qk_k_g[0], dt_bias[0], a_log[0])
    g1 = norm1_g[0][None, :]
    conv_wl, conv_bl = conv_w[0], conv_b[0][None, :]
    dsk = jnp.repeat(d_skip[0], SSM_HEAD_DIM, axis=1)
    sel = _expansion_selector()

    narrow_c, wide_c = _front_call(ctx, g1, mod_c, w_narrow, w_wide, conv_wl, conv_bl, n_ctx, True, "front_ctx")
    narrow_x, wide_x = _front_call(x, g1, mod_x, w_narrow, w_wide, conv_wl, conv_bl, FRONT_TM, False, "front")

    t1c, t2c = _identity_rope_tables(n_ctx)
    t1x, t2x = _rope_tables(n_lat)
    q_all, k_all, vt_all, crow, pieces = _mla_call(narrow_c, narrow_x, jnp.concatenate([t1x, t1c], axis=0),
                                                   jnp.concatenate([t2x, t2c], axis=0), consts, "mla")

    h_zero = jnp.zeros((b, SSM_GROUPS, 2, D_STATE, GROUP_W), F32)
    _, h_ctx = _ssd_call(wide_c, pieces, crow, n_lat // n_ctx, sel, dsk, h_zero, "ssd_ctx")
    y_x, _ = _ssd_call(wide_x, pieces, crow, 0, sel, dsk, h_ctx, "ssd")

    att = _attn_call(q_all, k_all, vt_all, n_lat, ATTN_TQ, "attn")

    return _tail_call(att, y_x, wide_x, x, mod_x, ssm_norm_g[0][None, :], norm2_g[0][None, :],
                      w_proj_a[0].astype(BF16), w_proj_b[0].astype(BF16), w_out[0].astype(BF16),
                      w_mlp1[0].astype(BF16), w_mlp2[0].astype(BF16), TAIL_TM, "tail")
```

```python
import functools

import numpy as np
import jax
import jax.numpy as jnp
from jax import lax
from jax.experimental import pallas as pl
from jax.experimental.pallas import tpu as pltpu

F32 = jnp.float32
BF16 = jnp.bfloat16

D_MODEL = 1024
GRID_W = 64
MLA_HEADS = 16
QK_NOPE = 64
QK_ROPE = 32
V_DIM = 64
Q_LORA = 256
KV_LORA = 128
ROPE_PAIRS = QK_ROPE // 4
ROPE_BASE = 10000.0
ATTN_SCALE = (QK_NOPE + QK_ROPE) ** -0.5
LOG2_E = 1.4426950408889634
D_INNER = 2 * D_MODEL
SSM_HEAD_DIM = 64
SSM_HEADS = D_INNER // SSM_HEAD_DIM
SSM_GROUPS = 8
HEADS_PER_GROUP = SSM_HEADS // SSM_GROUPS
D_STATE = 128
D_CONV = 5
CHUNK = 128
CONV_DIM = D_INNER + 2 * SSM_GROUPS * D_STATE
D_FF = 4 * D_MODEL
N_MOD = 6
EPS = 1e-6

LANES = 128
HEAD_PAD = 128
GROUP_W = HEADS_PER_GROUP * SSM_HEAD_DIM
NARROW_ROPE_A = Q_LORA + KV_LORA
NARROW_ROPE_B = NARROW_ROPE_A + LANES
NARROW_DT = NARROW_ROPE_B + LANES
NARROW_W = NARROW_DT + LANES
WIDE_W = D_INNER + CONV_DIM + 2 * D_MODEL
VMEM_LIMIT = 56 * 1024 * 1024
FRONT_TM = 256
TAIL_TM = 512
ATTN_TQ = 1024
MOD_TN = 1024
FF_CHUNK = 1024


def _sigmoid(x):
    return 1.0 / (1.0 + jnp.exp(-x))


def _split3(x):
    hi = x.astype(BF16)
    r1 = x - hi.astype(F32)
    mid = r1.astype(BF16)
    lo = (r1 - mid.astype(F32)).astype(BF16)
    return hi, mid, lo


def _dot(a, b):
    return jnp.dot(a, b, preferred_element_type=F32)


def _params(*sem):
    return pltpu.CompilerParams(dimension_semantics=sem, vmem_limit_bytes=VMEM_LIMIT)


def _mod_kernel(c_ref, w_ref, b_ref, o_ref):
    c = c_ref[...]
    a = c * _sigmoid(c)
    w = w_ref[...]
    a_hi = a.astype(BF16)
    a_lo = (a - a_hi.astype(F32)).astype(BF16)
    w_hi = w.astype(BF16)
    w_lo = (w - w_hi.astype(F32)).astype(BF16)
    o_ref[...] = _dot(a_hi, w_hi) + _dot(a_hi, w_lo) + _dot(a_lo, w_hi) + b_ref[...]


def _mod_call(cvec, w_mod, b_mod):
    rows, d = cvec.shape
    n_out = w_mod.shape[1]
    tn = MOD_TN
    return pl.pallas_call(
        _mod_kernel,
        out_shape=jax.ShapeDtypeStruct((rows, n_out), F32),
        grid=(n_out // tn,),
        in_specs=[pl.BlockSpec((rows, d), lambda j: (0, 0)),
                  pl.BlockSpec((d, tn), lambda j: (0, j)),
                  pl.BlockSpec((1, tn), lambda j: (0, j))],
        out_specs=pl.BlockSpec((rows, tn), lambda j: (0, j)),
        compiler_params=_params("parallel"),
        name="mod",
    )(cvec, w_mod, b_mod)


FRONT_HALO = 16
FRONT_COLS = 1024


def _front_kernel(x_ref, xp_ref, xn_ref, g_ref, mod_ref, wn_ref, ww_ref, cw_ref, cb_ref, narrow_ref, wide_ref, pre_ref):
    i = pl.program_id(1)
    last = pl.num_programs(1) - 1
    tm = x_ref.shape[1]
    g = g_ref[...]
    shift = mod_ref[0, 0:1, :]
    scale = mod_ref[0, 1:2, :]

    def normmod(xv):
        ms = jnp.mean(xv * xv, axis=-1, keepdims=True)
        return xv * lax.rsqrt(ms + EPS) * g * (1.0 + scale) + shift

    hc = normmod(x_ref[0]).astype(BF16)
    hp = (normmod(xp_ref[0]) * (i > 0).astype(F32)).astype(BF16)
    hn = (normmod(xn_ref[0]) * (i < last).astype(F32)).astype(BF16)
    h_ext = jnp.concatenate([hp, hc, hn], axis=0)
    narrow_ref[0] = _dot(hc, wn_ref[...])
    pad = (D_CONV - 1) // 2
    n_chunks = WIDE_W // FRONT_COLS
    is_conv = [D_INNER <= j * FRONT_COLS < D_INNER + CONV_DIM for j in range(n_chunks)]
    conv_js = [j for j in range(n_chunks) if is_conv[j]]
    plain_js = [j for j in range(n_chunks) if not is_conv[j]]
    order = [j for pair in zip(conv_js, plain_js) for j in pair] + conv_js[len(plain_js):] + plain_js[len(conv_js):]
    for j in order:
        lo = j * FRONT_COLS
        cols = slice(lo, lo + FRONT_COLS)
        if D_INNER <= lo < D_INNER + CONV_DIM:
            slot = j % 2
            pre_ref[slot] = _dot(h_ext, ww_ref[:, cols])
            cc = slice(lo - D_INNER, lo - D_INNER + FRONT_COLS)
            acc = cb_ref[:, cc] + cw_ref[0:1, cc] * pre_ref[slot, pl.ds(FRONT_HALO - pad, tm), :]
            for k in range(1, D_CONV):
                acc = acc + cw_ref[k:k + 1, cc] * pre_ref[slot, pl.ds(FRONT_HALO - pad + k, tm), :]
            wide_ref[0, :, cols] = (acc * _sigmoid(acc)).astype(BF16)
        else:
            wide_ref[0, :, cols] = _dot(hc, ww_ref[:, cols]).astype(BF16)


def _front_call(x, g, mod, w_narrow, w_wide, conv_w, conv_b, tm, shared_mod, name):
    b, n, d = x.shape
    per = tm // FRONT_HALO
    n_halo = n // FRONT_HALO
    mod_map = (lambda bi, i: (0, 0, 0)) if shared_mod else (lambda bi, i: (bi, 0, 0))
    resident = lambda a: pl.BlockSpec(a.shape, lambda bi, i: (0,) * a.ndim, pipeline_mode=pl.Buffered(1))
    return pl.pallas_call(
        _front_kernel,
        out_shape=[jax.ShapeDtypeStruct((b, n, NARROW_W), F32), jax.ShapeDtypeStruct((b, n, WIDE_W), BF16)],
        grid=(b, n // tm),
        in_specs=[pl.BlockSpec((1, tm, d), lambda bi, i: (bi, i, 0)),
                  pl.BlockSpec((1, FRONT_HALO, d), lambda bi, i: (bi, jnp.maximum(i * per - 1, 0), 0)),
                  pl.BlockSpec((1, FRONT_HALO, d), lambda bi, i: (bi, jnp.minimum((i + 1) * per, n_halo - 1), 0)),
                  pl.BlockSpec((1, d), lambda bi, i: (0, 0)),
                  pl.BlockSpec((1, 8, d), mod_map),
                  resident(w_narrow), resident(w_wide), resident(conv_w), resident(conv_b)],
        out_specs=[pl.BlockSpec((1, tm, NARROW_W), lambda bi, i: (bi, i, 0)),
                   pl.BlockSpec((1, tm, WIDE_W), lambda bi, i: (bi, i, 0))],
        scratch_shapes=[pltpu.VMEM((2, tm + 2 * FRONT_HALO, FRONT_COLS), F32)],
        compiler_params=_params("parallel", "parallel"),
        name=name,
    )(x, x, x, g, mod, w_narrow, w_wide, conv_w, conv_b)


def _segment_rsqrt(t, e_ref, et_ref):
    ms = _dot((t * t).astype(BF16), e_ref[...])
    r = lax.rsqrt(ms + EPS)
    hi = r.astype(BF16)
    lo = (r - hi.astype(F32)).astype(BF16)
    return _dot(jnp.concatenate([hi, lo], axis=1), et_ref[...])


def _mla_kernel(sc_ref, sx_ref, t1_ref, t2_ref, qag_ref, kvag_ref, wuq_ref, wukv_ref, e_ref, et_ref,
                gq_ref, gk_ref, ga_ref, gb_ref, dtb_ref, alane_ref, perm_ref, q_ref, k_ref, vt_ref, crow_ref, pg_ref):
    is_ctx = pl.program_id(1) == pl.num_programs(1) - 1
    s = jnp.where(is_ctx, sc_ref[0], sx_ref[0])
    t1 = t1_ref[...]
    t2 = t2_ref[...]
    lane = lax.broadcasted_iota(jnp.int32, (1, LANES), 1)
    n_heads = MLA_HEADS
    k_w = n_heads * HEAD_PAD

    ckv = s[:, Q_LORA:Q_LORA + KV_LORA]
    ckvn = ckv * lax.rsqrt(jnp.mean(ckv * ckv, axis=-1, keepdims=True) + EPS) * kvag_ref[...]
    kv = _dot(ckvn.astype(BF16), wukv_ref[...])
    kn = kv[:, :k_w]
    vt = kv[:, k_w:].T
    ones = jnp.ones((VT_ROWS - V_DIM, vt.shape[1]), F32)
    vt_ref[0, 0] = jnp.concatenate(
        [blk for h in range(n_heads) for blk in (vt[h * V_DIM:(h + 1) * V_DIM], ones)], axis=0).astype(BF16)
    kn = kn * _segment_rsqrt(kn, e_ref, et_ref) * gk_ref[...]
    g1 = s[:, NARROW_ROPE_A:NARROW_ROPE_B]
    g2 = s[:, NARROW_ROPE_B:NARROW_DT]
    rope_lanes = (lane >= QK_NOPE) & (lane < QK_NOPE + QK_ROPE)
    kr_ms = jnp.sum(jnp.where(rope_lanes, g1 * g1, 0.0), axis=-1, keepdims=True) * (1.0 / QK_ROPE)
    krot = lax.rsqrt(kr_ms + EPS) * (g1 * ga_ref[...] * t1 + g2 * gb_ref[...] * t2)
    k_ref[0] = (kn + jnp.concatenate([krot] * n_heads, axis=1)).astype(BF16)

    cq = s[:, :Q_LORA]
    cqn = cq * lax.rsqrt(jnp.mean(cq * cq, axis=-1, keepdims=True) + EPS) * qag_ref[...]
    q = _dot(cqn.astype(BF16), wuq_ref[...])
    tq = t1 + jnp.where(lane < QK_NOPE, 1.0, 0.0)
    q = q * _segment_rsqrt(q, e_ref, et_ref) * gq_ref[...] * jnp.concatenate([tq] * n_heads, axis=1)
    q_ref[0] = (q * (ATTN_SCALE * LOG2_E)).astype(BF16)

    xdt = s[:, NARROW_DT:NARROW_W] + dtb_ref[...]
    dt = jnp.maximum(xdt, 0.0) + jnp.log1p(jnp.exp(-jnp.abs(xdt)))
    dt = jnp.where(lane < 2 * SSM_HEADS, dt, 0.0)
    da = dt * (-LOG2_E * jnp.exp(alane_ref[...]))
    ii = lax.broadcasted_iota(jnp.int32, (CHUNK, CHUNK), 0)
    jj = lax.broadcasted_iota(jnp.int32, (CHUNK, CHUNK), 1)
    tri_f = (jj <= ii).astype(BF16)
    tri_b = (jj >= ii).astype(BF16)
    fwd_lane = (lane % (2 * HEADS_PER_GROUP)) < HEADS_PER_GROUP
    cums = []
    for ch in range(s.shape[0] // CHUNK):
        pieces = _split3(da[ch * CHUNK:(ch + 1) * CHUNK])
        cf = sum(_dot(tri_f, p) for p in pieces)
        cb = sum(_dot(tri_b, p) for p in pieces)
        cums.append(jnp.where(fwd_lane, cf, cb))
        crow_ref[0, ch] = cums[-1].T[:2 * SSM_HEADS]
    cum = jnp.concatenate(cums, axis=0)
    d3 = _split3(dt)
    c3 = _split3(cum)
    dc3 = jnp.concatenate([d3[0], c3[0], d3[1], c3[1], d3[2], c3[2]], axis=1)
    pg_ref[0] = _dot(dc3, perm_ref[...]).astype(BF16)


def _piece_permutation():
    per = 2 * HEADS_PER_GROUP
    perm = np.zeros((6 * LANES, SSM_GROUPS * LANES), np.float32)
    for piece in range(3):
        for kind in range(2):
            for g in range(SSM_GROUPS):
                for j in range(per):
                    perm[(piece * 2 + kind) * LANES + g * per + j, g * LANES + piece * 2 * per + kind * per + j] = 1.0
    return jnp.asarray(perm, BF16)


def _mla_call(narrow_c, narrow_x, t1, t2, consts, name):
    b, n_ctx, _ = narrow_c.shape
    n_lat = narrow_x.shape[1]
    tm = n_ctx
    s_all = n_ctx + n_lat
    last_latent = n_lat // tm - 1
    k_w = MLA_HEADS * HEAD_PAD
    full = lambda a: pl.BlockSpec(a.shape, lambda bi, i: (0,) * a.ndim)
    latent_tile = lambda bi, i: (bi, jnp.minimum(i, last_latent), 0)
    return pl.pallas_call(
        _mla_kernel,
        out_shape=[jax.ShapeDtypeStruct((b, s_all, k_w), BF16),
                   jax.ShapeDtypeStruct((b, s_all, k_w), BF16),
                   jax.ShapeDtypeStruct((b, s_all // tm, MLA_HEADS * VT_ROWS, tm), BF16),
                   jax.ShapeDtypeStruct((b, s_all // CHUNK, 2 * SSM_HEADS, CHUNK), F32),
                   jax.ShapeDtypeStruct((b, s_all, SSM_GROUPS * LANES), BF16)],
        grid=(b, s_all // tm),
        in_specs=[pl.BlockSpec((1, tm, NARROW_W), lambda bi, i: (bi, 0, 0)),
                  pl.BlockSpec((1, tm, NARROW_W), latent_tile),
                  pl.BlockSpec((tm, LANES), lambda bi, i: (i, 0)),
                  pl.BlockSpec((tm, LANES), lambda bi, i: (i, 0)),
                  *[full(a) for a in consts]],
        out_specs=[pl.BlockSpec((1, tm, k_w), lambda bi, i: (bi, i, 0)),
                   pl.BlockSpec((1, tm, k_w), lambda bi, i: (bi, i, 0)),
                   pl.BlockSpec((1, 1, MLA_HEADS * VT_ROWS, tm), lambda bi, i: (bi, i, 0, 0)),
                   pl.BlockSpec((1, tm // CHUNK, 2 * SSM_HEADS, CHUNK), lambda bi, i: (bi, i, 0, 0)),
                   pl.BlockSpec((1, tm, SSM_GROUPS * LANES), lambda bi, i: (bi, i, 0))],
        compiler_params=_params("parallel", "arbitrary"),
        name=name,
    )(narrow_c, narrow_x, t1, t2, *consts)


EXP_W = 4 * GROUP_W


def _ssd_kernel(nc, xs_ref, b_ref, c_ref, p_ref, crow_ref, sel_ref, dsk_ref, h0_ref,
                y_ref, hout_ref, yacc_ref, s_ref, ex_ref, cb_ref, dec_ref, h_ref):
    q = CHUNK
    r_heads = HEADS_PER_GROUP
    wide_l = r_heads * q
    ii_w = lax.broadcasted_iota(jnp.int32, (q, wide_l), 0)
    jj_w = lax.broadcasted_iota(jnp.int32, (q, wide_l), 1) % q
    keep_f = jj_w <= ii_w
    keep_b = jj_w >= ii_w
    lane_head = lax.broadcasted_iota(jnp.int32, (1, GROUP_W), 1) // SSM_HEAD_DIM
    low_half = lax.broadcasted_iota(jnp.int32, (1, LANES), 1) < SSM_HEAD_DIM
    dsk = dsk_ref[0:1, :] + dsk_ref[1:2, :]
    gw = GROUP_W

    def per_head_128(c64):
        out = []
        for pair in range(r_heads // 2):
            v = c64[:, pair * LANES:(pair + 1) * LANES]
            sw = pltpu.roll(v, SSM_HEAD_DIM, axis=1)
            out += [jnp.where(low_half, v, sw), jnp.where(low_half, sw, v)]
        return jnp.concatenate(out, axis=1)

    def block_diag(xdt):
        xb = xdt.astype(BF16)
        return jnp.concatenate([jnp.where(lane_head == r, xb, jnp.zeros_like(xb)) for r in range(r_heads)], axis=0)

    def expand(c, carry):
        rows = pl.ds(pl.multiple_of(c * q, q), q)
        ex_ref[rows, :] = _dot(p_ref[0, rows, :], sel_ref[...])
        cb_ref[c] = lax.dot_general(c_ref[0, rows, :], b_ref[0, rows, :], (((1,), (1,)), ((), ())),
                                    preferred_element_type=F32)
        return carry

    lax.fori_loop(0, nc, expand, 0, unroll=4 if nc % 4 == 0 else 2)

    def local(c, carry):
        rows = pl.ds(pl.multiple_of(c * q, q), q)
        x = xs_ref[0, rows, :].astype(F32)
        bm = b_ref[0, rows, :]
        dte_f, dte_b = ex_ref[rows, 0:gw], ex_ref[rows, gw:2 * gw]
        c64_f, c64_b = ex_ref[rows, 2 * gw:3 * gw], ex_ref[rows, 3 * gw:4 * gw]
        cw_f, cw_b = per_head_128(c64_f), per_head_128(c64_b)
        crow = crow_ref[0, c]
        flat_f = jnp.concatenate([crow[r:r + 1, :] for r in range(r_heads)], axis=1)
        flat_b = jnp.concatenate([crow[r_heads + r:r_heads + r + 1, :] for r in range(r_heads)], axis=1)
        cb4 = jnp.concatenate([cb_ref[c]] * r_heads, axis=1)
        l_f = (jnp.where(keep_f, jnp.exp2(cw_f - flat_f), 0.0) * cb4).astype(BF16)
        l_b = (jnp.where(keep_b, jnp.exp2(cw_b - flat_b), 0.0) * cb4).astype(BF16)
        xdt_f = x * dte_f
        xdt_b = x * dte_b
        y = _dot(jnp.concatenate([l_f, l_b], axis=1),
                 jnp.concatenate([block_diag(xdt_f), block_diag(xdt_b)], axis=0))
        yacc_ref[rows, :] = y + dsk * x
        edge_f = c64_f[q - 1:q, :]
        edge_b = c64_b[0:1, :]
        xw = jnp.concatenate([(xdt_f * jnp.exp2(edge_f - c64_f)).astype(BF16),
                              (xdt_b * jnp.exp2(edge_b - c64_b)).astype(BF16)], axis=1)
        st = lax.dot_general(bm, xw, (((0,), (0,)), ((), ())), preferred_element_type=F32)
        s_ref[0, c] = st[:, :gw]
        s_ref[1, c] = st[:, gw:]
        dec_ref[0, c] = jnp.broadcast_to(jnp.exp2(edge_f), (8, gw))
        dec_ref[1, c] = jnp.broadcast_to(jnp.exp2(edge_b), (8, gw))
        return carry

    lax.fori_loop(0, nc, local, 0, unroll=4 if nc % 4 == 0 else 2)

    h_ref[0] = h0_ref[0, 0, 0]
    h_ref[1] = h0_ref[0, 0, 1]

    def carry_states(t, carry):
        for d, c in ((0, t), (1, nc - 1 - t)):
            rows = pl.ds(pl.multiple_of(c * q, q), q)
            h = h_ref[d]
            yo = _dot(c_ref[0, rows, :], h.astype(BF16)) * jnp.exp2(ex_ref[rows, (2 + d) * gw:(3 + d) * gw])
            yacc_ref[rows, :] = yacc_ref[rows, :] + yo
            h_ref[d] = h * dec_ref[d, c, 0:1, :] + s_ref[d, c]
        return carry

    lax.fori_loop(0, nc, carry_states, 0, unroll=4 if nc % 4 == 0 else 2)
    hout_ref[0, 0, 0] = h_ref[0]
    hout_ref[0, 0, 1] = h_ref[1]

    def emit(c, carry):
        rows = pl.ds(pl.multiple_of(c * q, q), q)
        y_ref[0, rows, :] = yacc_ref[rows, :].astype(y_ref.dtype)
        return carry

    lax.fori_loop(0, nc, emit, 0)


def _ssd_call(wide, pieces, crow, seq_block, sel, dsk, h0, name):
    xbc = wide
    b, n, _ = wide.shape
    nc = n // CHUNK
    g = SSM_GROUPS
    x_col0 = D_INNER // GROUP_W
    b_col0 = 2 * D_INNER // D_STATE
    c_col0 = b_col0 + g
    return pl.pallas_call(
        functools.partial(_ssd_kernel, nc),
        out_shape=[jax.ShapeDtypeStruct((b, n, D_INNER), BF16),
                   jax.ShapeDtypeStruct((b, g, 2, D_STATE, GROUP_W), F32)],
        grid=(b, g),
        in_specs=[pl.BlockSpec((1, n, GROUP_W), lambda bi, gi: (bi, 0, x_col0 + gi)),
                  pl.BlockSpec((1, n, D_STATE), lambda bi, gi: (bi, 0, b_col0 + gi)),
                  pl.BlockSpec((1, n, D_STATE), lambda bi, gi: (bi, 0, c_col0 + gi)),
                  pl.BlockSpec((1, n, LANES), lambda bi, gi: (bi, seq_block, gi)),
                  pl.BlockSpec((1, nc, 2 * HEADS_PER_GROUP, CHUNK), lambda bi, gi: (bi, seq_block, gi, 0)),
                  pl.BlockSpec((LANES, EXP_W), lambda bi, gi: (0, 0)),
                  pl.BlockSpec((2, GROUP_W), lambda bi, gi: (0, gi)),
                  pl.BlockSpec((1, 1, 2, D_STATE, GROUP_W), lambda bi, gi: (bi, gi, 0, 0, 0))],
        out_specs=[pl.BlockSpec((1, n, GROUP_W), lambda bi, gi: (bi, 0, gi)),
                   pl.BlockSpec((1, 1, 2, D_STATE, GROUP_W), lambda bi, gi: (bi, gi, 0, 0, 0))],
        scratch_shapes=[pltpu.VMEM((n, GROUP_W), F32),
                        pltpu.VMEM((2, nc, D_STATE, GROUP_W), F32),
                        pltpu.VMEM((n, EXP_W), F32),
                        pltpu.VMEM((nc, CHUNK, CHUNK), F32),
                        pltpu.VMEM((2, nc, 8, GROUP_W), F32),
                        pltpu.VMEM((2, D_STATE, GROUP_W), F32)],
        compiler_params=_params("parallel", "parallel"),
        name=name,
    )(xbc, xbc, xbc, pieces, crow, sel, dsk, h0)


VT_ROWS = V_DIM + 16

KEY_CHUNK = 512
ATTN_HEADS = 4


def _attn_kernel(q_ref, k_ref, vt_ref, o_ref, s_bufs, cmaxs, acc, m_run):
    nt = (((1,), (1,)), ((), ()))
    n_keys = k_ref.shape[1]
    blk = vt_ref.shape[3]
    per = KEY_CHUNK // blk
    n_full = n_keys // KEY_CHUNK
    tail = n_keys - n_full * KEY_CHUNK

    def produce(slot, c, n_rows=KEY_CHUNK):
        start = c * KEY_CHUNK
        rows = pl.ds(start if isinstance(c, int) else pl.multiple_of(start, KEY_CHUNK), n_rows)
        for h in range(ATTN_HEADS):
            st = lax.dot_general(k_ref[0, rows, h * HEAD_PAD:(h + 1) * HEAD_PAD],
                                 q_ref[0, :, h * HEAD_PAD:(h + 1) * HEAD_PAD], nt, preferred_element_type=F32)
            s_bufs[slot][h, 0:n_rows, :] = st
            cmaxs[slot][h, 0:1, :] = jnp.max(st, axis=0, keepdims=True)

    def consume(slot, c, n_rows=KEY_CHUNK):
        for h in range(ATTN_HEADS):
            m_old = m_run[h, 0:1, :]
            m_new = jnp.maximum(m_old, cmaxs[slot][h, 0:1, :])
            p = jnp.exp2(s_bufs[slot][h, 0:n_rows, :] - m_new).astype(BF16)
            vt = jnp.concatenate([vt_ref[0, per * c + j, h * VT_ROWS:(h + 1) * VT_ROWS, :]
                                  for j in range(n_rows // blk)], axis=1)
            acc[h] = acc[h] * jnp.exp2(m_old - m_new) + _dot(vt, p)
            m_run[h, 0:1, :] = m_new

    m_run[...] = jnp.full(m_run.shape, -jnp.inf, F32)
    acc[...] = jnp.zeros(acc.shape, F32)
    n_chunks = n_full + (1 if tail else 0)
    size = lambda c: tail if c == n_full else KEY_CHUNK
    produce(0, 0)

    def body(j, carry):
        produce(1, 2 * j + 1)
        consume(0, 2 * j)
        produce(0, 2 * j + 2)
        consume(1, 2 * j + 1)
        return carry

    trips = (n_full - 1) // 2
    lax.fori_loop(0, trips, body, 0)
    for c in range(2 * trips, n_chunks):
        slot = c % 2
        if c + 1 < n_chunks:
            produce(1 - slot, c + 1, size(c + 1))
        consume(slot, c, size(c))
    outs = [acc[h, 0:V_DIM, :] / acc[h, V_DIM:V_DIM + 1, :] for h in range(ATTN_HEADS)]
    o_ref[0] = jnp.concatenate(outs, axis=0).T.astype(o_ref.dtype)


def _attn_call(q, k, vt, n, tq, name):
    b = q.shape[0]
    s = k.shape[1]
    n_blk, _, blk = vt.shape[1:]
    nh = ATTN_HEADS
    return pl.pallas_call(
        _attn_kernel,
        out_shape=jax.ShapeDtypeStruct((b, n, MLA_HEADS * V_DIM), BF16),
        grid=(b, MLA_HEADS // nh, n // tq),
        in_specs=[pl.BlockSpec((1, tq, nh * HEAD_PAD), lambda bi, hp, i: (bi, i, hp)),
                  pl.BlockSpec((1, s, nh * HEAD_PAD), lambda bi, hp, i: (bi, 0, hp)),
                  pl.BlockSpec((1, n_blk, nh * VT_ROWS, blk), lambda bi, hp, i: (bi, 0, hp, 0))],
        out_specs=pl.BlockSpec((1, tq, nh * V_DIM), lambda bi, hp, i: (bi, i, hp)),
        scratch_shapes=[[pltpu.VMEM((nh, KEY_CHUNK, tq), F32)] * 2,
                        [pltpu.VMEM((nh, 8, tq), F32)] * 2,
                        pltpu.VMEM((nh, VT_ROWS, tq), F32),
                        pltpu.VMEM((nh, 8, tq), F32)],
        compiler_params=_params("parallel", "parallel", "parallel"),
        name=name,
    )(q, k, vt)


def _tail_kernel(ff_chunk, att_ref, y_ref, z_ref, ga_ref, gb_ref, x_ref, mod_ref, sg_ref, g2_ref,
                 wa_ref, wb_ref, wo_ref, w1_ref, w2_ref, o_ref):
    y = y_ref[0].astype(F32)
    z = z_ref[0].astype(F32)
    u = y * (z * _sigmoid(z))
    gw = D_INNER // SSM_GROUPS
    parts = []
    for g in range(SSM_GROUPS):
        ug = u[:, g * gw:(g + 1) * gw]
        parts.append(ug * lax.rsqrt(jnp.mean(ug * ug, axis=-1, keepdims=True) + EPS))
    un = (jnp.concatenate(parts, axis=1) * sg_ref[...]).astype(BF16)
    a = _dot(att_ref[0], wa_ref[...])
    bb = _dot(un, wb_ref[...])
    merged = _sigmoid(ga_ref[0].astype(F32)) * a + _sigmoid(gb_ref[0].astype(F32)) * bb
    o = _dot(merged.astype(BF16), wo_ref[...])
    x = x_ref[0] + mod_ref[0, 2:3, :] * o

    ms = jnp.mean(x * x, axis=-1, keepdims=True)
    xn = x * lax.rsqrt(ms + EPS) * g2_ref[...]
    h = (xn * (1.0 + mod_ref[0, 4:5, :]) + mod_ref[0, 3:4, :]).astype(BF16)
    acc = None
    for c in range(D_FF // ff_chunk):
        a = _dot(h, w1_ref[:, c * ff_chunk:(c + 1) * ff_chunk])
        a = jnp.maximum(a, 0.0)
        part = _dot((a * a).astype(BF16), w2_ref[c * ff_chunk:(c + 1) * ff_chunk, :])
        acc = part if acc is None else acc + part
    o_ref[0] = x + mod_ref[0, 5:6, :] * acc


def _tail_call(att, y, wide, x, mod, sg, g2, wa, wb, wo, w1, w2, tm, name):
    b, n, d = x.shape
    ga_blk = (D_INNER + CONV_DIM) // d
    full = lambda a: pl.BlockSpec(a.shape, lambda bi, i: (0,) * a.ndim)
    resident = lambda a: pl.BlockSpec(a.shape, lambda bi, i: (0,) * a.ndim, pipeline_mode=pl.Buffered(1))
    return pl.pallas_call(
        functools.partial(_tail_kernel, FF_CHUNK),
        out_shape=jax.ShapeDtypeStruct((b, n, d), F32),
        grid=(b, n // tm),
        in_specs=[pl.BlockSpec((1, tm, MLA_HEADS * V_DIM), lambda bi, i: (bi, i, 0)),
                  pl.BlockSpec((1, tm, D_INNER), lambda bi, i: (bi, i, 0)),
                  pl.BlockSpec((1, tm, D_INNER), lambda bi, i: (bi, i, 0)),
                  pl.BlockSpec((1, tm, d), lambda bi, i: (bi, i, ga_blk)),
                  pl.BlockSpec((1, tm, d), lambda bi, i: (bi, i, ga_blk + 1)),
                  pl.BlockSpec((1, tm, d), lambda bi, i: (bi, i, 0)),
                  pl.BlockSpec((1, 8, d), lambda bi, i: (bi, 0, 0)),
                  full(sg), full(g2), resident(wa), resident(wb), resident(wo), resident(w1), resident(w2)],
        out_specs=pl.BlockSpec((1, tm, d), lambda bi, i: (bi, i, 0)),
        compiler_params=_params("parallel", "parallel"),
        name=name,
    )(att, y, wide, wide, wide, x, mod, sg, g2, wa, wb, wo, w1, w2)


_ROPE_SWAP = np.concatenate([np.arange(8, 16), np.arange(0, 8), np.arange(24, 32), np.arange(16, 24)])
_DT_LANE_ORDER = np.array([d * SSM_HEADS + g * HEADS_PER_GROUP + r for g in range(SSM_GROUPS) for d in range(2)
                           for r in range(HEADS_PER_GROUP)])


def _rope_tables(n_lat):
    rows = n_lat // GRID_W
    row = jnp.repeat(jnp.arange(rows), GRID_W)
    col = jnp.tile(jnp.arange(GRID_W), rows)
    freqs = ROPE_BASE ** (-jnp.arange(ROPE_PAIRS, dtype=F32) / ROPE_PAIRS)
    ang = jnp.stack([row, col], axis=-1).astype(F32)[..., None] * freqs
    cos, sin = jnp.cos(ang), jnp.sin(ang)
    cos32 = jnp.stack([cos, cos], axis=2).reshape(n_lat, QK_ROPE)
    sin32 = jnp.stack([-sin, sin], axis=2).reshape(n_lat, QK_ROPE)
    zeros = jnp.zeros((n_lat, QK_NOPE), F32)
    return (jnp.concatenate([zeros, cos32, sin32], axis=1),
            jnp.concatenate([zeros, sin32, cos32], axis=1))


def _identity_rope_tables(n):
    zeros = jnp.zeros((n, QK_NOPE), F32)
    one = jnp.ones((n, QK_ROPE), F32)
    zero = jnp.zeros((n, QK_ROPE), F32)
    return jnp.concatenate([zeros, one, zero], axis=1), jnp.concatenate([zeros, zero, one], axis=1)


def _layout_params(w_in, w_uq, w_ukv, q_a_g, kv_a_g, qk_q_g, qk_k_g, dt_bias, a_log):
    d = w_in.shape[0]
    o_cq, o_ckv, o_kr = 0, Q_LORA, Q_LORA + KV_LORA
    o_z = o_kr + QK_ROPE
    o_xbc = o_z + D_INNER
    o_dt = o_xbc + CONV_DIM
    o_ga = o_dt + 2 * SSM_HEADS
    kr = w_in[:, o_kr:o_kr + QK_ROPE]
    kr_sw = kr[:, _ROPE_SWAP]
    z64 = jnp.zeros((d, LANES - 2 * QK_ROPE), F32)
    w_narrow = jnp.concatenate(
        [w_in[:, o_cq:o_kr], z64, kr, kr_sw, z64, kr_sw, kr, w_in[:, o_dt:o_ga][:, _DT_LANE_ORDER], z64],
        axis=1).astype(BF16)
    w_wide = jnp.concatenate([w_in[:, o_z:o_dt], w_in[:, o_ga:]], axis=1).astype(BF16)

    hq = w_uq.reshape(Q_LORA, MLA_HEADS, QK_NOPE + QK_ROPE)
    rope_q = hq[:, :, QK_NOPE:]
    wuq = jnp.concatenate([hq[:, :, :QK_NOPE], rope_q, rope_q[:, :, _ROPE_SWAP]], axis=2)
    wuq = wuq.reshape(Q_LORA, MLA_HEADS * HEAD_PAD).astype(BF16)
    hkv = w_ukv.reshape(KV_LORA, MLA_HEADS, QK_NOPE + V_DIM)
    wk = jnp.concatenate([hkv[:, :, :QK_NOPE], jnp.zeros((KV_LORA, MLA_HEADS, HEAD_PAD - QK_NOPE), F32)], axis=2)
    wukv = jnp.concatenate([wk.reshape(KV_LORA, MLA_HEADS * HEAD_PAD),
                            hkv[:, :, QK_NOPE:].reshape(KV_LORA, MLA_HEADS * V_DIM)], axis=1).astype(BF16)

    gq_r = qk_q_g[QK_NOPE:]
    gq = jnp.tile(jnp.concatenate([qk_q_g[:QK_NOPE], gq_r, gq_r[_ROPE_SWAP]]), MLA_HEADS)[None, :]
    gk = jnp.tile(jnp.concatenate([qk_k_g[:QK_NOPE], jnp.zeros((HEAD_PAD - QK_NOPE,), F32)]), MLA_HEADS)[None, :]
    gk_r = qk_k_g[QK_NOPE:]
    z64v = jnp.zeros((QK_NOPE,), F32)
    ga = jnp.concatenate([z64v, gk_r, gk_r[_ROPE_SWAP]])[None, :]
    gb = jnp.concatenate([z64v, gk_r[_ROPE_SWAP], gk_r])[None, :]
    lane_pad = jnp.zeros((LANES - 2 * SSM_HEADS,), F32)
    dtb = jnp.concatenate([dt_bias.reshape(-1)[_DT_LANE_ORDER], lane_pad])[None, :]
    alane = jnp.concatenate([a_log.astype(F32).reshape(-1)[_DT_LANE_ORDER], lane_pad])[None, :]

    lane = np.arange(MLA_HEADS * HEAD_PAD)
    head, off = lane // HEAD_PAD, lane % HEAD_PAD
    e = np.zeros((MLA_HEADS * HEAD_PAD, LANES), np.float32)
    e[lane[off < QK_NOPE], 2 * head[off < QK_NOPE]] = 1.0 / QK_NOPE
    rope = (off >= QK_NOPE) & (off < QK_NOPE + QK_ROPE)
    e[lane[rope], 2 * head[rope] + 1] = 1.0 / QK_ROPE
    et = np.zeros((LANES, MLA_HEADS * HEAD_PAD), np.float32)
    et[2 * head[off < QK_NOPE], lane[off < QK_NOPE]] = 1.0
    et[2 * head[off >= QK_NOPE] + 1, lane[off >= QK_NOPE]] = 1.0
    et = np.concatenate([et, et], axis=0)
    consts = (q_a_g[None, :], kv_a_g[None, :], wuq, wukv, jnp.asarray(e, BF16), jnp.asarray(et, BF16),
              gq, gk, ga, gb, dtb, alane, _piece_permutation())
    return w_narrow, w_wide, consts


def _expansion_selector():
    r = HEADS_PER_GROUP
    sel = np.zeros((LANES, EXP_W), np.float32)
    for piece in range(3):
        for d in range(2):
            for h in range(r):
                row_dt = piece * 4 * r + d * r + h
                row_cum = piece * 4 * r + 2 * r + d * r + h
                sel[row_dt, d * GROUP_W + h * SSM_HEAD_DIM:d * GROUP_W + (h + 1) * SSM_HEAD_DIM] = 1.0
                base = 2 * GROUP_W + d * GROUP_W + h * SSM_HEAD_DIM
                sel[row_cum, base:base + SSM_HEAD_DIM] = 1.0
    return jnp.asarray(sel, BF16)


def kernel(x, c, ctx, c_ctx, norm1_g, norm2_g, w_mod, b_mod, w_in, q_a_g, w_uq, kv_a_g, w_ukv, qk_q_g, qk_k_g,
           conv_w, conv_b, dt_bias, a_log, d_skip, ssm_norm_g, w_proj_a, w_proj_b, w_out, w_mlp1, w_mlp2):
    assert w_mod.shape[0] == 1, "single-layer block"
    b, n_lat, d = x.shape
    n_ctx = ctx.shape[1]

    cvec = jnp.concatenate([c, c_ctx[None, :], jnp.zeros((8 - b - 1, d), F32)], axis=0)
    mod = _mod_call(cvec, w_mod[0], b_mod[0][None, :]).reshape(8, N_MOD, d)
    mod = jnp.concatenate([mod, jnp.zeros((8, 8 - N_MOD, d), F32)], axis=1)
    mod_x, mod_c = mod[:b], mod[b:b + 1]

    w_narrow, w_wide, consts = _layout_params(w_in[0], w_uq[0], w_ukv[0], q_a_g[0], kv_a_g[0], qk_q_g[0],
                                              qk_k_g[0], dt_bias[0], a_log[0])
    g1 = norm1_g[0][None, :]
    conv_wl, conv_bl = conv_w[0], conv_b[0][None, :]
    dsk = jnp.repeat(d_skip[0], SSM_HEAD_DIM, axis=1)
    sel = _expansion_selector()

    narrow_c, wide_c = _front_call(ctx, g1, mod_c, w_narrow, w_wide, conv_wl, conv_bl, n_ctx, True, "front_ctx")
    narrow_x, wide_x = _front_call(x, g1, mod_x, w_narrow, w_wide, conv_wl, conv_bl, FRONT_TM, False, "front")

    t1c, t2c = _identity_rope_tables(n_ctx)
    t1x, t2x = _rope_tables(n_lat)
    q_all, k_all, vt_all, crow, pieces = _mla_call(narrow_c, narrow_x, jnp.concatenate([t1x, t1c], axis=0),
                                                   jnp.concatenate([t2x, t2c], axis=0), consts, "mla")

    h_zero = jnp.zeros((b, SSM_GROUPS, 2, D_STATE, GROUP_W), F32)
    _, h_ctx = _ssd_call(wide_c, pieces, crow, n_lat // n_ctx, sel, dsk, h_zero, "ssd_ctx")
    y_x, _ = _ssd_call(wide_x, pieces, crow, 0, sel, dsk, h_ctx, "ssd")

    att = _attn_call(q_all, k_all, vt_all, n_lat, ATTN_TQ, "attn")

    return _tail_call(att, y_x, wide_x, x, mod_x, ssm_norm_g[0][None, :], norm2_g[0][None, :],
                      w_proj_a[0].astype(BF16), w_proj_b[0].astype(BF16), w_out[0].astype(BF16),
                      w_mlp1[0].astype(BF16), w_mlp2[0].astype(BF16), TAIL_TM, "tail")
```

```python
import functools

import numpy as np
import jax
import jax.numpy as jnp
from jax import lax
from jax.experimental import pallas as pl
from jax.experimental.pallas import tpu as pltpu

F32 = jnp.float32
BF16 = jnp.bfloat16

D_MODEL = 1024
GRID_W = 64
MLA_HEADS = 16
QK_NOPE = 64
QK_ROPE = 32
V_DIM = 64
Q_LORA = 256
KV_LORA = 128
ROPE_PAIRS = QK_ROPE // 4
ROPE_BASE = 10000.0
ATTN_SCALE = (QK_NOPE + QK_ROPE) ** -0.5
LOG2_E = 1.4426950408889634
D_INNER = 2 * D_MODEL
SSM_HEAD_DIM = 64
SSM_HEADS = D_INNER // SSM_HEAD_DIM
SSM_GROUPS = 8
HEADS_PER_GROUP = SSM_HEADS // SSM_GROUPS
D_STATE = 128
D_CONV = 5
CHUNK = 128
CONV_DIM = D_INNER + 2 * SSM_GROUPS * D_STATE
D_FF = 4 * D_MODEL
N_MOD = 6
EPS = 1e-6

LANES = 128
HEAD_PAD = 128
GROUP_W = HEADS_PER_GROUP * SSM_HEAD_DIM
NARROW_ROPE_A = Q_LORA + KV_LORA
NARROW_ROPE_B = NARROW_ROPE_A + LANES
NARROW_DT = NARROW_ROPE_B + LANES
NARROW_W = NARROW_DT + LANES
WIDE_W = D_INNER + CONV_DIM + 2 * D_MODEL
VMEM_LIMIT = 56 * 1024 * 1024
FRONT_TM = 256
TAIL_TM = 512
ATTN_TQ = 1024
MOD_TN = 1024
FF_CHUNK = 1024


def _sigmoid(x):
    return 1.0 / (1.0 + jnp.exp(-x))


def _split3(x):
    hi = x.astype(BF16)
    r1 = x - hi.astype(F32)
    mid = r1.astype(BF16)
    lo = (r1 - mid.astype(F32)).astype(BF16)
    return hi, mid, lo


def _dot(a, b):
    return jnp.dot(a, b, preferred_element_type=F32)


def _params(*sem):
    return pltpu.CompilerParams(dimension_semantics=sem, vmem_limit_bytes=VMEM_LIMIT)


def _mod_kernel(c_ref, w_ref, b_ref, o_ref):
    c = c_ref[...]
    a = c * _sigmoid(c)
    w = w_ref[...]
    a_hi = a.astype(BF16)
    a_lo = (a - a_hi.astype(F32)).astype(BF16)
    w_hi = w.astype(BF16)
    w_lo = (w - w_hi.astype(F32)).astype(BF16)
    o_ref[...] = _dot(a_hi, w_hi) + _dot(a_hi, w_lo) + _dot(a_lo, w_hi) + b_ref[...]


def _mod_call(cvec, w_mod, b_mod):
    rows, d = cvec.shape
    n_out = w_mod.shape[1]
    tn = MOD_TN
    return pl.pallas_call(
        _mod_kernel,
        out_shape=jax.ShapeDtypeStruct((rows, n_out), F32),
        grid=(n_out // tn,),
        in_specs=[pl.BlockSpec((rows, d), lambda j: (0, 0)),
                  pl.BlockSpec((d, tn), lambda j: (0, j)),
                  pl.BlockSpec((1, tn), lambda j: (0, j))],
        out_specs=pl.BlockSpec((rows, tn), lambda j: (0, j)),
        compiler_params=_params("parallel"),
        name="mod",
    )(cvec, w_mod, b_mod)


FRONT_HALO = 16
FRONT_COLS = 1024


def _front_kernel(x_ref, xp_ref, xn_ref, g_ref, mod_ref, wn_ref, ww_ref, cw_ref, cb_ref, narrow_ref, wide_ref, pre_ref):
    i = pl.program_id(1)
    last = pl.num_programs(1) - 1
    tm = x_ref.shape[1]
    g = g_ref[...]
    shift = mod_ref[0, 0:1, :]
    scale = mod_ref[0, 1:2, :]

    def normmod(xv):
        ms = jnp.mean(xv * xv, axis=-1, keepdims=True)
        return xv * lax.rsqrt(ms + EPS) * g * (1.0 + scale) + shift

    hc = normmod(x_ref[0]).astype(BF16)
    hp = (normmod(xp_ref[0]) * (i > 0).astype(F32)).astype(BF16)
    hn = (normmod(xn_ref[0]) * (i < last).astype(F32)).astype(BF16)
    h_ext = jnp.concatenate([hp, hc, hn], axis=0)
    narrow_ref[0] = _dot(hc, wn_ref[...])
    pad = (D_CONV - 1) // 2
    n_chunks = WIDE_W // FRONT_COLS
    is_conv = [D_INNER <= j * FRONT_COLS < D_INNER + CONV_DIM for j in range(n_chunks)]
    conv_js = [j for j in range(n_chunks) if is_conv[j]]
    plain_js = [j for j in range(n_chunks) if not is_conv[j]]
    order = [j for pair in zip(conv_js, plain_js) for j in pair] + conv_js[len(plain_js):] + plain_js[len(conv_js):]
    for j in order:
        lo = j * FRONT_COLS
        cols = slice(lo, lo + FRONT_COLS)
        if D_INNER <= lo < D_INNER + CONV_DIM:
            slot = j % 2
            pre_ref[slot] = _dot(h_ext, ww_ref[:, cols])
            cc = slice(lo - D_INNER, lo - D_INNER + FRONT_COLS)
            acc = cb_ref[:, cc] + cw_ref[0:1, cc] * pre_ref[slot, pl.ds(FRONT_HALO - pad, tm), :]
            for k in range(1, D_CONV):
                acc = acc + cw_ref[k:k + 1, cc] * pre_ref[slot, pl.ds(FRONT_HALO - pad + k, tm), :]
            wide_ref[0, :, cols] = (acc * _sigmoid(acc)).astype(BF16)
        else:
            wide_ref[0, :, cols] = _dot(hc, ww_ref[:, cols]).astype(BF16)


def _front_call(x, g, mod, w_narrow, w_wide, conv_w, conv_b, tm, shared_mod, name):
    b, n, d = x.shape
    per = tm // FRONT_HALO
    n_halo = n // FRONT_HALO
    mod_map = (lambda bi, i: (0, 0, 0)) if shared_mod else (lambda bi, i: (bi, 0, 0))
    resident = lambda a: pl.BlockSpec(a.shape, lambda bi, i: (0,) * a.ndim, pipeline_mode=pl.Buffered(1))
    return pl.pallas_call(
        _front_kernel,
        out_shape=[jax.ShapeDtypeStruct((b, n, NARROW_W), F32), jax.ShapeDtypeStruct((b, n, WIDE_W), BF16)],
        grid=(b, n // tm),
        in_specs=[pl.BlockSpec((1, tm, d), lambda bi, i: (bi, i, 0)),
                  pl.BlockSpec((1, FRONT_HALO, d), lambda bi, i: (bi, jnp.maximum(i * per - 1, 0), 0)),
                  pl.BlockSpec((1, FRONT_HALO, d), lambda bi, i: (bi, jnp.minimum((i + 1) * per, n_halo - 1), 0)),
                  pl.BlockSpec((1, d), lambda bi, i: (0, 0)),
                  pl.BlockSpec((1, 8, d), mod_map),
                  resident(w_narrow), resident(w_wide), resident(conv_w), resident(conv_b)],
        out_specs=[pl.BlockSpec((1, tm, NARROW_W), lambda bi, i: (bi, i, 0)),
                   pl.BlockSpec((1, tm, WIDE_W), lambda bi, i: (bi, i, 0))],
        scratch_shapes=[pltpu.VMEM((2, tm + 2 * FRONT_HALO, FRONT_COLS), F32)],
        compiler_params=_params("parallel", "parallel"),
        name=name,
    )(x, x, x, g, mod, w_narrow, w_wide, conv_w, conv_b)


def _segment_rsqrt(t, e_ref, et_ref):
    ms = _dot((t * t).astype(BF16), e_ref[...])
    r = lax.rsqrt(ms + EPS)
    hi = r.astype(BF16)
    lo = (r - hi.astype(F32)).astype(BF16)
    return _dot(jnp.concatenate([hi, lo], axis=1), et_ref[...])


def _mla_kernel(sc_ref, sx_ref, t1_ref, t2_ref, qag_ref, kvag_ref, wuq_ref, wukv_ref, e_ref, et_ref,
                gq_ref, gk_ref, ga_ref, gb_ref, dtb_ref, alane_ref, perm_ref, q_ref, k_ref, vt_ref, crow_ref, pg_ref):
    is_ctx = pl.program_id(1) == pl.num_programs(1) - 1
    s = jnp.where(is_ctx, sc_ref[0], sx_ref[0])
    t1 = t1_ref[...]
    t2 = t2_ref[...]
    lane = lax.broadcasted_iota(jnp.int32, (1, LANES), 1)
    n_heads = MLA_HEADS
    k_w = n_heads * HEAD_PAD

    ckv = s[:, Q_LORA:Q_LORA + KV_LORA]
    ckvn = ckv * lax.rsqrt(jnp.mean(ckv * ckv, axis=-1, keepdims=True) + EPS) * kvag_ref[...]
    kv = _dot(ckvn.astype(BF16), wukv_ref[...])
    kn = kv[:, :k_w]
    vt = kv[:, k_w:].T
    ones = jnp.ones((VT_ROWS - V_DIM, vt.shape[1]), F32)
    vt_ref[0, 0] = jnp.concatenate(
        [blk for h in range(n_heads) for blk in (vt[h * V_DIM:(h + 1) * V_DIM], ones)], axis=0).astype(BF16)
    kn = kn * _segment_rsqrt(kn, e_ref, et_ref) * gk_ref[...]
    g1 = s[:, NARROW_ROPE_A:NARROW_ROPE_B]
    g2 = s[:, NARROW_ROPE_B:NARROW_DT]
    rope_lanes = (lane >= QK_NOPE) & (lane < QK_NOPE + QK_ROPE)
    kr_ms = jnp.sum(jnp.where(rope_lanes, g1 * g1, 0.0), axis=-1, keepdims=True) * (1.0 / QK_ROPE)
    krot = lax.rsqrt(kr_ms + EPS) * (g1 * ga_ref[...] * t1 + g2 * gb_ref[...] * t2)
    k_ref[0] = (kn + jnp.concatenate([krot] * n_heads, axis=1)).astype(BF16)

    cq = s[:, :Q_LORA]
    cqn = cq * lax.rsqrt(jnp.mean(cq * cq, axis=-1, keepdims=True) + EPS) * qag_ref[...]
    q = _dot(cqn.astype(BF16), wuq_ref[...])
    tq = t1 + jnp.where(lane < QK_NOPE, 1.0, 0.0)
    q = q * _segment_rsqrt(q, e_ref, et_ref) * gq_ref[...] * jnp.concatenate([tq] * n_heads, axis=1)
    q_ref[0] = (q * (ATTN_SCALE * LOG2_E)).astype(BF16)

    xdt = s[:, NARROW_DT:NARROW_W] + dtb_ref[...]
    dt = jnp.maximum(xdt, 0.0) + jnp.log1p(jnp.exp(-jnp.abs(xdt)))
    dt = jnp.where(lane < 2 * SSM_HEADS, dt, 0.0)
    da = dt * (-LOG2_E * jnp.exp(alane_ref[...]))
    ii = lax.broadcasted_iota(jnp.int32, (CHUNK, CHUNK), 0)
    jj = lax.broadcasted_iota(jnp.int32, (CHUNK, CHUNK), 1)
    tri_f = (jj <= ii).astype(BF16)
    tri_b = (jj >= ii).astype(BF16)
    fwd_lane = (lane % (2 * HEADS_PER_GROUP)) < HEADS_PER_GROUP
    cums = []
    for ch in range(s.shape[0] // CHUNK):
        pieces = _split3(da[ch * CHUNK:(ch + 1) * CHUNK])
        cf = sum(_dot(tri_f, p) for p in pieces)
        cb = sum(_dot(tri_b, p) for p in pieces)
        cums.append(jnp.where(fwd_lane, cf, cb))
        crow_ref[0, ch] = cums[-1].T[:2 * SSM_HEADS]
    cum = jnp.concatenate(cums, axis=0)
    d3 = _split3(dt)
    c3 = _split3(cum)
    dc3 = jnp.concatenate([d3[0], c3[0], d3[1], c3[1], d3[2], c3[2]], axis=1)
    pg_ref[0] = _dot(dc3, perm_ref[...]).astype(BF16)


def _piece_permutation():
    per = 2 * HEADS_PER_GROUP
    perm = np.zeros((6 * LANES, SSM_GROUPS * LANES), np.float32)
    for piece in range(3):
        for kind in range(2):
            for g in range(SSM_GROUPS):
                for j in range(per):
                    perm[(piece * 2 + kind) * LANES + g * per + j, g * LANES + piece * 2 * per + kind * per + j] = 1.0
    return jnp.asarray(perm, BF16)


def _mla_call(narrow_c, narrow_x, t1, t2, consts, name):
    b, n_ctx, _ = narrow_c.shape
    n_lat = narrow_x.shape[1]
    tm = n_ctx
    s_all = n_ctx + n_lat
    last_latent = n_lat // tm - 1
    k_w = MLA_HEADS * HEAD_PAD
    full = lambda a: pl.BlockSpec(a.shape, lambda bi, i: (0,) * a.ndim)
    latent_tile = lambda bi, i: (bi, jnp.minimum(i, last_latent), 0)
    return pl.pallas_call(
        _mla_kernel,
        out_shape=[jax.ShapeDtypeStruct((b, s_all, k_w), BF16),
                   jax.ShapeDtypeStruct((b, s_all, k_w), BF16),
                   jax.ShapeDtypeStruct((b, s_all // tm, MLA_HEADS * VT_ROWS, tm), BF16),
                   jax.ShapeDtypeStruct((b, s_all // CHUNK, 2 * SSM_HEADS, CHUNK), F32),
                   jax.ShapeDtypeStruct((b, s_all, SSM_GROUPS * LANES), BF16)],
        grid=(b, s_all // tm),
        in_specs=[pl.BlockSpec((1, tm, NARROW_W), lambda bi, i: (bi, 0, 0)),
                  pl.BlockSpec((1, tm, NARROW_W), latent_tile),
                  pl.BlockSpec((tm, LANES), lambda bi, i: (i, 0)),
                  pl.BlockSpec((tm, LANES), lambda bi, i: (i, 0)),
                  *[full(a) for a in consts]],
        out_specs=[pl.BlockSpec((1, tm, k_w), lambda bi, i: (bi, i, 0)),
                   pl.BlockSpec((1, tm, k_w), lambda bi, i: (bi, i, 0)),
                   pl.BlockSpec((1, 1, MLA_HEADS * VT_ROWS, tm), lambda bi, i: (bi, i, 0, 0)),
                   pl.BlockSpec((1, tm // CHUNK, 2 * SSM_HEADS, CHUNK), lambda bi, i: (bi, i, 0, 0)),
                   pl.BlockSpec((1, tm, SSM_GROUPS * LANES), lambda bi, i: (bi, i, 0))],
        compiler_params=_params("parallel", "arbitrary"),
        name=name,
    )(narrow_c, narrow_x, t1, t2, *consts)


EXP_W = 4 * GROUP_W


def _ssd_kernel(nc, xs_ref, b_ref, c_ref, p_ref, crow_ref, sel_ref, dsk_ref, h0_ref,
                y_ref, hout_ref, yacc_ref, s_ref, ex_ref, cb_ref, dec_ref, h_ref):
    q = CHUNK
    r_heads = HEADS_PER_GROUP
    wide_l = r_heads * q
    ii_w = lax.broadcasted_iota(jnp.int32, (q, wide_l), 0)
    jj_w = lax.broadcasted_iota(jnp.int32, (q, wide_l), 1) % q
    keep_f = jj_w <= ii_w
    keep_b = jj_w >= ii_w
    lane_head = lax.broadcasted_iota(jnp.int32, (1, GROUP_W), 1) // SSM_HEAD_DIM
    low_half = lax.broadcasted_iota(jnp.int32, (1, LANES), 1) < SSM_HEAD_DIM
    dsk = dsk_ref[0:1, :] + dsk_ref[1:2, :]
    gw = GROUP_W

    def per_head_128(c64):
        out = []
        for pair in range(r_heads // 2):
            v = c64[:, pair * LANES:(pair + 1) * LANES]
            sw = pltpu.roll(v, SSM_HEAD_DIM, axis=1)
            out += [jnp.where(low_half, v, sw), jnp.where(low_half, sw, v)]
        return jnp.concatenate(out, axis=1)

    def block_diag(xdt):
        xb = xdt.astype(BF16)
        return jnp.concatenate([jnp.where(lane_head == r, xb, jnp.zeros_like(xb)) for r in range(r_heads)], axis=0)

    def expand(c, carry):
        rows = pl.ds(pl.multiple_of(c * q, q), q)
        ex_ref[rows, :] = _dot(p_ref[0, rows, :], sel_ref[...])
        cb_ref[c] = lax.dot_general(c_ref[0, rows, :], b_ref[0, rows, :], (((1,), (1,)), ((), ())),
                                    preferred_element_type=F32)
        return carry

    lax.fori_loop(0, nc, expand, 0, unroll=4 if nc % 4 == 0 else 2)

    def local(c, carry):
        rows = pl.ds(pl.multiple_of(c * q, q), q)
        x = xs_ref[0, rows, :].astype(F32)
        bm = b_ref[0, rows, :]
        dte_f, dte_b = ex_ref[rows, 0:gw], ex_ref[rows, gw:2 * gw]
        c64_f, c64_b = ex_ref[rows, 2 * gw:3 * gw], ex_ref[rows, 3 * gw:4 * gw]
        cw_f, cw_b = per_head_128(c64_f), per_head_128(c64_b)
        crow = crow_ref[0, c]
        flat_f = jnp.concatenate([crow[r:r + 1, :] for r in range(r_heads)], axis=1)
        flat_b = jnp.concatenate([crow[r_heads + r:r_heads + r + 1, :] for r in range(r_heads)], axis=1)
        cb4 = jnp.concatenate([cb_ref[c]] * r_heads, axis=1)
        l_f = (jnp.where(keep_f, jnp.exp2(cw_f - flat_f), 0.0) * cb4).astype(BF16)
        l_b = (jnp.where(keep_b, jnp.exp2(cw_b - flat_b), 0.0) * cb4).astype(BF16)
        xdt_f = x * dte_f
        xdt_b = x * dte_b
        y = _dot(jnp.concatenate([l_f, l_b], axis=1),
                 jnp.concatenate([block_diag(xdt_f), block_diag(xdt_b)], axis=0))
        yacc_ref[rows, :] = y + dsk * x
        edge_f = c64_f[q - 1:q, :]
        edge_b = c64_b[0:1, :]
        xw = jnp.concatenate([(xdt_f * jnp.exp2(edge_f - c64_f)).astype(BF16),
                              (xdt_b * jnp.exp2(edge_b - c64_b)).astype(BF16)], axis=1)
        st = lax.dot_general(bm, xw, (((0,), (0,)), ((), ())), preferred_element_type=F32)
        s_ref[0, c] = st[:, :gw]
        s_ref[1, c] = st[:, gw:]
        dec_ref[0, c] = jnp.broadcast_to(jnp.exp2(edge_f), (8, gw))
        dec_ref[1, c] = jnp.broadcast_to(jnp.exp2(edge_b), (8, gw))
        return carry

    lax.fori_loop(0, nc, local, 0, unroll=4 if nc % 4 == 0 else 2)

    h_ref[0] = h0_ref[0, 0, 0]
    h_ref[1] = h0_ref[0, 0, 1]

    def carry_states(t, carry):
        for d, c in ((0, t), (1, nc - 1 - t)):
            rows = pl.ds(pl.multiple_of(c * q, q), q)
            h = h_ref[d]
            yo = _dot(c_ref[0, rows, :], h.astype(BF16)) * jnp.exp2(ex_ref[rows, (2 + d) * gw:(3 + d) * gw])
            yacc_ref[rows, :] = yacc_ref[rows, :] + yo
            h_ref[d] = h * dec_ref[d, c, 0:1, :] + s_ref[d, c]
        return carry

    lax.fori_loop(0, nc, carry_states, 0, unroll=4 if nc % 4 == 0 else 2)
    hout_ref[0, 0, 0] = h_ref[0]
    hout_ref[0, 0, 1] = h_ref[1]

    def emit(c, carry):
        rows = pl.ds(pl.multiple_of(c * q, q), q)
        y_ref[0, rows, :] = yacc_ref[rows, :].astype(y_ref.dtype)
        return carry

    lax.fori_loop(0, nc, emit, 0)


def _ssd_call(wide, pieces, crow, seq_block, sel, dsk, h0, name):
    xbc = wide
    b, n, _ = wide.shape
    nc = n // CHUNK
    g = SSM_GROUPS
    x_col0 = D_INNER // GROUP_W
    b_col0 = 2 * D_INNER // D_STATE
    c_col0 = b_col0 + g
    return pl.pallas_call(
        functools.partial(_ssd_kernel, nc),
        out_shape=[jax.ShapeDtypeStruct((b, n, D_INNER), BF16),
                   jax.ShapeDtypeStruct((b, g, 2, D_STATE, GROUP_W), F32)],
        grid=(b, g),
        in_specs=[pl.BlockSpec((1, n, GROUP_W), lambda bi, gi: (bi, 0, x_col0 + gi)),
                  pl.BlockSpec((1, n, D_STATE), lambda bi, gi: (bi, 0, b_col0 + gi)),
                  pl.BlockSpec((1, n, D_STATE), lambda bi, gi: (bi, 0, c_col0 + gi)),
                  pl.BlockSpec((1, n, LANES), lambda bi, gi: (bi, seq_block, gi)),
                  pl.BlockSpec((1, nc, 2 * HEADS_PER_GROUP, CHUNK), lambda bi, gi: (bi, seq_block, gi, 0)),
                  pl.BlockSpec((LANES, EXP_W), lambda bi, gi: (0, 0)),
                  pl.BlockSpec((2, GROUP_W), lambda bi, gi: (0, gi)),
                  pl.BlockSpec((1, 1, 2, D_STATE, GROUP_W), lambda bi, gi: (bi, gi, 0, 0, 0))],
        out_specs=[pl.BlockSpec((1, n, GROUP_W), lambda bi, gi: (bi, 0, gi)),
                   pl.BlockSpec((1, 1, 2, D_STATE, GROUP_W), lambda bi, gi: (bi, gi, 0, 0, 0))],
        scratch_shapes=[pltpu.VMEM((n, GROUP_W), F32),
                        pltpu.VMEM((2, nc, D_STATE, GROUP_W), F32),
                        pltpu.VMEM((n, EXP_W), F32),
                        pltpu.VMEM((nc, CHUNK, CHUNK), F32),
                        pltpu.VMEM((2, nc, 8, GROUP_W), F32),
                        pltpu.VMEM((2, D_STATE, GROUP_W), F32)],
        compiler_params=_params("parallel", "parallel"),
        name=name,
    )(xbc, xbc, xbc, pieces, crow, sel, dsk, h0)


VT_ROWS = V_DIM + 16

KEY_CHUNK = 512
ATTN_HEADS = 4


def _attn_kernel(q_ref, k_ref, vt_ref, o_ref, s_bufs, cmaxs, acc, m_run):
    nt = (((1,), (1,)), ((), ()))
    n_keys = k_ref.shape[1]
    blk = vt_ref.shape[3]
    per = KEY_CHUNK // blk
    n_full = n_keys // KEY_CHUNK
    tail = n_keys - n_full * KEY_CHUNK

    def produce_head(h, slot, c, n_rows):
        start = c * KEY_CHUNK
        rows = pl.ds(start if isinstance(c, int) else pl.multiple_of(start, KEY_CHUNK), n_rows)
        st = lax.dot_general(k_ref[0, rows, h * HEAD_PAD:(h + 1) * HEAD_PAD],
                             q_ref[0, :, h * HEAD_PAD:(h + 1) * HEAD_PAD], nt, preferred_element_type=F32)
        s_bufs[slot][h, 0:n_rows, :] = st
        cmaxs[slot][h, 0:1, :] = jnp.max(st, axis=0, keepdims=True)

    def consume_head(h, slot, c, n_rows):
        m_old = m_run[h, 0:1, :]
        m_new = jnp.maximum(m_old, cmaxs[slot][h, 0:1, :])
        p = jnp.exp2(s_bufs[slot][h, 0:n_rows, :] - m_new).astype(BF16)
        vt = jnp.concatenate([vt_ref[0, per * c + j, h * VT_ROWS:(h + 1) * VT_ROWS, :]
                              for j in range(n_rows // blk)], axis=1)
        acc[h] = acc[h] * jnp.exp2(m_old - m_new) + _dot(vt, p)
        m_run[h, 0:1, :] = m_new

    def produce(slot, c, n_rows=KEY_CHUNK):
        for h in range(ATTN_HEADS):
            produce_head(h, slot, c, n_rows)

    def consume(slot, c, n_rows=KEY_CHUNK):
        for h in range(ATTN_HEADS):
            consume_head(h, slot, c, n_rows)

    def produce_consume(p_slot, p_c, c_slot, c_c, p_rows=KEY_CHUNK, c_rows=KEY_CHUNK):
        for h in range(ATTN_HEADS):
            produce_head(h, p_slot, p_c, p_rows)
            consume_head(h, c_slot, c_c, c_rows)

    m_run[...] = jnp.full(m_run.shape, -jnp.inf, F32)
    acc[...] = jnp.zeros(acc.shape, F32)
    n_chunks = n_full + (1 if tail else 0)
    size = lambda c: tail if c == n_full else KEY_CHUNK
    produce(0, 0)

    def body(j, carry):
        produce_consume(1, 2 * j + 1, 0, 2 * j)
        produce_consume(0, 2 * j + 2, 1, 2 * j + 1)
        return carry

    trips = (n_full - 1) // 2
    lax.fori_loop(0, trips, body, 0)
    for c in range(2 * trips, n_chunks):
        slot = c % 2
        if c + 1 < n_chunks:
            produce_consume(1 - slot, c + 1, slot, c, size(c + 1), size(c))
        else:
            consume(slot, c, size(c))
    outs = [acc[h, 0:V_DIM, :] / acc[h, V_DIM:V_DIM + 1, :] for h in range(ATTN_HEADS)]
    o_ref[0] = jnp.concatenate(outs, axis=0).T.astype(o_ref.dtype)


def _attn_call(q, k, vt, n, tq, name):
    b = q.shape[0]
    s = k.shape[1]
    n_blk, _, blk = vt.shape[1:]
    nh = ATTN_HEADS
    return pl.pallas_call(
        _attn_kernel,
        out_shape=jax.ShapeDtypeStruct((b, n, MLA_HEADS * V_DIM), BF16),
        grid=(b, MLA_HEADS // nh, n // tq),
        in_specs=[pl.BlockSpec((1, tq, nh * HEAD_PAD), lambda bi, hp, i: (bi, i, hp)),
                  pl.BlockSpec((1, s, nh * HEAD_PAD), lambda bi, hp, i: (bi, 0, hp)),
                  pl.BlockSpec((1, n_blk, nh * VT_ROWS, blk), lambda bi, hp, i: (bi, 0, hp, 0))],
        out_specs=pl.BlockSpec((1, tq, nh * V_DIM), lambda bi, hp, i: (bi, i, hp)),
        scratch_shapes=[[pltpu.VMEM((nh, KEY_CHUNK, tq), F32)] * 2,
                        [pltpu.VMEM((nh, 8, tq), F32)] * 2,
                        pltpu.VMEM((nh, VT_ROWS, tq), F32),
                        pltpu.VMEM((nh, 8, tq), F32)],
        compiler_params=_params("parallel", "parallel", "parallel"),
        name=name,
    )(q, k, vt)


def _tail_kernel(ff_chunk, att_ref, y_ref, z_ref, ga_ref, gb_ref, x_ref, mod_ref, sg_ref, g2_ref,
                 wa_ref, wb_ref, wo_ref, w1_ref, w2_ref, o_ref):
    y = y_ref[0].astype(F32)
    z = z_ref[0].astype(F32)
    u = y * (z * _sigmoid(z))
    gw = D_INNER // SSM_GROUPS
    parts = []
    for g in range(SSM_GROUPS):
        ug = u[:, g * gw:(g + 1) * gw]
        parts.append(ug * lax.rsqrt(jnp.mean(ug * ug, axis=-1, keepdims=True) + EPS))
    un = (jnp.concatenate(parts, axis=1) * sg_ref[...]).astype(BF16)
    a = _dot(att_ref[0], wa_ref[...])
    bb = _dot(un, wb_ref[...])
    merged = _sigmoid(ga_ref[0].astype(F32)) * a + _sigmoid(gb_ref[0].astype(F32)) * bb
    o = _dot(merged.astype(BF16), wo_ref[...])
    x = x_ref[0] + mod_ref[0, 2:3, :] * o

    ms = jnp.mean(x * x, axis=-1, keepdims=True)
    xn = x * lax.rsqrt(ms + EPS) * g2_ref[...]
    h = (xn * (1.0 + mod_ref[0, 4:5, :]) + mod_ref[0, 3:4, :]).astype(BF16)
    acc = None
    for c in range(D_FF // ff_chunk):
        a = _dot(h, w1_ref[:, c * ff_chunk:(c + 1) * ff_chunk])
        a = jnp.maximum(a, 0.0)
        part = _dot((a * a).astype(BF16), w2_ref[c * ff_chunk:(c + 1) * ff_chunk, :])
        acc = part if acc is None else acc + part
    o_ref[0] = x + mod_ref[0, 5:6, :] * acc


def _tail_call(att, y, wide, x, mod, sg, g2, wa, wb, wo, w1, w2, tm, name):
    b, n, d = x.shape
    ga_blk = (D_INNER + CONV_DIM) // d
    full = lambda a: pl.BlockSpec(a.shape, lambda bi, i: (0,) * a.ndim)
    resident = lambda a: pl.BlockSpec(a.shape, lambda bi, i: (0,) * a.ndim, pipeline_mode=pl.Buffered(1))
    return pl.pallas_call(
        functools.partial(_tail_kernel, FF_CHUNK),
        out_shape=jax.ShapeDtypeStruct((b, n, d), F32),
        grid=(b, n // tm),
        in_specs=[pl.BlockSpec((1, tm, MLA_HEADS * V_DIM), lambda bi, i: (bi, i, 0)),
                  pl.BlockSpec((1, tm, D_INNER), lambda bi, i: (bi, i, 0)),
                  pl.BlockSpec((1, tm, D_INNER), lambda bi, i: (bi, i, 0)),
                  pl.BlockSpec((1, tm, d), lambda bi, i: (bi, i, ga_blk)),
                  pl.BlockSpec((1, tm, d), lambda bi, i: (bi, i, ga_blk + 1)),
                  pl.BlockSpec((1, tm, d), lambda bi, i: (bi, i, 0)),
                  pl.BlockSpec((1, 8, d), lambda bi, i: (bi, 0, 0)),
                  full(sg), full(g2), resident(wa), resident(wb), resident(wo), resident(w1), resident(w2)],
        out_specs=pl.BlockSpec((1, tm, d), lambda bi, i: (bi, i, 0)),
        compiler_params=_params("parallel", "parallel"),
        name=name,
    )(att, y, wide, wide, wide, x, mod, sg, g2, wa, wb, wo, w1, w2)


_ROPE_SWAP = np.concatenate([np.arange(8, 16), np.arange(0, 8), np.arange(24, 32), np.arange(16, 24)])
_DT_LANE_ORDER = np.array([d * SSM_HEADS + g * HEADS_PER_GROUP + r for g in range(SSM_GROUPS) for d in range(2)
                           for r in range(HEADS_PER_GROUP)])


def _rope_tables(n_lat):
    rows = n_lat // GRID_W
    row = jnp.repeat(jnp.arange(rows), GRID_W)
    col = jnp.tile(jnp.arange(GRID_W), rows)
    freqs = ROPE_BASE ** (-jnp.arange(ROPE_PAIRS, dtype=F32) / ROPE_PAIRS)
    ang = jnp.stack([row, col], axis=-1).astype(F32)[..., None] * freqs
    cos, sin = jnp.cos(ang), jnp.sin(ang)
    cos32 = jnp.stack([cos, cos], axis=2).reshape(n_lat, QK_ROPE)
    sin32 = jnp.stack([-sin, sin], axis=2).reshape(n_lat, QK_ROPE)
    zeros = jnp.zeros((n_lat, QK_NOPE), F32)
    return (jnp.concatenate([zeros, cos32, sin32], axis=1),
            jnp.concatenate([zeros, sin32, cos32], axis=1))


def _identity_rope_tables(n):
    zeros = jnp.zeros((n, QK_NOPE), F32)
    one = jnp.ones((n, QK_ROPE), F32)
    zero = jnp.zeros((n, QK_ROPE), F32)
    return jnp.concatenate([zeros, one, zero], axis=1), jnp.concatenate([zeros, zero, one], axis=1)


def _layout_params(w_in, w_uq, w_ukv, q_a_g, kv_a_g, qk_q_g, qk_k_g, dt_bias, a_log):
    d = w_in.shape[0]
    o_cq, o_ckv, o_kr = 0, Q_LORA, Q_LORA + KV_LORA
    o_z = o_kr + QK_ROPE
    o_xbc = o_z + D_INNER
    o_dt = o_xbc + CONV_DIM
    o_ga = o_dt + 2 * SSM_HEADS
    kr = w_in[:, o_kr:o_kr + QK_ROPE]
    kr_sw = kr[:, _ROPE_SWAP]
    z64 = jnp.zeros((d, LANES - 2 * QK_ROPE), F32)
    w_narrow = jnp.concatenate(
        [w_in[:, o_cq:o_kr], z64, kr, kr_sw, z64, kr_sw, kr, w_in[:, o_dt:o_ga][:, _DT_LANE_ORDER], z64],
        axis=1).astype(BF16)
    w_wide = jnp.concatenate([w_in[:, o_z:o_dt], w_in[:, o_ga:]], axis=1).astype(BF16)

    hq = w_uq.reshape(Q_LORA, MLA_HEADS, QK_NOPE + QK_ROPE)
    rope_q = hq[:, :, QK_NOPE:]
    wuq = jnp.concatenate([hq[:, :, :QK_NOPE], rope_q, rope_q[:, :, _ROPE_SWAP]], axis=2)
    wuq = wuq.reshape(Q_LORA, MLA_HEADS * HEAD_PAD).astype(BF16)
    hkv = w_ukv.reshape(KV_LORA, MLA_HEADS, QK_NOPE + V_DIM)
    wk = jnp.concatenate([hkv[:, :, :QK_NOPE], jnp.zeros((KV_LORA, MLA_HEADS, HEAD_PAD - QK_NOPE), F32)], axis=2)
    wukv = jnp.concatenate([wk.reshape(KV_LORA, MLA_HEADS * HEAD_PAD),
                            hkv[:, :, QK_NOPE:].reshape(KV_LORA, MLA_HEADS * V_DIM)], axis=1).astype(BF16)

    gq_r = qk_q_g[QK_NOPE:]
    gq = jnp.tile(jnp.concatenate([qk_q_g[:QK_NOPE], gq_r, gq_r[_ROPE_SWAP]]), MLA_HEADS)[None, :]
    gk = jnp.tile(jnp.concatenate([qk_k_g[:QK_NOPE], jnp.zeros((HEAD_PAD - QK_NOPE,), F32)]), MLA_HEADS)[None, :]
    gk_r = qk_k_g[QK_NOPE:]
    z64v = jnp.zeros((QK_NOPE,), F32)
    ga = jnp.concatenate([z64v, gk_r, gk_r[_ROPE_SWAP]])[None, :]
    gb = jnp.concatenate([z64v, gk_r[_ROPE_SWAP], gk_r])[None, :]
    lane_pad = jnp.zeros((LANES - 2 * SSM_HEADS,), F32)
    dtb = jnp.concatenate([dt_bias.reshape(-1)[_DT_LANE_ORDER], lane_pad])[None, :]
    alane = jnp.concatenate([a_log.astype(F32).reshape(-1)[_DT_LANE_ORDER], lane_pad])[None, :]

    lane = np.arange(MLA_HEADS * HEAD_PAD)
    head, off = lane // HEAD_PAD, lane % HEAD_PAD
    e = np.zeros((MLA_HEADS * HEAD_PAD, LANES), np.float32)
    e[lane[off < QK_NOPE], 2 * head[off < QK_NOPE]] = 1.0 / QK_NOPE
    rope = (off >= QK_NOPE) & (off < QK_NOPE + QK_ROPE)
    e[lane[rope], 2 * head[rope] + 1] = 1.0 / QK_ROPE
    et = np.zeros((LANES, MLA_HEADS * HEAD_PAD), np.float32)
    et[2 * head[off < QK_NOPE], lane[off < QK_NOPE]] = 1.0
    et[2 * head[off >= QK_NOPE] + 1, lane[off >= QK_NOPE]] = 1.0
    et = np.concatenate([et, et], axis=0)
    consts = (q_a_g[None, :], kv_a_g[None, :], wuq, wukv, jnp.asarray(e, BF16), jnp.asarray(et, BF16),
              gq, gk, ga, gb, dtb, alane, _piece_permutation())
    return w_narrow, w_wide, consts


def _expansion_selector():
    r = HEADS_PER_GROUP
    sel = np.zeros((LANES, EXP_W), np.float32)
    for piece in range(3):
        for d in range(2):
            for h in range(r):
                row_dt = piece * 4 * r + d * r + h
                row_cum = piece * 4 * r + 2 * r + d * r + h
                sel[row_dt, d * GROUP_W + h * SSM_HEAD_DIM:d * GROUP_W + (h + 1) * SSM_HEAD_DIM] = 1.0
                base = 2 * GROUP_W + d * GROUP_W + h * SSM_HEAD_DIM
                sel[row_cum, base:base + SSM_HEAD_DIM] = 1.0
    return jnp.asarray(sel, BF16)


def kernel(x, c, ctx, c_ctx, norm1_g, norm2_g, w_mod, b_mod, w_in, q_a_g, w_uq, kv_a_g, w_ukv, qk_q_g, qk_k_g,
           conv_w, conv_b, dt_bias, a_log, d_skip, ssm_norm_g, w_proj_a, w_proj_b, w_out, w_mlp1, w_mlp2):
    assert w_mod.shape[0] == 1, "single-layer block"
    b, n_lat, d = x.shape
    n_ctx = ctx.shape[1]

    cvec = jnp.concatenate([c, c_ctx[None, :], jnp.zeros((8 - b - 1, d), F32)], axis=0)
    mod = _mod_call(cvec, w_mod[0], b_mod[0][None, :]).reshape(8, N_MOD, d)
    mod = jnp.concatenate([mod, jnp.zeros((8, 8 - N_MOD, d), F32)], axis=1)
    mod_x, mod_c = mod[:b], mod[b:b + 1]

    w_narrow, w_wide, consts = _layout_params(w_in[0], w_uq[0], w_ukv[0], q_a_g[0], kv_a_g[0], qk_q_g[0],
                                              qk_k_g[0], dt_bias[0], a_log[0])
    g1 = norm1_g[0][None, :]
    conv_wl, conv_bl = conv_w[0], conv_b[0][None, :]
    dsk = jnp.repeat(d_skip[0], SSM_HEAD_DIM, axis=1)
    sel = _expansion_selector()

    narrow_c, wide_c = _front_call(ctx, g1, mod_c, w_narrow, w_wide, conv_wl, conv_bl, n_ctx, True, "front_ctx")
    narrow_x, wide_x = _front_call(x, g1, mod_x, w_narrow, w_wide, conv_wl, conv_bl, FRONT_TM, False, "front")

    t1c, t2c = _identity_rope_tables(n_ctx)
    t1x, t2x = _rope_tables(n_lat)
    q_all, k_all, vt_all, crow, pieces = _mla_call(narrow_c, narrow_x, jnp.concatenate([t1x, t1c], axis=0),
                                                   jnp.concatenate([t2x, t2c], axis=0), consts, "mla")

    h_zero = jnp.zeros((b, SSM_GROUPS, 2, D_STATE, GROUP_W), F32)
    _, h_ctx = _ssd_call(wide_c, pieces, crow, n_lat // n_ctx, sel, dsk, h_zero, "ssd_ctx")
    y_x, _ = _ssd_call(wide_x, pieces, crow, 0, sel, dsk, h_ctx, "ssd")

    att = _attn_call(q_all, k_all, vt_all, n_lat, ATTN_TQ, "attn")

    return _tail_call(att, y_x, wide_x, x, mod_x, ssm_norm_g[0][None, :], norm2_g[0][None, :],
                      w_proj_a[0].astype(BF16), w_proj_b[0].astype(BF16), w_out[0].astype(BF16),
                      w_mlp1[0].astype(BF16), w_mlp2[0].astype(BF16), TAIL_TM, "tail")
```

```python
import functools

import numpy as np
import jax
import jax.numpy as jnp
from jax import lax
from jax.experimental import pallas as pl
from jax.experimental.pallas import tpu as pltpu

F32 = jnp.float32
BF16 = jnp.bfloat16

D_MODEL = 1024
GRID_W = 64
MLA_HEADS = 16
QK_NOPE = 64
QK_ROPE = 32
V_DIM = 64
Q_LORA = 256
KV_LORA = 128
ROPE_PAIRS = QK_ROPE // 4
ROPE_BASE = 10000.0
ATTN_SCALE = (QK_NOPE + QK_ROPE) ** -0.5
LOG2_E = 1.4426950408889634
D_INNER = 2 * D_MODEL
SSM_HEAD_DIM = 64
SSM_HEADS = D_INNER // SSM_HEAD_DIM
SSM_GROUPS = 8
HEADS_PER_GROUP = SSM_HEADS // SSM_GROUPS
D_STATE = 128
D_CONV = 5
CHUNK = 128
CONV_DIM = D_INNER + 2 * SSM_GROUPS * D_STATE
D_FF = 4 * D_MODEL
N_MOD = 6
EPS = 1e-6

LANES = 128
HEAD_PAD = 128
GROUP_W = HEADS_PER_GROUP * SSM_HEAD_DIM
NARROW_ROPE_A = Q_LORA + KV_LORA
NARROW_ROPE_B = NARROW_ROPE_A + LANES
NARROW_DT = NARROW_ROPE_B + LANES
NARROW_W = NARROW_DT + LANES
WIDE_W = D_INNER + CONV_DIM + 2 * D_MODEL
VMEM_LIMIT = 56 * 1024 * 1024
FRONT_TM = 256
TAIL_TM = 512
ATTN_TQ = 1024
MOD_TN = 1024
FF_CHUNK = 1024


def _sigmoid(x):
    return 1.0 / (1.0 + jnp.exp(-x))


def _split3(x):
    hi = x.astype(BF16)
    r1 = x - hi.astype(F32)
    mid = r1.astype(BF16)
    lo = (r1 - mid.astype(F32)).astype(BF16)
    return hi, mid, lo


def _dot(a, b):
    return jnp.dot(a, b, preferred_element_type=F32)


def _params(*sem):
    return pltpu.CompilerParams(dimension_semantics=sem, vmem_limit_bytes=VMEM_LIMIT)


def _mod_kernel(c_ref, w_ref, b_ref, o_ref):
    c = c_ref[...]
    a = c * _sigmoid(c)
    w = w_ref[...]
    a_hi = a.astype(BF16)
    a_lo = (a - a_hi.astype(F32)).astype(BF16)
    w_hi = w.astype(BF16)
    w_lo = (w - w_hi.astype(F32)).astype(BF16)
    o_ref[...] = _dot(a_hi, w_hi) + _dot(a_hi, w_lo) + _dot(a_lo, w_hi) + b_ref[...]


def _mod_call(cvec, w_mod, b_mod):
    rows, d = cvec.shape
    n_out = w_mod.shape[1]
    tn = MOD_TN
    return pl.pallas_call(
        _mod_kernel,
        out_shape=jax.ShapeDtypeStruct((rows, n_out), F32),
        grid=(n_out // tn,),
        in_specs=[pl.BlockSpec((rows, d), lambda j: (0, 0)),
                  pl.BlockSpec((d, tn), lambda j: (0, j)),
                  pl.BlockSpec((1, tn), lambda j: (0, j))],
        out_specs=pl.BlockSpec((rows, tn), lambda j: (0, j)),
        compiler_params=_params("parallel"),
        name="mod",
    )(cvec, w_mod, b_mod)


FRONT_HALO = 16
FRONT_COLS = 1024


def _front_kernel(x_ref, xp_ref, xn_ref, g_ref, mod_ref, wn_ref, ww_ref, cw_ref, cb_ref, narrow_ref, wide_ref, pre_ref):
    i = pl.program_id(1)
    last = pl.num_programs(1) - 1
    tm = x_ref.shape[1]
    g = g_ref[...]
    shift = mod_ref[0, 0:1, :]
    scale = mod_ref[0, 1:2, :]

    def normmod(xv):
        ms = jnp.mean(xv * xv, axis=-1, keepdims=True)
        return xv * lax.rsqrt(ms + EPS) * g * (1.0 + scale) + shift

    hc = normmod(x_ref[0]).astype(BF16)
    hp = (normmod(xp_ref[0]) * (i > 0).astype(F32)).astype(BF16)
    hn = (normmod(xn_ref[0]) * (i < last).astype(F32)).astype(BF16)
    h_ext = jnp.concatenate([hp, hc, hn], axis=0)
    narrow_ref[0] = _dot(hc, wn_ref[...])
    pad = (D_CONV - 1) // 2
    n_chunks = WIDE_W // FRONT_COLS
    is_conv = [D_INNER <= j * FRONT_COLS < D_INNER + CONV_DIM for j in range(n_chunks)]
    conv_js = [j for j in range(n_chunks) if is_conv[j]]
    plain_js = [j for j in range(n_chunks) if not is_conv[j]]
    order = [j for pair in zip(conv_js, plain_js) for j in pair] + conv_js[len(plain_js):] + plain_js[len(conv_js):]
    for j in order:
        lo = j * FRONT_COLS
        cols = slice(lo, lo + FRONT_COLS)
        if D_INNER <= lo < D_INNER + CONV_DIM:
            slot = j % 2
            pre_ref[slot] = _dot(h_ext, ww_ref[:, cols])
            cc = slice(lo - D_INNER, lo - D_INNER + FRONT_COLS)
            acc = cb_ref[:, cc] + cw_ref[0:1, cc] * pre_ref[slot, pl.ds(FRONT_HALO - pad, tm), :]
            for k in range(1, D_CONV):
                acc = acc + cw_ref[k:k + 1, cc] * pre_ref[slot, pl.ds(FRONT_HALO - pad + k, tm), :]
            wide_ref[0, :, cols] = (acc * _sigmoid(acc)).astype(BF16)
        else:
            wide_ref[0, :, cols] = _dot(hc, ww_ref[:, cols]).astype(BF16)


def _front_call(x, g, mod, w_narrow, w_wide, conv_w, conv_b, tm, shared_mod, name):
    b, n, d = x.shape
    per = tm // FRONT_HALO
    n_halo = n // FRONT_HALO
    mod_map = (lambda bi, i: (0, 0, 0)) if shared_mod else (lambda bi, i: (bi, 0, 0))
    resident = lambda a: pl.BlockSpec(a.shape, lambda bi, i: (0,) * a.ndim, pipeline_mode=pl.Buffered(1))
    return pl.pallas_call(
        _front_kernel,
        out_shape=[jax.ShapeDtypeStruct((b, n, NARROW_W), F32), jax.ShapeDtypeStruct((b, n, WIDE_W), BF16)],
        grid=(b, n // tm),
        in_specs=[pl.BlockSpec((1, tm, d), lambda bi, i: (bi, i, 0)),
                  pl.BlockSpec((1, FRONT_HALO, d), lambda bi, i: (bi, jnp.maximum(i * per - 1, 0), 0)),
                  pl.BlockSpec((1, FRONT_HALO, d), lambda bi, i: (bi, jnp.minimum((i + 1) * per, n_halo - 1), 0)),
                  pl.BlockSpec((1, d), lambda bi, i: (0, 0)),
                  pl.BlockSpec((1, 8, d), mod_map),
                  resident(w_narrow), resident(w_wide), resident(conv_w), resident(conv_b)],
        out_specs=[pl.BlockSpec((1, tm, NARROW_W), lambda bi, i: (bi, i, 0)),
                   pl.BlockSpec((1, tm, WIDE_W), lambda bi, i: (bi, i, 0))],
        scratch_shapes=[pltpu.VMEM((2, tm + 2 * FRONT_HALO, FRONT_COLS), F32)],
        compiler_params=_params("parallel", "parallel"),
        name=name,
    )(x, x, x, g, mod, w_narrow, w_wide, conv_w, conv_b)


def _segment_rsqrt(t, e_ref, et_ref):
    ms = _dot((t * t).astype(BF16), e_ref[...])
    r = lax.rsqrt(ms + EPS)
    hi = r.astype(BF16)
    lo = (r - hi.astype(F32)).astype(BF16)
    return _dot(jnp.concatenate([hi, lo], axis=1), et_ref[...])


def _mla_kernel(sc_ref, sx_ref, t1_ref, t2_ref, qag_ref, kvag_ref, wuq_ref, wukv_ref, e_ref, et_ref,
                gq_ref, gk_ref, ga_ref, gb_ref, dtb_ref, alane_ref, perm_ref, q_ref, k_ref, vt_ref, crow_ref, pg_ref):
    is_ctx = pl.program_id(1) == pl.num_programs(1) - 1
    s = jnp.where(is_ctx, sc_ref[0], sx_ref[0])
    t1 = t1_ref[...]
    t2 = t2_ref[...]
    lane = lax.broadcasted_iota(jnp.int32, (1, LANES), 1)
    n_heads = MLA_HEADS
    k_w = n_heads * HEAD_PAD

    ckv = s[:, Q_LORA:Q_LORA + KV_LORA]
    ckvn = ckv * lax.rsqrt(jnp.mean(ckv * ckv, axis=-1, keepdims=True) + EPS) * kvag_ref[...]
    kv = _dot(ckvn.astype(BF16), wukv_ref[...])
    kn = kv[:, :k_w]
    vt = kv[:, k_w:].T
    ones = jnp.ones((VT_ROWS - V_DIM, vt.shape[1]), F32)
    vt_ref[0, 0] = jnp.concatenate(
        [blk for h in range(n_heads) for blk in (vt[h * V_DIM:(h + 1) * V_DIM], ones)], axis=0).astype(BF16)
    kn = kn * _segment_rsqrt(kn, e_ref, et_ref) * gk_ref[...]
    g1 = s[:, NARROW_ROPE_A:NARROW_ROPE_B]
    g2 = s[:, NARROW_ROPE_B:NARROW_DT]
    rope_lanes = (lane >= QK_NOPE) & (lane < QK_NOPE + QK_ROPE)
    kr_ms = jnp.sum(jnp.where(rope_lanes, g1 * g1, 0.0), axis=-1, keepdims=True) * (1.0 / QK_ROPE)
    krot = lax.rsqrt(kr_ms + EPS) * (g1 * ga_ref[...] * t1 + g2 * gb_ref[...] * t2)
    k_ref[0] = (kn + jnp.concatenate([krot] * n_heads, axis=1)).astype(BF16)

    cq = s[:, :Q_LORA]
    cqn = cq * lax.rsqrt(jnp.mean(cq * cq, axis=-1, keepdims=True) + EPS) * qag_ref[...]
    q = _dot(cqn.astype(BF16), wuq_ref[...])
    tq = t1 + jnp.where(lane < QK_NOPE, 1.0, 0.0)
    q = q * _segment_rsqrt(q, e_ref, et_ref) * gq_ref[...] * jnp.concatenate([tq] * n_heads, axis=1)
    q_ref[0] = (q * (ATTN_SCALE * LOG2_E)).T.astype(BF16)

    xdt = s[:, NARROW_DT:NARROW_W] + dtb_ref[...]
    dt = jnp.maximum(xdt, 0.0) + jnp.log1p(jnp.exp(-jnp.abs(xdt)))
    dt = jnp.where(lane < 2 * SSM_HEADS, dt, 0.0)
    da = dt * (-LOG2_E * jnp.exp(alane_ref[...]))
    ii = lax.broadcasted_iota(jnp.int32, (CHUNK, CHUNK), 0)
    jj = lax.broadcasted_iota(jnp.int32, (CHUNK, CHUNK), 1)
    tri_f = (jj <= ii).astype(BF16)
    tri_b = (jj >= ii).astype(BF16)
    fwd_lane = (lane % (2 * HEADS_PER_GROUP)) < HEADS_PER_GROUP
    cums = []
    for ch in range(s.shape[0] // CHUNK):
        pieces = _split3(da[ch * CHUNK:(ch + 1) * CHUNK])
        cf = sum(_dot(tri_f, p) for p in pieces)
        cb = sum(_dot(tri_b, p) for p in pieces)
        cums.append(jnp.where(fwd_lane, cf, cb))
        crow_ref[0, ch] = cums[-1].T[:2 * SSM_HEADS]
    cum = jnp.concatenate(cums, axis=0)
    d3 = _split3(dt)
    c3 = _split3(cum)
    dc3 = jnp.concatenate([d3[0], c3[0], d3[1], c3[1], d3[2], c3[2]], axis=1)
    pg_ref[0] = _dot(dc3, perm_ref[...]).astype(BF16)


def _piece_permutation():
    per = 2 * HEADS_PER_GROUP
    perm = np.zeros((6 * LANES, SSM_GROUPS * LANES), np.float32)
    for piece in range(3):
        for kind in range(2):
            for g in range(SSM_GROUPS):
                for j in range(per):
                    perm[(piece * 2 + kind) * LANES + g * per + j, g * LANES + piece * 2 * per + kind * per + j] = 1.0
    return jnp.asarray(perm, BF16)


def _mla_call(narrow_c, narrow_x, t1, t2, consts, name):
    b, n_ctx, _ = narrow_c.shape
    n_lat = narrow_x.shape[1]
    tm = n_ctx
    s_all = n_ctx + n_lat
    last_latent = n_lat // tm - 1
    k_w = MLA_HEADS * HEAD_PAD
    full = lambda a: pl.BlockSpec(a.shape, lambda bi, i: (0,) * a.ndim)
    latent_tile = lambda bi, i: (bi, jnp.minimum(i, last_latent), 0)
    return pl.pallas_call(
        _mla_kernel,
        out_shape=[jax.ShapeDtypeStruct((b, k_w, s_all), BF16),
                   jax.ShapeDtypeStruct((b, s_all, k_w), BF16),
                   jax.ShapeDtypeStruct((b, s_all // tm, MLA_HEADS * VT_ROWS, tm), BF16),
                   jax.ShapeDtypeStruct((b, s_all // CHUNK, 2 * SSM_HEADS, CHUNK), F32),
                   jax.ShapeDtypeStruct((b, s_all, SSM_GROUPS * LANES), BF16)],
        grid=(b, s_all // tm),
        in_specs=[pl.BlockSpec((1, tm, NARROW_W), lambda bi, i: (bi, 0, 0)),
                  pl.BlockSpec((1, tm, NARROW_W), latent_tile),
                  pl.BlockSpec((tm, LANES), lambda bi, i: (i, 0)),
                  pl.BlockSpec((tm, LANES), lambda bi, i: (i, 0)),
                  *[full(a) for a in consts]],
        out_specs=[pl.BlockSpec((1, k_w, tm), lambda bi, i: (bi, 0, i)),
                   pl.BlockSpec((1, tm, k_w), lambda bi, i: (bi, i, 0)),
                   pl.BlockSpec((1, 1, MLA_HEADS * VT_ROWS, tm), lambda bi, i: (bi, i, 0, 0)),
                   pl.BlockSpec((1, tm // CHUNK, 2 * SSM_HEADS, CHUNK), lambda bi, i: (bi, i, 0, 0)),
                   pl.BlockSpec((1, tm, SSM_GROUPS * LANES), lambda bi, i: (bi, i, 0))],
        compiler_params=_params("parallel", "arbitrary"),
        name=name,
    )(narrow_c, narrow_x, t1, t2, *consts)


EXP_W = 4 * GROUP_W


def _ssd_kernel(nc, xs_ref, b_ref, c_ref, p_ref, crow_ref, sel_ref, dsk_ref, h0_ref,
                y_ref, hout_ref, yacc_ref, s_ref, ex_ref, cb_ref, dec_ref, h_ref):
    q = CHUNK
    r_heads = HEADS_PER_GROUP
    wide_l = r_heads * q
    ii_w = lax.broadcasted_iota(jnp.int32, (q, wide_l), 0)
    jj_w = lax.broadcasted_iota(jnp.int32, (q, wide_l), 1) % q
    keep_f = jj_w <= ii_w
    keep_b = jj_w >= ii_w
    lane_head = lax.broadcasted_iota(jnp.int32, (1, GROUP_W), 1) // SSM_HEAD_DIM
    low_half = lax.broadcasted_iota(jnp.int32, (1, LANES), 1) < SSM_HEAD_DIM
    dsk = dsk_ref[0:1, :] + dsk_ref[1:2, :]
    gw = GROUP_W

    def per_head_128(c64):
        out = []
        for pair in range(r_heads // 2):
            v = c64[:, pair * LANES:(pair + 1) * LANES]
            sw = pltpu.roll(v, SSM_HEAD_DIM, axis=1)
            out += [jnp.where(low_half, v, sw), jnp.where(low_half, sw, v)]
        return jnp.concatenate(out, axis=1)

    def block_diag(xdt):
        xb = xdt.astype(BF16)
        return jnp.concatenate([jnp.where(lane_head == r, xb, jnp.zeros_like(xb)) for r in range(r_heads)], axis=0)

    def expand(c, carry):
        rows = pl.ds(pl.multiple_of(c * q, q), q)
        ex_ref[rows, :] = _dot(p_ref[0, rows, :], sel_ref[...])
        cb_ref[c] = lax.dot_general(c_ref[0, rows, :], b_ref[0, rows, :], (((1,), (1,)), ((), ())),
                                    preferred_element_type=F32)
        return carry

    lax.fori_loop(0, nc, expand, 0, unroll=4 if nc % 4 == 0 else 2)

    def local(c, carry):
        rows = pl.ds(pl.multiple_of(c * q, q), q)
        x = xs_ref[0, rows, :].astype(F32)
        bm = b_ref[0, rows, :]
        dte_f, dte_b = ex_ref[rows, 0:gw], ex_ref[rows, gw:2 * gw]
        c64_f, c64_b = ex_ref[rows, 2 * gw:3 * gw], ex_ref[rows, 3 * gw:4 * gw]
        cw_f, cw_b = per_head_128(c64_f), per_head_128(c64_b)
        crow = crow_ref[0, c]
        flat_f = jnp.concatenate([crow[r:r + 1, :] for r in range(r_heads)], axis=1)
        flat_b = jnp.concatenate([crow[r_heads + r:r_heads + r + 1, :] for r in range(r_heads)], axis=1)
        cb4 = jnp.concatenate([cb_ref[c]] * r_heads, axis=1)
        l_f = (jnp.where(keep_f, jnp.exp2(cw_f - flat_f), 0.0) * cb4).astype(BF16)
        l_b = (jnp.where(keep_b, jnp.exp2(cw_b - flat_b), 0.0) * cb4).astype(BF16)
        xdt_f = x * dte_f
        xdt_b = x * dte_b
        y = _dot(jnp.concatenate([l_f, l_b], axis=1),
                 jnp.concatenate([block_diag(xdt_f), block_diag(xdt_b)], axis=0))
        yacc_ref[rows, :] = y + dsk * x
        edge_f = c64_f[q - 1:q, :]
        edge_b = c64_b[0:1, :]
        xw = jnp.concatenate([(xdt_f * jnp.exp2(edge_f - c64_f)).astype(BF16),
                              (xdt_b * jnp.exp2(edge_b - c64_b)).astype(BF16)], axis=1)
        st = lax.dot_general(bm, xw, (((0,), (0,)), ((), ())), preferred_element_type=F32)
        s_ref[0, c] = st[:, :gw]
        s_ref[1, c] = st[:, gw:]
        dec_ref[0, c] = jnp.broadcast_to(jnp.exp2(edge_f), (8, gw))
        dec_ref[1, c] = jnp.broadcast_to(jnp.exp2(edge_b), (8, gw))
        return carry

    lax.fori_loop(0, nc, local, 0, unroll=4 if nc % 4 == 0 else 2)

    h_ref[0] = h0_ref[0, 0, 0]
    h_ref[1] = h0_ref[0, 0, 1]

    def carry_states(t, carry):
        for d, c in ((0, t), (1, nc - 1 - t)):
            rows = pl.ds(pl.multiple_of(c * q, q), q)
            h = h_ref[d]
            yo = _dot(c_ref[0, rows, :], h.astype(BF16)) * jnp.exp2(ex_ref[rows, (2 + d) * gw:(3 + d) * gw])
            yacc_ref[rows, :] = yacc_ref[rows, :] + yo
            h_ref[d] = h * dec_ref[d, c, 0:1, :] + s_ref[d, c]
        return carry

    lax.fori_loop(0, nc, carry_states, 0, unroll=4 if nc % 4 == 0 else 2)
    hout_ref[0, 0, 0] = h_ref[0]
    hout_ref[0, 0, 1] = h_ref[1]

    def emit(c, carry):
        rows = pl.ds(pl.multiple_of(c * q, q), q)
        y_ref[0, rows, :] = yacc_ref[rows, :].astype(y_ref.dtype)
        return carry

    lax.fori_loop(0, nc, emit, 0)


def _ssd_call(wide, pieces, crow, seq_block, sel, dsk, h0, name):
    xbc = wide
    b, n, _ = wide.shape
    nc = n // CHUNK
    g = SSM_GROUPS
    x_col0 = D_INNER // GROUP_W
    b_col0 = 2 * D_INNER // D_STATE
    c_col0 = b_col0 + g
    return pl.pallas_call(
        functools.partial(_ssd_kernel, nc),
        out_shape=[jax.ShapeDtypeStruct((b, n, D_INNER), BF16),
                   jax.ShapeDtypeStruct((b, g, 2, D_STATE, GROUP_W), F32)],
        grid=(b, g),
        in_specs=[pl.BlockSpec((1, n, GROUP_W), lambda bi, gi: (bi, 0, x_col0 + gi)),
                  pl.BlockSpec((1, n, D_STATE), lambda bi, gi: (bi, 0, b_col0 + gi)),
                  pl.BlockSpec((1, n, D_STATE), lambda bi, gi: (bi, 0, c_col0 + gi)),
                  pl.BlockSpec((1, n, LANES), lambda bi, gi: (bi, seq_block, gi)),
                  pl.BlockSpec((1, nc, 2 * HEADS_PER_GROUP, CHUNK), lambda bi, gi: (bi, seq_block, gi, 0)),
                  pl.BlockSpec((LANES, EXP_W), lambda bi, gi: (0, 0)),
                  pl.BlockSpec((2, GROUP_W), lambda bi, gi: (0, gi)),
                  pl.BlockSpec((1, 1, 2, D_STATE, GROUP_W), lambda bi, gi: (bi, gi, 0, 0, 0))],
        out_specs=[pl.BlockSpec((1, n, GROUP_W), lambda bi, gi: (bi, 0, gi)),
                   pl.BlockSpec((1, 1, 2, D_STATE, GROUP_W), lambda bi, gi: (bi, gi, 0, 0, 0))],
        scratch_shapes=[pltpu.VMEM((n, GROUP_W), F32),
                        pltpu.VMEM((2, nc, D_STATE, GROUP_W), F32),
                        pltpu.VMEM((n, EXP_W), F32),
                        pltpu.VMEM((nc, CHUNK, CHUNK), F32),
                        pltpu.VMEM((2, nc, 8, GROUP_W), F32),
                        pltpu.VMEM((2, D_STATE, GROUP_W), F32)],
        compiler_params=_params("parallel", "parallel"),
        name=name,
    )(xbc, xbc, xbc, pieces, crow, sel, dsk, h0)


VT_ROWS = V_DIM + 16

KEY_CHUNK = 512
ATTN_HEADS = 4


def _attn_kernel(q_ref, k_ref, vt_ref, o_ref, s_bufs, cmaxs, acc, m_run):
    n_keys = k_ref.shape[1]
    blk = vt_ref.shape[3]
    per = KEY_CHUNK // blk
    n_full = n_keys // KEY_CHUNK
    tail = n_keys - n_full * KEY_CHUNK

    def produce_head(h, slot, c, n_rows):
        start = c * KEY_CHUNK
        rows = pl.ds(start if isinstance(c, int) else pl.multiple_of(start, KEY_CHUNK), n_rows)
        st = _dot(k_ref[0, rows, h * HEAD_PAD:(h + 1) * HEAD_PAD], q_ref[0, h * HEAD_PAD:(h + 1) * HEAD_PAD, :])
        s_bufs[slot][h, 0:n_rows, :] = st
        cmaxs[slot][h, 0:1, :] = jnp.max(st, axis=0, keepdims=True)

    def consume_head(h, slot, c, n_rows):
        m_old = m_run[h, 0:1, :]
        m_new = jnp.maximum(m_old, cmaxs[slot][h, 0:1, :])
        p = jnp.exp2(s_bufs[slot][h, 0:n_rows, :] - m_new).astype(BF16)
        vt = jnp.concatenate([vt_ref[0, per * c + j, h * VT_ROWS:(h + 1) * VT_ROWS, :]
                              for j in range(n_rows // blk)], axis=1)
        acc[h] = acc[h] * jnp.exp2(m_old - m_new) + _dot(vt, p)
        m_run[h, 0:1, :] = m_new

    def produce(slot, c, n_rows=KEY_CHUNK):
        for h in range(ATTN_HEADS):
            produce_head(h, slot, c, n_rows)

    def consume(slot, c, n_rows=KEY_CHUNK):
        for h in range(ATTN_HEADS):
            consume_head(h, slot, c, n_rows)

    def produce_consume(p_slot, p_c, c_slot, c_c, p_rows=KEY_CHUNK, c_rows=KEY_CHUNK):
        for h in range(ATTN_HEADS):
            produce_head(h, p_slot, p_c, p_rows)
            consume_head(h, c_slot, c_c, c_rows)

    m_run[...] = jnp.full(m_run.shape, -jnp.inf, F32)
    acc[...] = jnp.zeros(acc.shape, F32)
    n_chunks = n_full + (1 if tail else 0)
    size = lambda c: tail if c == n_full else KEY_CHUNK
    produce(0, 0)

    def body(j, carry):
        produce_consume(1, 2 * j + 1, 0, 2 * j)
        produce_consume(0, 2 * j + 2, 1, 2 * j + 1)
        return carry

    trips = (n_full - 1) // 2
    lax.fori_loop(0, trips, body, 0)
    for c in range(2 * trips, n_chunks):
        slot = c % 2
        if c + 1 < n_chunks:
            produce_consume(1 - slot, c + 1, slot, c, size(c + 1), size(c))
        else:
            consume(slot, c, size(c))
    outs = [acc[h, 0:V_DIM, :] / acc[h, V_DIM:V_DIM + 1, :] for h in range(ATTN_HEADS)]
    o_ref[0] = jnp.concatenate(outs, axis=0).T.astype(o_ref.dtype)


def _attn_call(q, k, vt, n, tq, name):
    b = q.shape[0]
    s = k.shape[1]
    n_blk, _, blk = vt.shape[1:]
    nh = ATTN_HEADS
    return pl.pallas_call(
        _attn_kernel,
        out_shape=jax.ShapeDtypeStruct((b, n, MLA_HEADS * V_DIM), BF16),
        grid=(b, MLA_HEADS // nh, n // tq),
        in_specs=[pl.BlockSpec((1, nh * HEAD_PAD, tq), lambda bi, hp, i: (bi, hp, i)),
                  pl.BlockSpec((1, s, nh * HEAD_PAD), lambda bi, hp, i: (bi, 0, hp)),
                  pl.BlockSpec((1, n_blk, nh * VT_ROWS, blk), lambda bi, hp, i: (bi, 0, hp, 0))],
        out_specs=pl.BlockSpec((1, tq, nh * V_DIM), lambda bi, hp, i: (bi, i, hp)),
        scratch_shapes=[[pltpu.VMEM((nh, KEY_CHUNK, tq), F32)] * 2,
                        [pltpu.VMEM((nh, 8, tq), F32)] * 2,
                        pltpu.VMEM((nh, VT_ROWS, tq), F32),
                        pltpu.VMEM((nh, 8, tq), F32)],
        compiler_params=_params("parallel", "parallel", "parallel"),
        name=name,
    )(q, k, vt)


def _tail_kernel(ff_chunk, att_ref, y_ref, z_ref, ga_ref, gb_ref, x_ref, mod_ref, sg_ref, g2_ref,
                 wa_ref, wb_ref, wo_ref, w1_ref, w2_ref, o_ref):
    y = y_ref[0].astype(F32)
    z = z_ref[0].astype(F32)
    u = y * (z * _sigmoid(z))
    gw = D_INNER // SSM_GROUPS
    parts = []
    for g in range(SSM_GROUPS):
        ug = u[:, g * gw:(g + 1) * gw]
        parts.append(ug * lax.rsqrt(jnp.mean(ug * ug, axis=-1, keepdims=True) + EPS))
    un = (jnp.concatenate(parts, axis=1) * sg_ref[...]).astype(BF16)
    a = _dot(att_ref[0], wa_ref[...])
    bb = _dot(un, wb_ref[...])
    merged = _sigmoid(ga_ref[0].astype(F32)) * a + _sigmoid(gb_ref[0].astype(F32)) * bb
    o = _dot(merged.astype(BF16), wo_ref[...])
    x = x_ref[0] + mod_ref[0, 2:3, :] * o

    ms = jnp.mean(x * x, axis=-1, keepdims=True)
    xn = x * lax.rsqrt(ms + EPS) * g2_ref[...]
    h = (xn * (1.0 + mod_ref[0, 4:5, :]) + mod_ref[0, 3:4, :]).astype(BF16)
    acc = None
    for c in range(D_FF // ff_chunk):
        a = _dot(h, w1_ref[:, c * ff_chunk:(c + 1) * ff_chunk])
        a = jnp.maximum(a, 0.0)
        part = _dot((a * a).astype(BF16), w2_ref[c * ff_chunk:(c + 1) * ff_chunk, :])
        acc = part if acc is None else acc + part
    o_ref[0] = x + mod_ref[0, 5:6, :] * acc


def _tail_call(att, y, wide, x, mod, sg, g2, wa, wb, wo, w1, w2, tm, name):
    b, n, d = x.shape
    ga_blk = (D_INNER + CONV_DIM) // d
    full = lambda a: pl.BlockSpec(a.shape, lambda bi, i: (0,) * a.ndim)
    resident = lambda a: pl.BlockSpec(a.shape, lambda bi, i: (0,) * a.ndim, pipeline_mode=pl.Buffered(1))
    return pl.pallas_call(
        functools.partial(_tail_kernel, FF_CHUNK),
        out_shape=jax.ShapeDtypeStruct((b, n, d), F32),
        grid=(b, n // tm),
        in_specs=[pl.BlockSpec((1, tm, MLA_HEADS * V_DIM), lambda bi, i: (bi, i, 0)),
                  pl.BlockSpec((1, tm, D_INNER), lambda bi, i: (bi, i, 0)),
                  pl.BlockSpec((1, tm, D_INNER), lambda bi, i: (bi, i, 0)),
                  pl.BlockSpec((1, tm, d), lambda bi, i: (bi, i, ga_blk)),
                  pl.BlockSpec((1, tm, d), lambda bi, i: (bi, i, ga_blk + 1)),
                  pl.BlockSpec((1, tm, d), lambda bi, i: (bi, i, 0)),
                  pl.BlockSpec((1, 8, d), lambda bi, i: (bi, 0, 0)),
                  full(sg), full(g2), resident(wa), resident(wb), resident(wo), resident(w1), resident(w2)],
        out_specs=pl.BlockSpec((1, tm, d), lambda bi, i: (bi, i, 0)),
        compiler_params=_params("parallel", "parallel"),
        name=name,
    )(att, y, wide, wide, wide, x, mod, sg, g2, wa, wb, wo, w1, w2)


_ROPE_SWAP = np.concatenate([np.arange(8, 16), np.arange(0, 8), np.arange(24, 32), np.arange(16, 24)])
_DT_LANE_ORDER = np.array([d * SSM_HEADS + g * HEADS_PER_GROUP + r for g in range(SSM_GROUPS) for d in range(2)
                           for r in range(HEADS_PER_GROUP)])


def _rope_tables(n_lat):
    rows = n_lat // GRID_W
    row = jnp.repeat(jnp.arange(rows), GRID_W)
    col = jnp.tile(jnp.arange(GRID_W), rows)
    freqs = ROPE_BASE ** (-jnp.arange(ROPE_PAIRS, dtype=F32) / ROPE_PAIRS)
    ang = jnp.stack([row, col], axis=-1).astype(F32)[..., None] * freqs
    cos, sin = jnp.cos(ang), jnp.sin(ang)
    cos32 = jnp.stack([cos, cos], axis=2).reshape(n_lat, QK_ROPE)
    sin32 = jnp.stack([-sin, sin], axis=2).reshape(n_lat, QK_ROPE)
    zeros = jnp.zeros((n_lat, QK_NOPE), F32)
    return (jnp.concatenate([zeros, cos32, sin32], axis=1),
            jnp.concatenate([zeros, sin32, cos32], axis=1))


def _identity_rope_tables(n):
    zeros = jnp.zeros((n, QK_NOPE), F32)
    one = jnp.ones((n, QK_ROPE), F32)
    zero = jnp.zeros((n, QK_ROPE), F32)
    return jnp.concatenate([zeros, one, zero], axis=1), jnp.concatenate([zeros, zero, one], axis=1)


def _layout_params(w_in, w_uq, w_ukv, q_a_g, kv_a_g, qk_q_g, qk_k_g, dt_bias, a_log):
    d = w_in.shape[0]
    o_cq, o_ckv, o_kr = 0, Q_LORA, Q_LORA + KV_LORA
    o_z = o_kr + QK_ROPE
    o_xbc = o_z + D_INNER
    o_dt = o_xbc + CONV_DIM
    o_ga = o_dt + 2 * SSM_HEADS
    kr = w_in[:, o_kr:o_kr + QK_ROPE]
    kr_sw = kr[:, _ROPE_SWAP]
    z64 = jnp.zeros((d, LANES - 2 * QK_ROPE), F32)
    w_narrow = jnp.concatenate(
        [w_in[:, o_cq:o_kr], z64, kr, kr_sw, z64, kr_sw, kr, w_in[:, o_dt:o_ga][:, _DT_LANE_ORDER], z64],
        axis=1).astype(BF16)
    w_wide = jnp.concatenate([w_in[:, o_z:o_dt], w_in[:, o_ga:]], axis=1).astype(BF16)

    hq = w_uq.reshape(Q_LORA, MLA_HEADS, QK_NOPE + QK_ROPE)
    rope_q = hq[:, :, QK_NOPE:]
    wuq = jnp.concatenate([hq[:, :, :QK_NOPE], rope_q, rope_q[:, :, _ROPE_SWAP]], axis=2)
    wuq = wuq.reshape(Q_LORA, MLA_HEADS * HEAD_PAD).astype(BF16)
    hkv = w_ukv.reshape(KV_LORA, MLA_HEADS, QK_NOPE + V_DIM)
    wk = jnp.concatenate([hkv[:, :, :QK_NOPE], jnp.zeros((KV_LORA, MLA_HEADS, HEAD_PAD - QK_NOPE), F32)], axis=2)
    wukv = jnp.concatenate([wk.reshape(KV_LORA, MLA_HEADS * HEAD_PAD),
                            hkv[:, :, QK_NOPE:].reshape(KV_LORA, MLA_HEADS * V_DIM)], axis=1).astype(BF16)

    gq_r = qk_q_g[QK_NOPE:]
    gq = jnp.tile(jnp.concatenate([qk_q_g[:QK_NOPE], gq_r, gq_r[_ROPE_SWAP]]), MLA_HEADS)[None, :]
    gk = jnp.tile(jnp.concatenate([qk_k_g[:QK_NOPE], jnp.zeros((HEAD_PAD - QK_NOPE,), F32)]), MLA_HEADS)[None, :]
    gk_r = qk_k_g[QK_NOPE:]
    z64v = jnp.zeros((QK_NOPE,), F32)
    ga = jnp.concatenate([z64v, gk_r, gk_r[_ROPE_SWAP]])[None, :]
    gb = jnp.concatenate([z64v, gk_r[_ROPE_SWAP], gk_r])[None, :]
    lane_pad = jnp.zeros((LANES - 2 * SSM_HEADS,), F32)
    dtb = jnp.concatenate([dt_bias.reshape(-1)[_DT_LANE_ORDER], lane_pad])[None, :]
    alane = jnp.concatenate([a_log.astype(F32).reshape(-1)[_DT_LANE_ORDER], lane_pad])[None, :]

    lane = np.arange(MLA_HEADS * HEAD_PAD)
    head, off = lane // HEAD_PAD, lane % HEAD_PAD
    e = np.zeros((MLA_HEADS * HEAD_PAD, LANES), np.float32)
    e[lane[off < QK_NOPE], 2 * head[off < QK_NOPE]] = 1.0 / QK_NOPE
    rope = (off >= QK_NOPE) & (off < QK_NOPE + QK_ROPE)
    e[lane[rope], 2 * head[rope] + 1] = 1.0 / QK_ROPE
    et = np.zeros((LANES, MLA_HEADS * HEAD_PAD), np.float32)
    et[2 * head[off < QK_NOPE], lane[off < QK_NOPE]] = 1.0
    et[2 * head[off >= QK_NOPE] + 1, lane[off >= QK_NOPE]] = 1.0
    et = np.concatenate([et, et], axis=0)
    consts = (q_a_g[None, :], kv_a_g[None, :], wuq, wukv, jnp.asarray(e, BF16), jnp.asarray(et, BF16),
              gq, gk, ga, gb, dtb, alane, _piece_permutation())
    return w_narrow, w_wide, consts


def _expansion_selector():
    r = HEADS_PER_GROUP
    sel = np.zeros((LANES, EXP_W), np.float32)
    for piece in range(3):
        for d in range(2):
            for h in range(r):
                row_dt = piece * 4 * r + d * r + h
                row_cum = piece * 4 * r + 2 * r + d * r + h
                sel[row_dt, d * GROUP_W + h * SSM_HEAD_DIM:d * GROUP_W + (h + 1) * SSM_HEAD_DIM] = 1.0
                base = 2 * GROUP_W + d * GROUP_W + h * SSM_HEAD_DIM
                sel[row_cum, base:base + SSM_HEAD_DIM] = 1.0
    return jnp.asarray(sel, BF16)


def kernel(x, c, ctx, c_ctx, norm1_g, norm2_g, w_mod, b_mod, w_in, q_a_g, w_uq, kv_a_g, w_ukv, qk_q_g, qk_k_g,
           conv_w, conv_b, dt_bias, a_log, d_skip, ssm_norm_g, w_proj_a, w_proj_b, w_out, w_mlp1, w_mlp2):
    assert w_mod.shape[0] == 1, "single-layer block"
    b, n_lat, d = x.shape
    n_ctx = ctx.shape[1]

    cvec = jnp.concatenate([c, c_ctx[None, :], jnp.zeros((8 - b - 1, d), F32)], axis=0)
    mod = _mod_call(cvec, w_mod[0], b_mod[0][None, :]).reshape(8, N_MOD, d)
    mod = jnp.concatenate([mod, jnp.zeros((8, 8 - N_MOD, d), F32)], axis=1)
    mod_x, mod_c = mod[:b], mod[b:b + 1]

    w_narrow, w_wide, consts = _layout_params(w_in[0], w_uq[0], w_ukv[0], q_a_g[0], kv_a_g[0], qk_q_g[0],
                                              qk_k_g[0], dt_bias[0], a_log[0])
    g1 = norm1_g[0][None, :]
    conv_wl, conv_bl = conv_w[0], conv_b[0][None, :]
    dsk = jnp.repeat(d_skip[0], SSM_HEAD_DIM, axis=1)
    sel = _expansion_selector()

    narrow_c, wide_c = _front_call(ctx, g1, mod_c, w_narrow, w_wide, conv_wl, conv_bl, n_ctx, True, "front_ctx")
    narrow_x, wide_x = _front_call(x, g1, mod_x, w_narrow, w_wide, conv_wl, conv_bl, FRONT_TM, False, "front")

    t1c, t2c = _identity_rope_tables(n_ctx)
    t1x, t2x = _rope_tables(n_lat)
    q_all, k_all, vt_all, crow, pieces = _mla_call(narrow_c, narrow_x, jnp.concatenate([t1x, t1c], axis=0),
                                                   jnp.concatenate([t2x, t2c], axis=0), consts, "mla")

    h_zero = jnp.zeros((b, SSM_GROUPS, 2, D_STATE, GROUP_W), F32)
    _, h_ctx = _ssd_call(wide_c, pieces, crow, n_lat // n_ctx, sel, dsk, h_zero, "ssd_ctx")
    y_x, _ = _ssd_call(wide_x, pieces, crow, 0, sel, dsk, h_ctx, "ssd")

    att = _attn_call(q_all, k_all, vt_all, n_lat, ATTN_TQ, "attn")

    return _tail_call(att, y_x, wide_x, x, mod_x, ssm_norm_g[0][None, :], norm2_g[0][None, :],
                      w_proj_a[0].astype(BF16), w_proj_b[0].astype(BF16), w_out[0].astype(BF16),
                      w_mlp1[0].astype(BF16), w_mlp2[0].astype(BF16), TAIL_TM, "tail")
```

```python
import functools

import numpy as np
import jax
import jax.numpy as jnp
from jax import lax
from jax.experimental import pallas as pl
from jax.experimental.pallas import tpu as pltpu

F32 = jnp.float32
BF16 = jnp.bfloat16

D_MODEL = 1024
GRID_W = 64
MLA_HEADS = 16
QK_NOPE = 64
QK_ROPE = 32
V_DIM = 64
Q_LORA = 256
KV_LORA = 128
ROPE_PAIRS = QK_ROPE // 4
ROPE_BASE = 10000.0
ATTN_SCALE = (QK_NOPE + QK_ROPE) ** -0.5
LOG2_E = 1.4426950408889634
D_INNER = 2 * D_MODEL
SSM_HEAD_DIM = 64
SSM_HEADS = D_INNER // SSM_HEAD_DIM
SSM_GROUPS = 8
HEADS_PER_GROUP = SSM_HEADS // SSM_GROUPS
D_STATE = 128
D_CONV = 5
CHUNK = 128
CONV_DIM = D_INNER + 2 * SSM_GROUPS * D_STATE
D_FF = 4 * D_MODEL
N_MOD = 6
EPS = 1e-6

LANES = 128
HEAD_PAD = 128
GROUP_W = HEADS_PER_GROUP * SSM_HEAD_DIM
NARROW_ROPE_A = Q_LORA + KV_LORA
NARROW_ROPE_B = NARROW_ROPE_A + LANES
NARROW_DT = NARROW_ROPE_B + LANES
NARROW_W = NARROW_DT + LANES
WIDE_W = D_INNER + CONV_DIM + 2 * D_MODEL
VMEM_LIMIT = 56 * 1024 * 1024
FRONT_TM = 512
TAIL_TM = 512
ATTN_TQ = 1024
MOD_TN = 1024
FF_CHUNK = 1024


def _sigmoid(x):
    return 1.0 / (1.0 + jnp.exp(-x))


def _split3(x):
    hi = x.astype(BF16)
    r1 = x - hi.astype(F32)
    mid = r1.astype(BF16)
    lo = (r1 - mid.astype(F32)).astype(BF16)
    return hi, mid, lo


def _dot(a, b):
    return jnp.dot(a, b, preferred_element_type=F32)


def _params(*sem):
    return pltpu.CompilerParams(dimension_semantics=sem, vmem_limit_bytes=VMEM_LIMIT)


def _mod_kernel(c_ref, w_ref, b_ref, o_ref):
    c = c_ref[...]
    a = c * _sigmoid(c)
    w = w_ref[...]
    a_hi = a.astype(BF16)
    a_lo = (a - a_hi.astype(F32)).astype(BF16)
    w_hi = w.astype(BF16)
    w_lo = (w - w_hi.astype(F32)).astype(BF16)
    o_ref[...] = _dot(a_hi, w_hi) + _dot(a_hi, w_lo) + _dot(a_lo, w_hi) + b_ref[...]


def _mod_call(cvec, w_mod, b_mod):
    rows, d = cvec.shape
    n_out = w_mod.shape[1]
    tn = MOD_TN
    return pl.pallas_call(
        _mod_kernel,
        out_shape=jax.ShapeDtypeStruct((rows, n_out), F32),
        grid=(n_out // tn,),
        in_specs=[pl.BlockSpec((rows, d), lambda j: (0, 0)),
                  pl.BlockSpec((d, tn), lambda j: (0, j)),
                  pl.BlockSpec((1, tn), lambda j: (0, j))],
        out_specs=pl.BlockSpec((rows, tn), lambda j: (0, j)),
        compiler_params=_params("parallel"),
        name="mod",
    )(cvec, w_mod, b_mod)


FRONT_HALO = 16
FRONT_COLS = 1024


def _front_kernel(x_ref, xp_ref, xn_ref, g_ref, mod_ref, wn_ref, ww_ref, cw_ref, cb_ref, narrow_ref, wide_ref, pre_ref):
    i = pl.program_id(1)
    last = pl.num_programs(1) - 1
    tm = x_ref.shape[1]
    g = g_ref[...]
    shift = mod_ref[0, 0:1, :]
    scale = mod_ref[0, 1:2, :]

    def normmod(xv):
        ms = jnp.mean(xv * xv, axis=-1, keepdims=True)
        return xv * lax.rsqrt(ms + EPS) * g * (1.0 + scale) + shift

    hc = normmod(x_ref[0]).astype(BF16)
    hp = (normmod(xp_ref[0]) * (i > 0).astype(F32)).astype(BF16)
    hn = (normmod(xn_ref[0]) * (i < last).astype(F32)).astype(BF16)
    h_ext = jnp.concatenate([hp, hc, hn], axis=0)
    narrow_ref[0] = _dot(hc, wn_ref[...])
    pad = (D_CONV - 1) // 2
    n_chunks = WIDE_W // FRONT_COLS
    is_conv = [D_INNER <= j * FRONT_COLS < D_INNER + CONV_DIM for j in range(n_chunks)]
    conv_js = [j for j in range(n_chunks) if is_conv[j]]
    plain_js = [j for j in range(n_chunks) if not is_conv[j]]
    order = [j for pair in zip(conv_js, plain_js) for j in pair] + conv_js[len(plain_js):] + plain_js[len(conv_js):]
    for j in order:
        lo = j * FRONT_COLS
        cols = slice(lo, lo + FRONT_COLS)
        if D_INNER <= lo < D_INNER + CONV_DIM:
            slot = j % 2
            pre_ref[slot] = _dot(h_ext, ww_ref[:, cols])
            cc = slice(lo - D_INNER, lo - D_INNER + FRONT_COLS)
            acc = cb_ref[:, cc] + cw_ref[0:1, cc] * pre_ref[slot, pl.ds(FRONT_HALO - pad, tm), :]
            for k in range(1, D_CONV):
                acc = acc + cw_ref[k:k + 1, cc] * pre_ref[slot, pl.ds(FRONT_HALO - pad + k, tm), :]
            wide_ref[0, :, cols] = (acc * _sigmoid(acc)).astype(BF16)
        else:
            wide_ref[0, :, cols] = _dot(hc, ww_ref[:, cols]).astype(BF16)


def _front_call(x, g, mod, w_narrow, w_wide, conv_w, conv_b, tm, shared_mod, name):
    b, n, d = x.shape
    per = tm // FRONT_HALO
    n_halo = n // FRONT_HALO
    mod_map = (lambda bi, i: (0, 0, 0)) if shared_mod else (lambda bi, i: (bi, 0, 0))
    resident = lambda a: pl.BlockSpec(a.shape, lambda bi, i: (0,) * a.ndim, pipeline_mode=pl.Buffered(1))
    return pl.pallas_call(
        _front_kernel,
        out_shape=[jax.ShapeDtypeStruct((b, n, NARROW_W), F32), jax.ShapeDtypeStruct((b, n, WIDE_W), BF16)],
        grid=(b, n // tm),
        in_specs=[pl.BlockSpec((1, tm, d), lambda bi, i: (bi, i, 0)),
                  pl.BlockSpec((1, FRONT_HALO, d), lambda bi, i: (bi, jnp.maximum(i * per - 1, 0), 0)),
                  pl.BlockSpec((1, FRONT_HALO, d), lambda bi, i: (bi, jnp.minimum((i + 1) * per, n_halo - 1), 0)),
                  pl.BlockSpec((1, d), lambda bi, i: (0, 0)),
                  pl.BlockSpec((1, 8, d), mod_map),
                  resident(w_narrow), resident(w_wide), resident(conv_w), resident(conv_b)],
        out_specs=[pl.BlockSpec((1, tm, NARROW_W), lambda bi, i: (bi, i, 0)),
                   pl.BlockSpec((1, tm, WIDE_W), lambda bi, i: (bi, i, 0))],
        scratch_shapes=[pltpu.VMEM((2, tm + 2 * FRONT_HALO, FRONT_COLS), F32)],
        compiler_params=_params("parallel", "parallel"),
        name=name,
    )(x, x, x, g, mod, w_narrow, w_wide, conv_w, conv_b)


def _segment_rsqrt(t, e_ref, et_ref):
    ms = _dot((t * t).astype(BF16), e_ref[...])
    r = lax.rsqrt(ms + EPS)
    hi = r.astype(BF16)
    lo = (r - hi.astype(F32)).astype(BF16)
    return _dot(jnp.concatenate([hi, lo], axis=1), et_ref[...])


def _mla_kernel(sc_ref, sx_ref, t1_ref, t2_ref, qag_ref, kvag_ref, wuq_ref, wukv_ref, e_ref, et_ref,
                gq_ref, gk_ref, ga_ref, gb_ref, dtb_ref, alane_ref, perm_ref, q_ref, k_ref, vt_ref, crow_ref, pg_ref):
    is_ctx = pl.program_id(1) == pl.num_programs(1) - 1
    s = jnp.where(is_ctx, sc_ref[0], sx_ref[0])
    t1 = t1_ref[...]
    t2 = t2_ref[...]
    lane = lax.broadcasted_iota(jnp.int32, (1, LANES), 1)
    n_heads = MLA_HEADS
    k_w = n_heads * HEAD_PAD

    ckv = s[:, Q_LORA:Q_LORA + KV_LORA]
    ckvn = ckv * lax.rsqrt(jnp.mean(ckv * ckv, axis=-1, keepdims=True) + EPS) * kvag_ref[...]
    kv = _dot(ckvn.astype(BF16), wukv_ref[...])
    kn = kv[:, :k_w]
    vt = kv[:, k_w:].T
    ones = jnp.ones((VT_ROWS - V_DIM, vt.shape[1]), F32)
    vt_ref[0, 0] = jnp.concatenate(
        [blk for h in range(n_heads) for blk in (vt[h * V_DIM:(h + 1) * V_DIM], ones)], axis=0).astype(BF16)
    kn = kn * _segment_rsqrt(kn, e_ref, et_ref) * gk_ref[...]
    g1 = s[:, NARROW_ROPE_A:NARROW_ROPE_B]
    g2 = s[:, NARROW_ROPE_B:NARROW_DT]
    rope_lanes = (lane >= QK_NOPE) & (lane < QK_NOPE + QK_ROPE)
    kr_ms = jnp.sum(jnp.where(rope_lanes, g1 * g1, 0.0), axis=-1, keepdims=True) * (1.0 / QK_ROPE)
    krot = lax.rsqrt(kr_ms + EPS) * (g1 * ga_ref[...] * t1 + g2 * gb_ref[...] * t2)
    k_ref[0] = (kn + jnp.concatenate([krot] * n_heads, axis=1)).astype(BF16)

    cq = s[:, :Q_LORA]
    cqn = cq * lax.rsqrt(jnp.mean(cq * cq, axis=-1, keepdims=True) + EPS) * qag_ref[...]
    q = _dot(cqn.astype(BF16), wuq_ref[...])
    tq = t1 + jnp.where(lane < QK_NOPE, 1.0, 0.0)
    q = q * _segment_rsqrt(q, e_ref, et_ref) * gq_ref[...] * jnp.concatenate([tq] * n_heads, axis=1)
    q_ref[0] = (q * (ATTN_SCALE * LOG2_E)).T.astype(BF16)

    xdt = s[:, NARROW_DT:NARROW_W] + dtb_ref[...]
    dt = jnp.maximum(xdt, 0.0) + jnp.log1p(jnp.exp(-jnp.abs(xdt)))
    dt = jnp.where(lane < 2 * SSM_HEADS, dt, 0.0)
    da = dt * (-LOG2_E * jnp.exp(alane_ref[...]))
    ii = lax.broadcasted_iota(jnp.int32, (CHUNK, CHUNK), 0)
    jj = lax.broadcasted_iota(jnp.int32, (CHUNK, CHUNK), 1)
    tri_f = (jj <= ii).astype(BF16)
    tri_b = (jj >= ii).astype(BF16)
    fwd_lane = (lane % (2 * HEADS_PER_GROUP)) < HEADS_PER_GROUP
    cums = []
    for ch in range(s.shape[0] // CHUNK):
        pieces = _split3(da[ch * CHUNK:(ch + 1) * CHUNK])
        cf = sum(_dot(tri_f, p) for p in pieces)
        cb = sum(_dot(tri_b, p) for p in pieces)
        cums.append(jnp.where(fwd_lane, cf, cb))
        crow_ref[0, ch] = cums[-1].T[:2 * SSM_HEADS]
    cum = jnp.concatenate(cums, axis=0)
    d3 = _split3(dt)
    c3 = _split3(cum)
    dc3 = jnp.concatenate([d3[0], c3[0], d3[1], c3[1], d3[2], c3[2]], axis=1)
    pg_ref[0] = _dot(dc3, perm_ref[...]).astype(BF16)


def _piece_permutation():
    per = 2 * HEADS_PER_GROUP
    perm = np.zeros((6 * LANES, SSM_GROUPS * LANES), np.float32)
    for piece in range(3):
        for kind in range(2):
            for g in range(SSM_GROUPS):
                for j in range(per):
                    perm[(piece * 2 + kind) * LANES + g * per + j, g * LANES + piece * 2 * per + kind * per + j] = 1.0
    return jnp.asarray(perm, BF16)


def _mla_call(narrow_c, narrow_x, t1, t2, consts, name):
    b, n_ctx, _ = narrow_c.shape
    n_lat = narrow_x.shape[1]
    tm = n_ctx
    s_all = n_ctx + n_lat
    last_latent = n_lat // tm - 1
    k_w = MLA_HEADS * HEAD_PAD
    full = lambda a: pl.BlockSpec(a.shape, lambda bi, i: (0,) * a.ndim)
    latent_tile = lambda bi, i: (bi, jnp.minimum(i, last_latent), 0)
    return pl.pallas_call(
        _mla_kernel,
        out_shape=[jax.ShapeDtypeStruct((b, k_w, s_all), BF16),
                   jax.ShapeDtypeStruct((b, s_all, k_w), BF16),
                   jax.ShapeDtypeStruct((b, s_all // tm, MLA_HEADS * VT_ROWS, tm), BF16),
                   jax.ShapeDtypeStruct((b, s_all // CHUNK, 2 * SSM_HEADS, CHUNK), F32),
                   jax.ShapeDtypeStruct((b, s_all, SSM_GROUPS * LANES), BF16)],
        grid=(b, s_all // tm),
        in_specs=[pl.BlockSpec((1, tm, NARROW_W), lambda bi, i: (bi, 0, 0)),
                  pl.BlockSpec((1, tm, NARROW_W), latent_tile),
                  pl.BlockSpec((tm, LANES), lambda bi, i: (i, 0)),
                  pl.BlockSpec((tm, LANES), lambda bi, i: (i, 0)),
                  *[full(a) for a in consts]],
        out_specs=[pl.BlockSpec((1, k_w, tm), lambda bi, i: (bi, 0, i)),
                   pl.BlockSpec((1, tm, k_w), lambda bi, i: (bi, i, 0)),
                   pl.BlockSpec((1, 1, MLA_HEADS * VT_ROWS, tm), lambda bi, i: (bi, i, 0, 0)),
                   pl.BlockSpec((1, tm // CHUNK, 2 * SSM_HEADS, CHUNK), lambda bi, i: (bi, i, 0, 0)),
                   pl.BlockSpec((1, tm, SSM_GROUPS * LANES), lambda bi, i: (bi, i, 0))],
        compiler_params=_params("parallel", "arbitrary"),
        name=name,
    )(narrow_c, narrow_x, t1, t2, *consts)


EXP_W = 4 * GROUP_W


def _ssd_kernel(nc, xs_ref, b_ref, c_ref, p_ref, crow_ref, sel_ref, dsk_ref, h0_ref,
                y_ref, hout_ref, yacc_ref, s_ref, ex_ref, cb_ref, dec_ref, h_ref):
    q = CHUNK
    r_heads = HEADS_PER_GROUP
    wide_l = r_heads * q
    ii_w = lax.broadcasted_iota(jnp.int32, (q, wide_l), 0)
    jj_w = lax.broadcasted_iota(jnp.int32, (q, wide_l), 1) % q
    keep_f = jj_w <= ii_w
    keep_b = jj_w >= ii_w
    lane_head = lax.broadcasted_iota(jnp.int32, (1, GROUP_W), 1) // SSM_HEAD_DIM
    low_half = lax.broadcasted_iota(jnp.int32, (1, LANES), 1) < SSM_HEAD_DIM
    dsk = dsk_ref[0:1, :] + dsk_ref[1:2, :]
    gw = GROUP_W

    def per_head_128(c64):
        out = []
        for pair in range(r_heads // 2):
            v = c64[:, pair * LANES:(pair + 1) * LANES]
            sw = pltpu.roll(v, SSM_HEAD_DIM, axis=1)
            out += [jnp.where(low_half, v, sw), jnp.where(low_half, sw, v)]
        return jnp.concatenate(out, axis=1)

    def block_diag(xdt):
        xb = xdt.astype(BF16)
        return jnp.concatenate([jnp.where(lane_head == r, xb, jnp.zeros_like(xb)) for r in range(r_heads)], axis=0)

    def expand(c, carry):
        rows = pl.ds(pl.multiple_of(c * q, q), q)
        ex_ref[rows, :] = _dot(p_ref[0, rows, :], sel_ref[...])
        cb_ref[c] = lax.dot_general(c_ref[0, rows, :], b_ref[0, rows, :], (((1,), (1,)), ((), ())),
                                    preferred_element_type=F32)
        return carry

    lax.fori_loop(0, nc, expand, 0, unroll=4 if nc % 4 == 0 else 2)

    def local(c, carry):
        rows = pl.ds(pl.multiple_of(c * q, q), q)
        x = xs_ref[0, rows, :].astype(F32)
        bm = b_ref[0, rows, :]
        dte_f, dte_b = ex_ref[rows, 0:gw], ex_ref[rows, gw:2 * gw]
        c64_f, c64_b = ex_ref[rows, 2 * gw:3 * gw], ex_ref[rows, 3 * gw:4 * gw]
        cw_f, cw_b = per_head_128(c64_f), per_head_128(c64_b)
        crow = crow_ref[0, c]
        flat_f = jnp.concatenate([crow[r:r + 1, :] for r in range(r_heads)], axis=1)
        flat_b = jnp.concatenate([crow[r_heads + r:r_heads + r + 1, :] for r in range(r_heads)], axis=1)
        cb4 = jnp.concatenate([cb_ref[c]] * r_heads, axis=1)
        l_f = (jnp.where(keep_f, jnp.exp2(cw_f - flat_f), 0.0) * cb4).astype(BF16)
        l_b = (jnp.where(keep_b, jnp.exp2(cw_b - flat_b), 0.0) * cb4).astype(BF16)
        xdt_f = x * dte_f
        xdt_b = x * dte_b
        y = _dot(jnp.concatenate([l_f, l_b], axis=1),
                 jnp.concatenate([block_diag(xdt_f), block_diag(xdt_b)], axis=0))
        yacc_ref[rows, :] = y + dsk * x
        edge_f = c64_f[q - 1:q, :]
        edge_b = c64_b[0:1, :]
        xw = jnp.concatenate([(xdt_f * jnp.exp2(edge_f - c64_f)).astype(BF16),
                              (xdt_b * jnp.exp2(edge_b - c64_b)).astype(BF16)], axis=1)
        st = lax.dot_general(bm, xw, (((0,), (0,)), ((), ())), preferred_element_type=F32)
        s_ref[0, c] = st[:, :gw]
        s_ref[1, c] = st[:, gw:]
        dec_ref[0, c] = jnp.broadcast_to(jnp.exp2(edge_f), (8, gw))
        dec_ref[1, c] = jnp.broadcast_to(jnp.exp2(edge_b), (8, gw))
        return carry

    lax.fori_loop(0, nc, local, 0, unroll=4 if nc % 4 == 0 else 2)

    h_ref[0] = h0_ref[0, 0, 0]
    h_ref[1] = h0_ref[0, 0, 1]

    def carry_states(t, carry):
        for d, c in ((0, t), (1, nc - 1 - t)):
            rows = pl.ds(pl.multiple_of(c * q, q), q)
            h = h_ref[d]
            yo = _dot(c_ref[0, rows, :], h.astype(BF16)) * jnp.exp2(ex_ref[rows, (2 + d) * gw:(3 + d) * gw])
            yacc_ref[rows, :] = yacc_ref[rows, :] + yo
            h_ref[d] = h * dec_ref[d, c, 0:1, :] + s_ref[d, c]
        return carry

    lax.fori_loop(0, nc, carry_states, 0, unroll=4 if nc % 4 == 0 else 2)
    hout_ref[0, 0, 0] = h_ref[0]
    hout_ref[0, 0, 1] = h_ref[1]

    def emit(c, carry):
        rows = pl.ds(pl.multiple_of(c * q, q), q)
        y_ref[0, rows, :] = yacc_ref[rows, :].astype(y_ref.dtype)
        return carry

    lax.fori_loop(0, nc, emit, 0)


def _ssd_call(wide, pieces, crow, seq_block, sel, dsk, h0, name):
    xbc = wide
    b, n, _ = wide.shape
    nc = n // CHUNK
    g = SSM_GROUPS
    x_col0 = D_INNER // GROUP_W
    b_col0 = 2 * D_INNER // D_STATE
    c_col0 = b_col0 + g
    return pl.pallas_call(
        functools.partial(_ssd_kernel, nc),
        out_shape=[jax.ShapeDtypeStruct((b, n, D_INNER), BF16),
                   jax.ShapeDtypeStruct((b, g, 2, D_STATE, GROUP_W), F32)],
        grid=(b, g),
        in_specs=[pl.BlockSpec((1, n, GROUP_W), lambda bi, gi: (bi, 0, x_col0 + gi)),
                  pl.BlockSpec((1, n, D_STATE), lambda bi, gi: (bi, 0, b_col0 + gi)),
                  pl.BlockSpec((1, n, D_STATE), lambda bi, gi: (bi, 0, c_col0 + gi)),
                  pl.BlockSpec((1, n, LANES), lambda bi, gi: (bi, seq_block, gi)),
                  pl.BlockSpec((1, nc, 2 * HEADS_PER_GROUP, CHUNK), lambda bi, gi: (bi, seq_block, gi, 0)),
                  pl.BlockSpec((LANES, EXP_W), lambda bi, gi: (0, 0)),
                  pl.BlockSpec((2, GROUP_W), lambda bi, gi: (0, gi)),
                  pl.BlockSpec((1, 1, 2, D_STATE, GROUP_W), lambda bi, gi: (bi, gi, 0, 0, 0))],
        out_specs=[pl.BlockSpec((1, n, GROUP_W), lambda bi, gi: (bi, 0, gi)),
                   pl.BlockSpec((1, 1, 2, D_STATE, GROUP_W), lambda bi, gi: (bi, gi, 0, 0, 0))],
        scratch_shapes=[pltpu.VMEM((n, GROUP_W), F32),
                        pltpu.VMEM((2, nc, D_STATE, GROUP_W), F32),
                        pltpu.VMEM((n, EXP_W), F32),
                        pltpu.VMEM((nc, CHUNK, CHUNK), F32),
                        pltpu.VMEM((2, nc, 8, GROUP_W), F32),
                        pltpu.VMEM((2, D_STATE, GROUP_W), F32)],
        compiler_params=_params("parallel", "parallel"),
        name=name,
    )(xbc, xbc, xbc, pieces, crow, sel, dsk, h0)


VT_ROWS = V_DIM + 16

KEY_CHUNK = 512
ATTN_HEADS = 4


def _attn_kernel(q_ref, k_ref, vt_ref, o_ref, s_bufs, cmaxs, acc, m_run):
    n_keys = k_ref.shape[1]
    blk = vt_ref.shape[3]
    per = KEY_CHUNK // blk
    n_full = n_keys // KEY_CHUNK
    tail = n_keys - n_full * KEY_CHUNK

    def produce_head(h, slot, c, n_rows):
        start = c * KEY_CHUNK
        rows = pl.ds(start if isinstance(c, int) else pl.multiple_of(start, KEY_CHUNK), n_rows)
        st = _dot(k_ref[0, rows, h * HEAD_PAD:(h + 1) * HEAD_PAD], q_ref[0, h * HEAD_PAD:(h + 1) * HEAD_PAD, :])
        s_bufs[slot][h, 0:n_rows, :] = st
        cmaxs[slot][h, 0:1, :] = jnp.max(st, axis=0, keepdims=True)

    def consume_head(h, slot, c, n_rows):
        m_old = m_run[h, 0:1, :]
        m_new = jnp.maximum(m_old, cmaxs[slot][h, 0:1, :])
        p = jnp.exp2(s_bufs[slot][h, 0:n_rows, :] - m_new).astype(BF16)
        vt = jnp.concatenate([vt_ref[0, per * c + j, h * VT_ROWS:(h + 1) * VT_ROWS, :]
                              for j in range(n_rows // blk)], axis=1)
        acc[h] = acc[h] * jnp.exp2(m_old - m_new) + _dot(vt, p)
        m_run[h, 0:1, :] = m_new

    def produce(slot, c, n_rows=KEY_CHUNK):
        for h in range(ATTN_HEADS):
            produce_head(h, slot, c, n_rows)

    def consume(slot, c, n_rows=KEY_CHUNK):
        for h in range(ATTN_HEADS):
            consume_head(h, slot, c, n_rows)

    def produce_consume(p_slot, p_c, c_slot, c_c, p_rows=KEY_CHUNK, c_rows=KEY_CHUNK):
        for h in range(ATTN_HEADS):
            produce_head(h, p_slot, p_c, p_rows)
            consume_head(h, c_slot, c_c, c_rows)

    m_run[...] = jnp.full(m_run.shape, -jnp.inf, F32)
    acc[...] = jnp.zeros(acc.shape, F32)
    n_chunks = n_full + (1 if tail else 0)
    size = lambda c: tail if c == n_full else KEY_CHUNK
    produce(0, 0)

    def body(j, carry):
        produce_consume(1, 2 * j + 1, 0, 2 * j)
        produce_consume(0, 2 * j + 2, 1, 2 * j + 1)
        return carry

    trips = (n_full - 1) // 2
    lax.fori_loop(0, trips, body, 0)
    for c in range(2 * trips, n_chunks):
        slot = c % 2
        if c + 1 < n_chunks:
            produce_consume(1 - slot, c + 1, slot, c, size(c + 1), size(c))
        else:
            consume(slot, c, size(c))
    outs = [acc[h, 0:V_DIM, :] / acc[h, V_DIM:V_DIM + 1, :] for h in range(ATTN_HEADS)]
    o_ref[0] = jnp.concatenate(outs, axis=0).T.astype(o_ref.dtype)


def _attn_call(q, k, vt, n, tq, name):
    b = q.shape[0]
    s = k.shape[1]
    n_blk, _, blk = vt.shape[1:]
    nh = ATTN_HEADS
    return pl.pallas_call(
        _attn_kernel,
        out_shape=jax.ShapeDtypeStruct((b, n, MLA_HEADS * V_DIM), BF16),
        grid=(b, MLA_HEADS // nh, n // tq),
        in_specs=[pl.BlockSpec((1, nh * HEAD_PAD, tq), lambda bi, hp, i: (bi, hp, i)),
                  pl.BlockSpec((1, s, nh * HEAD_PAD), lambda bi, hp, i: (bi, 0, hp)),
                  pl.BlockSpec((1, n_blk, nh * VT_ROWS, blk), lambda bi, hp, i: (bi, 0, hp, 0))],
        out_specs=pl.BlockSpec((1, tq, nh * V_DIM), lambda bi, hp, i: (bi, i, hp)),
        scratch_shapes=[[pltpu.VMEM((nh, KEY_CHUNK, tq), F32)] * 2,
                        [pltpu.VMEM((nh, 8, tq), F32)] * 2,
                        pltpu.VMEM((nh, VT_ROWS, tq), F32),
                        pltpu.VMEM((nh, 8, tq), F32)],
        compiler_params=_params("parallel", "parallel", "parallel"),
        name=name,
    )(q, k, vt)


def _tail_kernel(ff_chunk, att_ref, y_ref, z_ref, ga_ref, gb_ref, x_ref, mod_ref, sg_ref, g2_ref,
                 wa_ref, wb_ref, wo_ref, w1_ref, w2_ref, o_ref):
    y = y_ref[0].astype(F32)
    z = z_ref[0].astype(F32)
    u = y * (z * _sigmoid(z))
    gw = D_INNER // SSM_GROUPS
    parts = []
    for g in range(SSM_GROUPS):
        ug = u[:, g * gw:(g + 1) * gw]
        parts.append(ug * lax.rsqrt(jnp.mean(ug * ug, axis=-1, keepdims=True) + EPS))
    un = (jnp.concatenate(parts, axis=1) * sg_ref[...]).astype(BF16)
    a = _dot(att_ref[0], wa_ref[...])
    bb = _dot(un, wb_ref[...])
    merged = _sigmoid(ga_ref[0].astype(F32)) * a + _sigmoid(gb_ref[0].astype(F32)) * bb
    o = _dot(merged.astype(BF16), wo_ref[...])
    x = x_ref[0] + mod_ref[0, 2:3, :] * o

    ms = jnp.mean(x * x, axis=-1, keepdims=True)
    xn = x * lax.rsqrt(ms + EPS) * g2_ref[...]
    h = (xn * (1.0 + mod_ref[0, 4:5, :]) + mod_ref[0, 3:4, :]).astype(BF16)
    acc = None
    for c in range(D_FF // ff_chunk):
        a = _dot(h, w1_ref[:, c * ff_chunk:(c + 1) * ff_chunk])
        a = jnp.maximum(a, 0.0)
        part = _dot((a * a).astype(BF16), w2_ref[c * ff_chunk:(c + 1) * ff_chunk, :])
        acc = part if acc is None else acc + part
    o_ref[0] = x + mod_ref[0, 5:6, :] * acc


def _tail_call(att, y, wide, x, mod, sg, g2, wa, wb, wo, w1, w2, tm, name):
    b, n, d = x.shape
    ga_blk = (D_INNER + CONV_DIM) // d
    full = lambda a: pl.BlockSpec(a.shape, lambda bi, i: (0,) * a.ndim)
    resident = lambda a: pl.BlockSpec(a.shape, lambda bi, i: (0,) * a.ndim, pipeline_mode=pl.Buffered(1))
    return pl.pallas_call(
        functools.partial(_tail_kernel, FF_CHUNK),
        out_shape=jax.ShapeDtypeStruct((b, n, d), F32),
        grid=(b, n // tm),
        in_specs=[pl.BlockSpec((1, tm, MLA_HEADS * V_DIM), lambda bi, i: (bi, i, 0)),
                  pl.BlockSpec((1, tm, D_INNER), lambda bi, i: (bi, i, 0)),
                  pl.BlockSpec((1, tm, D_INNER), lambda bi, i: (bi, i, 0)),
                  pl.BlockSpec((1, tm, d), lambda bi, i: (bi, i, ga_blk)),
                  pl.BlockSpec((1, tm, d), lambda bi, i: (bi, i, ga_blk + 1)),
                  pl.BlockSpec((1, tm, d), lambda bi, i: (bi, i, 0)),
                  pl.BlockSpec((1, 8, d), lambda bi, i: (bi, 0, 0)),
                  full(sg), full(g2), resident(wa), resident(wb), resident(wo), resident(w1), resident(w2)],
        out_specs=pl.BlockSpec((1, tm, d), lambda bi, i: (bi, i, 0)),
        compiler_params=_params("parallel", "parallel"),
        name=name,
    )(att, y, wide, wide, wide, x, mod, sg, g2, wa, wb, wo, w1, w2)


_ROPE_SWAP = np.concatenate([np.arange(8, 16), np.arange(0, 8), np.arange(24, 32), np.arange(16, 24)])
_DT_LANE_ORDER = np.array([d * SSM_HEADS + g * HEADS_PER_GROUP + r for g in range(SSM_GROUPS) for d in range(2)
                           for r in range(HEADS_PER_GROUP)])


def _rope_tables(n_lat):
    rows = n_lat // GRID_W
    row = jnp.repeat(jnp.arange(rows), GRID_W)
    col = jnp.tile(jnp.arange(GRID_W), rows)
    freqs = ROPE_BASE ** (-jnp.arange(ROPE_PAIRS, dtype=F32) / ROPE_PAIRS)
    ang = jnp.stack([row, col], axis=-1).astype(F32)[..., None] * freqs
    cos, sin = jnp.cos(ang), jnp.sin(ang)
    cos32 = jnp.stack([cos, cos], axis=2).reshape(n_lat, QK_ROPE)
    sin32 = jnp.stack([-sin, sin], axis=2).reshape(n_lat, QK_ROPE)
    zeros = jnp.zeros((n_lat, QK_NOPE), F32)
    return (jnp.concatenate([zeros, cos32, sin32], axis=1),
            jnp.concatenate([zeros, sin32, cos32], axis=1))


def _identity_rope_tables(n):
    zeros = jnp.zeros((n, QK_NOPE), F32)
    one = jnp.ones((n, QK_ROPE), F32)
    zero = jnp.zeros((n, QK_ROPE), F32)
    return jnp.concatenate([zeros, one, zero], axis=1), jnp.concatenate([zeros, zero, one], axis=1)


def _layout_params(w_in, w_uq, w_ukv, q_a_g, kv_a_g, qk_q_g, qk_k_g, dt_bias, a_log):
    d = w_in.shape[0]
    o_cq, o_ckv, o_kr = 0, Q_LORA, Q_LORA + KV_LORA
    o_z = o_kr + QK_ROPE
    o_xbc = o_z + D_INNER
    o_dt = o_xbc + CONV_DIM
    o_ga = o_dt + 2 * SSM_HEADS
    kr = w_in[:, o_kr:o_kr + QK_ROPE]
    kr_sw = kr[:, _ROPE_SWAP]
    z64 = jnp.zeros((d, LANES - 2 * QK_ROPE), F32)
    w_narrow = jnp.concatenate(
        [w_in[:, o_cq:o_kr], z64, kr, kr_sw, z64, kr_sw, kr, w_in[:, o_dt:o_ga][:, _DT_LANE_ORDER], z64],
        axis=1).astype(BF16)
    w_wide = jnp.concatenate([w_in[:, o_z:o_dt], w_in[:, o_ga:]], axis=1).astype(BF16)

    hq = w_uq.reshape(Q_LORA, MLA_HEADS, QK_NOPE + QK_ROPE)
    rope_q = hq[:, :, QK_NOPE:]
    wuq = jnp.concatenate([hq[:, :, :QK_NOPE], rope_q, rope_q[:, :, _ROPE_SWAP]], axis=2)
    wuq = wuq.reshape(Q_LORA, MLA_HEADS * HEAD_PAD).astype(BF16)
    hkv = w_ukv.reshape(KV_LORA, MLA_HEADS, QK_NOPE + V_DIM)
    wk = jnp.concatenate([hkv[:, :, :QK_NOPE], jnp.zeros((KV_LORA, MLA_HEADS, HEAD_PAD - QK_NOPE), F32)], axis=2)
    wukv = jnp.concatenate([wk.reshape(KV_LORA, MLA_HEADS * HEAD_PAD),
                            hkv[:, :, QK_NOPE:].reshape(KV_LORA, MLA_HEADS * V_DIM)], axis=1).astype(BF16)

    gq_r = qk_q_g[QK_NOPE:]
    gq = jnp.tile(jnp.concatenate([qk_q_g[:QK_NOPE], gq_r, gq_r[_ROPE_SWAP]]), MLA_HEADS)[None, :]
    gk = jnp.tile(jnp.concatenate([qk_k_g[:QK_NOPE], jnp.zeros((HEAD_PAD - QK_NOPE,), F32)]), MLA_HEADS)[None, :]
    gk_r = qk_k_g[QK_NOPE:]
    z64v = jnp.zeros((QK_NOPE,), F32)
    ga = jnp.concatenate([z64v, gk_r, gk_r[_ROPE_SWAP]])[None, :]
    gb = jnp.concatenate([z64v, gk_r[_ROPE_SWAP], gk_r])[None, :]
    lane_pad = jnp.zeros((LANES - 2 * SSM_HEADS,), F32)
    dtb = jnp.concatenate([dt_bias.reshape(-1)[_DT_LANE_ORDER], lane_pad])[None, :]
    alane = jnp.concatenate([a_log.astype(F32).reshape(-1)[_DT_LANE_ORDER], lane_pad])[None, :]

    lane = np.arange(MLA_HEADS * HEAD_PAD)
    head, off = lane // HEAD_PAD, lane % HEAD_PAD
    e = np.zeros((MLA_HEADS * HEAD_PAD, LANES), np.float32)
    e[lane[off < QK_NOPE], 2 * head[off < QK_NOPE]] = 1.0 / QK_NOPE
    rope = (off >= QK_NOPE) & (off < QK_NOPE + QK_ROPE)
    e[lane[rope], 2 * head[rope] + 1] = 1.0 / QK_ROPE
    et = np.zeros((LANES, MLA_HEADS * HEAD_PAD), np.float32)
    et[2 * head[off < QK_NOPE], lane[off < QK_NOPE]] = 1.0
    et[2 * head[off >= QK_NOPE] + 1, lane[off >= QK_NOPE]] = 1.0
    et = np.concatenate([et, et], axis=0)
    consts = (q_a_g[None, :], kv_a_g[None, :], wuq, wukv, jnp.asarray(e, BF16), jnp.asarray(et, BF16),
              gq, gk, ga, gb, dtb, alane, _piece_permutation())
    return w_narrow, w_wide, consts


def _expansion_selector():
    r = HEADS_PER_GROUP
    sel = np.zeros((LANES, EXP_W), np.float32)
    for piece in range(3):
        for d in range(2):
            for h in range(r):
                row_dt = piece * 4 * r + d * r + h
                row_cum = piece * 4 * r + 2 * r + d * r + h
                sel[row_dt, d * GROUP_W + h * SSM_HEAD_DIM:d * GROUP_W + (h + 1) * SSM_HEAD_DIM] = 1.0
                base = 2 * GROUP_W + d * GROUP_W + h * SSM_HEAD_DIM
                sel[row_cum, base:base + SSM_HEAD_DIM] = 1.0
    return jnp.asarray(sel, BF16)


def kernel(x, c, ctx, c_ctx, norm1_g, norm2_g, w_mod, b_mod, w_in, q_a_g, w_uq, kv_a_g, w_ukv, qk_q_g, qk_k_g,
           conv_w, conv_b, dt_bias, a_log, d_skip, ssm_norm_g, w_proj_a, w_proj_b, w_out, w_mlp1, w_mlp2):
    assert w_mod.shape[0] == 1, "single-layer block"
    b, n_lat, d = x.shape
    n_ctx = ctx.shape[1]

    cvec = jnp.concatenate([c, c_ctx[None, :], jnp.zeros((8 - b - 1, d), F32)], axis=0)
    mod = _mod_call(cvec, w_mod[0], b_mod[0][None, :]).reshape(8, N_MOD, d)
    mod = jnp.concatenate([mod, jnp.zeros((8, 8 - N_MOD, d), F32)], axis=1)
    mod_x, mod_c = mod[:b], mod[b:b + 1]

    w_narrow, w_wide, consts = _layout_params(w_in[0], w_uq[0], w_ukv[0], q_a_g[0], kv_a_g[0], qk_q_g[0],
                                              qk_k_g[0], dt_bias[0], a_log[0])
    g1 = norm1_g[0][None, :]
    conv_wl, conv_bl = conv_w[0], conv_b[0][None, :]
    dsk = jnp.repeat(d_skip[0], SSM_HEAD_DIM, axis=1)
    sel = _expansion_selector()

    narrow_c, wide_c = _front_call(ctx, g1, mod_c, w_narrow, w_wide, conv_wl, conv_bl, n_ctx, True, "front_ctx")
    narrow_x, wide_x = _front_call(x, g1, mod_x, w_narrow, w_wide, conv_wl, conv_bl, FRONT_TM, False, "front")

    t1c, t2c = _identity_rope_tables(n_ctx)
    t1x, t2x = _rope_tables(n_lat)
    q_all, k_all, vt_all, crow, pieces = _mla_call(narrow_c, narrow_x, jnp.concatenate([t1x, t1c], axis=0),
                                                   jnp.concatenate([t2x, t2c], axis=0), consts, "mla")

    h_zero = jnp.zeros((b, SSM_GROUPS, 2, D_STATE, GROUP_W), F32)
    _, h_ctx = _ssd_call(wide_c, pieces, crow, n_lat // n_ctx, sel, dsk, h_zero, "ssd_ctx")
    y_x, _ = _ssd_call(wide_x, pieces, crow, 0, sel, dsk, h_ctx, "ssd")

    att = _attn_call(q_all, k_all, vt_all, n_lat, ATTN_TQ, "attn")

    return _tail_call(att, y_x, wide_x, x, mod_x, ssm_norm_g[0][None, :], norm2_g[0][None, :],
                      w_proj_a[0].astype(BF16), w_proj_b[0].astype(BF16), w_out[0].astype(BF16),
                      w_mlp1[0].astype(BF16), w_mlp2[0].astype(BF16), TAIL_TM, "tail")
```
